```python
import math, functools
import jax, jax.numpy as jnp
from jax import lax
import numpy as np

D_MODEL = 2048
BATCH = 4
SEQ = 2048
DEPTH = 2
DEC_BATCH = 8
DEC_SEQ = 8
PAST_LEN = 16384
PAGE_SIZE = 128

HEAD_DIM = 128
ATT_HEADS = (3 * D_MODEL) // (8 * HEAD_DIM)
ATT_W = ATT_HEADS * HEAD_DIM
IDX_HEADS = 16
IDX_DIM = 64
TOPK_MAX = 256
QBLOCK = 128
SSM_GROUP = 16
SSM_W = (3 * D_MODEL) // 8
SSM_GROUPS = SSM_W // SSM_GROUP
SSM_STATE = 64
CROSS_HEADS = 4
CROSS_HD = D_MODEL // 16
CROSS_W = CROSS_HEADS * CROSS_HD
MEM_LEN = 256
REL_BUCKETS = 32
REL_MAX_DIST = 128
N_BRANCHES = 3
EPS = 1e-6
_SPLIT_WIDTHS = (ATT_W, ATT_W, ATT_W, ATT_W, IDX_HEADS * IDX_DIM, IDX_HEADS, IDX_DIM,
                 SSM_W, SSM_W, CROSS_W, CROSS_W, N_BRANCHES * D_MODEL)
IN_WIDTH = sum(_SPLIT_WIDTHS)
SPLIT_POINTS = tuple(int(c) for c in np.cumsum(_SPLIT_WIDTHS)[:-1])

kernel_name = 'gated_dsa_s5_memory_decoder_step'


def rms_norm(x, g):
    x32 = x.astype(jnp.float32)
    y = x32 * lax.rsqrt(jnp.mean(x32 * x32, axis=-1, keepdims=True) + EPS)
    return (y * g.astype(jnp.float32)).astype(x.dtype)


def t5_bucket(dist):
    n = jnp.maximum(dist, 0)
    max_exact = REL_BUCKETS // 2
    nf = jnp.maximum(n, 1).astype(jnp.float32)
    large = max_exact + (jnp.log(nf / max_exact) / math.log(REL_MAX_DIST / max_exact)
                         * (REL_BUCKETS - max_exact)).astype(jnp.int32)
    large = jnp.minimum(large, REL_BUCKETS - 1)
    return jnp.where(n < max_exact, n, large)


def dsa_select(qi, wi, kidx, q_pos, topk):
    s = jnp.einsum('bqhd,bld->bqhl', qi, kidx)
    score = jnp.einsum('bqhl,bqh->bql', jax.nn.relu(s).astype(jnp.float32), wi.astype(jnp.float32))
    key_pos = jnp.arange(kidx.shape[1])
    admissible = key_pos[None, None, :] <= q_pos[None, :, None]
    score = jnp.where(admissible, score, -jnp.inf)
    _, idx = lax.top_k(score, topk)
    valid = idx <= q_pos[None, :, None]
    return idx, valid


def sparse_softmax(q, k_sel, v_sel, idx, valid, q_pos, rel_bias):
    logits = jnp.einsum('bqhd,bqkhd->bqhk', q, k_sel).astype(jnp.float32) * (HEAD_DIM ** -0.5)
    bias = rel_bias[t5_bucket(q_pos[None, :, None] - idx)]
    logits = logits + jnp.moveaxis(bias, -1, 2).astype(jnp.float32)
    logits = jnp.where(valid[:, :, None, :], logits, -jnp.inf)
    p = jax.nn.softmax(logits, axis=-1).astype(v_sel.dtype)
    return jnp.einsum('bqhk,bqkhd->bqhd', p, v_sel)


def dsa_prompt(rel_bias, q, k, v, qi, wi, kidx):
    B, S = q.shape[0], q.shape[1]
    nb = S // QBLOCK
    topk = min(TOPK_MAX, S // 4)
    bidx = jnp.arange(B)[:, None, None]

    def blocks(a):
        return jnp.moveaxis(a.reshape((B, nb, QBLOCK) + a.shape[2:]), 1, 0)

    def one_block(args):
        qb, qib, wb, t0 = args
        q_pos = t0 + jnp.arange(QBLOCK)
        idx, valid = dsa_select(qib, wb, kidx, q_pos, topk)
        return sparse_softmax(qb, k[bidx, idx], v[bidx, idx], idx, valid, q_pos, rel_bias)

    out = lax.map(one_block, (blocks(q), blocks(qi), blocks(wi), jnp.arange(nb) * QBLOCK))
    return jnp.moveaxis(out, 0, 1).reshape(B, S, ATT_HEADS, HEAD_DIM)


def dsa_sample(pool_k, pool_v, pool_kidx, page_table, rel_bias, q, k_new, v_new, qi, wi, kidx_new):
    DB, T = q.shape[0], q.shape[1]
    page = pool_k.shape[1]
    past = page_table.shape[1] * page
    kidx_past = pool_kidx[page_table].reshape(DB, past, IDX_DIM)
    kidx_all = jnp.concatenate([kidx_past, kidx_new.astype(kidx_past.dtype)], axis=1)
    topk = min(TOPK_MAX, (past + T) // 4)
    q_pos = past + jnp.arange(T)
    idx, valid = dsa_select(qi, wi, kidx_all, q_pos, topk)
    bidx = jnp.arange(DB)[:, None, None]
    pc = jnp.minimum(idx, past - 1)
    phys = page_table[bidx, pc // page]
    off = pc % page
    nc = jnp.clip(idx - past, 0, T - 1)
    is_new = (idx >= past)[..., None, None]
    k_sel = jnp.where(is_new, k_new[bidx, nc].astype(pool_k.dtype), pool_k[phys, off])
    v_sel = jnp.where(is_new, v_new[bidx, nc].astype(pool_v.dtype), pool_v[phys, off])
    return sparse_softmax(q, k_sel, v_sel, idx, valid, q_pos, rel_bias)


def memory_kv(mem, w_mem_kv):
    B, M = mem.shape[0], mem.shape[1]
    k, v = jnp.split(mem @ w_mem_kv, 2, axis=-1)
    return k.reshape(B, M, CROSS_HEADS, CROSS_HD), v.reshape(B, M, CROSS_HEADS, CROSS_HD)


def cross_attend(q, mem_k, mem_v):
    B, S = q.shape[0], q.shape[1]
    logits = jnp.einsum('bshd,bmhd->bhsm', q, mem_k.astype(q.dtype)).astype(jnp.float32) * (CROSS_HD ** -0.5)
    p = jax.nn.softmax(logits, axis=-1).astype(q.dtype)
    return jnp.einsum('bhsm,bmhd->bshd', p, mem_v.astype(q.dtype)).reshape(B, S, CROSS_W)


def _ssm_combine(e1, e2):
    a1r, a1i, b1r, b1i = e1
    a2r, a2i, b2r, b2i = e2
    return (a2r * a1r - a2i * a1i,
            a2r * a1i + a2i * a1r,
            a2r * b1r - a2i * b1i + b2r,
            a2r * b1i + a2i * b1r + b2i)


def s5_branch(u, h0_re, h0_im, a_re, a_im, b_re, b_im, c_re, c_im, d_skip, log_dt, w_glu, b_glu):
    f32 = jnp.float32
    B, S = u.shape[0], u.shape[1]
    u32 = u.astype(f32)
    ug = u32.reshape(B, S, SSM_GROUPS, SSM_GROUP)
    lam_re = jnp.minimum(a_re.astype(f32), -1e-4)
    lam_im = a_im.astype(f32)
    step = jnp.exp(log_dt.astype(f32))[:, None]
    mag = jnp.exp(lam_re * step)
    ang = lam_im * step
    lb_re, lb_im = mag * jnp.cos(ang), mag * jnp.sin(ang)
    nr, ni = lb_re - 1.0, lb_im
    den = lam_re * lam_re + lam_im * lam_im
    coef_re = (nr * lam_re + ni * lam_im) / den
    coef_im = (ni * lam_re - nr * lam_im) / den
    br, bi = b_re.astype(f32), b_im.astype(f32)
    bb_re = coef_re[..., None] * br - coef_im[..., None] * bi
    bb_im = coef_re[..., None] * bi + coef_im[..., None] * br
    bu_re = jnp.einsum('bsgc,gpc->bsgp', ug, bb_re)
    bu_im = jnp.einsum('bsgc,gpc->bsgp', ug, bb_im)
    h0r, h0i = h0_re.astype(f32), h0_im.astype(f32)
    bu_re = bu_re.at[:, 0].add(lb_re * h0r - lb_im * h0i)
    bu_im = bu_im.at[:, 0].add(lb_re * h0i + lb_im * h0r)
    a_r = jnp.broadcast_to(lb_re, bu_re.shape)
    a_i = jnp.broadcast_to(lb_im, bu_im.shape)
    _, _, x_re, x_im = lax.associative_scan(_ssm_combine, (a_r, a_i, bu_re, bu_im), axis=1)
    y = (jnp.einsum('gcp,bsgp->bsgc', c_re.astype(f32), x_re)
         - jnp.einsum('gcp,bsgp->bsgc', c_im.astype(f32), x_im)).reshape(B, S, SSM_W)
    y = y + d_skip.astype(f32) * u32
    z = jax.nn.gelu(y)
    ga, gb = jnp.split(z @ w_glu.astype(f32) + b_glu.astype(f32), 2, axis=-1)
    out = ga * jax.nn.sigmoid(gb)
    return out.astype(u.dtype), x_re[:, -1], x_im[:, -1]


def mixer_layer(x, attend, h0_re, h0_im, mem_k, mem_v, norm_g, w_in, w_ba, w_bs, w_bc, w_out, ssm_params):
    B, S = x.shape[0], x.shape[1]
    h = rms_norm(x, norm_g)
    q, k, v, g_a, qi, wi, ki, u, g_s, qc, g_c, g_m = jnp.split(h @ w_in, SPLIT_POINTS, axis=-1)
    q = q.reshape(B, S, ATT_HEADS, HEAD_DIM)
    k = k.reshape(B, S, ATT_HEADS, HEAD_DIM)
    v = v.reshape(B, S, ATT_HEADS, HEAD_DIM)
    qi = qi.reshape(B, S, IDX_HEADS, IDX_DIM)
    wi = wi * (IDX_HEADS ** -0.5)
    a_out = attend(q, k, v, qi, wi, ki).reshape(B, S, ATT_W) * jax.nn.silu(g_a)
    s_out, h_re, h_im = s5_branch(u, h0_re, h0_im, *ssm_params)
    s_out = s_out * jax.nn.silu(g_s)
    c_out = cross_attend(qc.reshape(B, S, CROSS_HEADS, CROSS_HD), mem_k, mem_v) * jax.nn.silu(g_c)
    gates = jax.nn.sigmoid(g_m).reshape(B, S, N_BRANCHES, D_MODEL)
    merged = (gates[:, :, 0] * (a_out @ w_ba) + gates[:, :, 1] * (s_out @ w_bs)
              + gates[:, :, 2] * (c_out @ w_bc))
    return x + merged @ w_out, (k, v, ki, h_re, h_im)


def setup_inputs(seed: int = 0) -> dict:
    key = jax.random.key(seed)
    ks = jax.random.split(key, 32)
    f32 = jnp.float32
    n_pages = PAST_LEN // PAGE_SIZE
    n_used = DEC_BATCH * n_pages
    n_pool = n_used + max(1, n_used // 4)
    nrm = lambda k, shape, s=1.0: jax.random.normal(k, shape, f32) * s
    page_table = jax.random.permutation(ks[0], n_pool)[:n_used].reshape(DEC_BATCH, n_pages).astype(jnp.int32)
    a_im = (jnp.pi * jnp.arange(SSM_STATE, dtype=f32))[None, None, :] + nrm(ks[20], (DEPTH, SSM_GROUPS, SSM_STATE), 0.01)
    return {
        'x_prompt': nrm(ks[1], (BATCH, SEQ, D_MODEL)),
        'x_sample': nrm(ks[2], (DEC_BATCH, DEC_SEQ, D_MODEL)),
        'cache_k': nrm(ks[3], (DEPTH, n_pool, PAGE_SIZE, ATT_HEADS, HEAD_DIM)),
        'cache_v': nrm(ks[4], (DEPTH, n_pool, PAGE_SIZE, ATT_HEADS, HEAD_DIM)),
        'cache_kidx': nrm(ks[5], (DEPTH, n_pool, PAGE_SIZE, IDX_DIM)),
        'cache_mem_k': nrm(ks[6], (DEPTH, DEC_BATCH, MEM_LEN, CROSS_HEADS, CROSS_HD)),
        'cache_mem_v': nrm(ks[7], (DEPTH, DEC_BATCH, MEM_LEN, CROSS_HEADS, CROSS_HD)),
        'state_ssm_re': nrm(ks[8], (DEPTH, DEC_BATCH, SSM_GROUPS, SSM_STATE), 0.5),
        'state_ssm_im': nrm(ks[9], (DEPTH, DEC_BATCH, SSM_GROUPS, SSM_STATE), 0.5),
        'page_table': page_table,
        'mem_prompt': nrm(ks[10], (BATCH, MEM_LEN, D_MODEL)),
        'norm_g': 1.0 + nrm(ks[11], (DEPTH, D_MODEL), 0.02),
        'w_in': nrm(ks[12], (DEPTH, D_MODEL, IN_WIDTH), D_MODEL ** -0.5),
        'w_branch_attn': nrm(ks[13], (DEPTH, ATT_W, D_MODEL), ATT_W ** -0.5),
        'w_branch_ssm': nrm(ks[14], (DEPTH, SSM_W, D_MODEL), SSM_W ** -0.5),
        'w_branch_cross': nrm(ks[15], (DEPTH, CROSS_W, D_MODEL), CROSS_W ** -0.5),
        'w_out': nrm(ks[16], (DEPTH, D_MODEL, D_MODEL), D_MODEL ** -0.5),
        'w_mem_kv': nrm(ks[17], (DEPTH, D_MODEL, 2 * CROSS_W), D_MODEL ** -0.5),
        'ssm_a_re': -0.5 * jnp.exp(nrm(ks[18], (DEPTH, SSM_GROUPS, SSM_STATE), 0.05)),
        'ssm_a_im': a_im,
        'ssm_b_re': nrm(ks[19], (DEPTH, SSM_GROUPS, SSM_STATE, SSM_GROUP), (2 * SSM_GROUP) ** -0.5),
        'ssm_b_im': nrm(ks[21], (DEPTH, SSM_GROUPS, SSM_STATE, SSM_GROUP), (2 * SSM_GROUP) ** -0.5),
        'ssm_c_re': nrm(ks[22], (DEPTH, SSM_GROUPS, SSM_GROUP, SSM_STATE), SSM_STATE ** -0.5),
        'ssm_c_im': nrm(ks[23], (DEPTH, SSM_GROUPS, SSM_GROUP, SSM_STATE), SSM_STATE ** -0.5),
        'ssm_d': nrm(ks[24], (DEPTH, SSM_W)),
        'ssm_log_dt': jax.random.uniform(ks[25], (DEPTH, SSM_GROUPS), f32, math.log(1e-3), math.log(1e-1)),
        'w_glu': nrm(ks[26], (DEPTH, SSM_W, 2 * SSM_W), SSM_W ** -0.5),
        'b_glu': nrm(ks[27], (DEPTH, 2 * SSM_W), 0.01),
        'rel_bias': nrm(ks[28], (REL_BUCKETS, ATT_HEADS), 0.5),
        'final_norm_g': 1.0 + nrm(ks[29], (D_MODEL,), 0.02),
    }


def reference(x_prompt, x_sample, cache_k, cache_v, cache_kidx, cache_mem_k, cache_mem_v,
              state_ssm_re, state_ssm_im, page_table, mem_prompt, norm_g, w_in, w_branch_attn,
              w_branch_ssm, w_branch_cross, w_out, w_mem_kv, ssm_a_re, ssm_a_im, ssm_b_re, ssm_b_im,
              ssm_c_re, ssm_c_im, ssm_d, ssm_log_dt, w_glu, b_glu, rel_bias, final_norm_g):
    yp, ys = x_prompt, x_sample
    B = x_prompt.shape[0]
    zeros_h = jnp.zeros((B, SSM_GROUPS, SSM_STATE), jnp.float32)
    kp_l, vp_l, kip_l, mkp_l, mvp_l, srp_l, sip_l = [], [], [], [], [], [], []
    ks_l, vs_l, kis_l, srs_l, sis_l = [], [], [], [], []
    attend_p = functools.partial(dsa_prompt, rel_bias)
    for l in range(DEPTH):
        layer_w = (norm_g[l], w_in[l], w_branch_attn[l], w_branch_ssm[l], w_branch_cross[l], w_out[l])
        ssm_l = (ssm_a_re[l], ssm_a_im[l], ssm_b_re[l], ssm_b_im[l], ssm_c_re[l], ssm_c_im[l],
                 ssm_d[l], ssm_log_dt[l], w_glu[l], b_glu[l])
        mk_p, mv_p = memory_kv(mem_prompt, w_mem_kv[l])
        yp, (kp, vp, kip, srp, sip) = mixer_layer(yp, attend_p, zeros_h, zeros_h, mk_p, mv_p, *layer_w, ssm_l)
        kp_l.append(kp); vp_l.append(vp); kip_l.append(kip); mkp_l.append(mk_p); mvp_l.append(mv_p)
        srp_l.append(srp); sip_l.append(sip)
        attend_s = functools.partial(dsa_sample, cache_k[l], cache_v[l], cache_kidx[l], page_table, rel_bias)
        ys, (kn, vn, kin, srs, sis) = mixer_layer(ys, attend_s, state_ssm_re[l], state_ssm_im[l],
                                                  cache_mem_k[l], cache_mem_v[l], *layer_w, ssm_l)
        ks_l.append(kn); vs_l.append(vn); kis_l.append(kin); srs_l.append(srs); sis_l.append(sis)
    y_prompt = rms_norm(yp, final_norm_g)
    y_sample = rms_norm(ys, final_norm_g)
    st = lambda xs: jnp.stack(xs, axis=0)
    return (y_prompt, y_sample, st(kp_l), st(vp_l), st(kip_l), st(mkp_l), st(mvp_l), st(srp_l), st(sip_l),
            st(ks_l), st(vs_l), st(kis_l), st(srs_l), st(sis_l))
```

```python
import functools
import math

import jax
import jax.numpy as jnp
import numpy as np
from jax import lax
from jax.experimental import pallas as pl
from jax.experimental.pallas import tpu as pltpu

F32 = jnp.float32
BF16 = jnp.bfloat16
I32 = jnp.int32
HIGHEST = lax.Precision.HIGHEST
NT_DIMS = (((1,), (1,)), ((), ()))

VMEM_LIMIT_BYTES = 56 * 1024 * 1024
LANES = 128
SUBLANES = 8

D_MODEL = 2048
HEAD_DIM = 128
ATT_HEADS = 6
ATT_W = ATT_HEADS * HEAD_DIM
IDX_HEADS = 16
IDX_DIM = 64
IDX_W = IDX_HEADS * IDX_DIM
TOPK_MAX = 256
QBLOCK = 128
SSM_GROUP = 16
SSM_W = 768
SSM_GROUPS = SSM_W // SSM_GROUP
SSM_STATE = 64
CROSS_HEADS = 4
CROSS_HD = 128
CROSS_W = CROSS_HEADS * CROSS_HD
REL_BUCKETS = 32
REL_MAX_DIST = 128
N_BRANCHES = 3
EPS = 1e-6
SPLIT_WIDTHS = (ATT_W, ATT_W, ATT_W, ATT_W, IDX_W, IDX_HEADS, IDX_DIM,
                SSM_W, SSM_W, CROSS_W, CROSS_W, N_BRANCHES * D_MODEL)
SPLIT_POINTS = tuple(int(c) for c in np.cumsum(SPLIT_WIDTHS)[:-1])

COL_GM = 0
COL_Q = 6144
COL_K = COL_Q + ATT_W
COL_V = COL_K + ATT_W
COL_GA = COL_V + ATT_W
COL_U = COL_GA + ATT_W
COL_GS = COL_U + SSM_W
COL_QC = COL_GS + SSM_W
COL_GC = COL_QC + CROSS_W
COL_KIWI = COL_GC + CROSS_W
COL_QI = 12288
PROJ_W = COL_QI + IDX_W
PROJ_TN = 512
PROJ_LO_TILES = COL_KIWI // PROJ_TN

SLAB_GROUPS = LANES // SSM_GROUP
SLAB_STATES = SLAB_GROUPS * SSM_STATE
N_SLABS = SSM_GROUPS // SLAB_GROUPS
SCAN_ROWS = SUBLANES

PAGES_PER_STEP = 4

INT_MIN = -2 ** 31


def _params(*sem):
    return pltpu.CompilerParams(dimension_semantics=sem, vmem_limit_bytes=VMEM_LIMIT_BYTES)


def _sigmoid(x):
    return 1.0 / (1.0 + jnp.exp(-x))


def _silu(x):
    return x * _sigmoid(x)


def _gelu_tanh(x):
    return 0.5 * x * (1.0 + jnp.tanh(math.sqrt(2.0 / math.pi) * (x + 0.044715 * (x * x * x))))


def _dot(a, b, hi):
    if hi:
        return jnp.dot(a.astype(F32), b.astype(F32), precision=HIGHEST, preferred_element_type=F32)
    return jnp.dot(a.astype(BF16), b.astype(BF16), preferred_element_type=F32)


def _split_bf16(x):
    hi = x.astype(BF16).astype(F32)
    return hi, x - hi


def _sortable_key(score):
    b = lax.bitcast_convert_type(score, I32)
    b = jnp.where(b == I32(INT_MIN), I32(0), b)
    key = b ^ ((b >> 31) & I32(0x7FFFFFFF))
    return jnp.where(score == -jnp.inf, I32(INT_MIN), key)


def _kth_largest_key(key, k):
    def count_ge(c):
        return jnp.sum((key >= c).astype(I32), axis=1, keepdims=True)

    zero = jnp.zeros((key.shape[0], 1), I32)
    t = jnp.where(count_ge(zero) >= k, zero, jnp.full_like(zero, INT_MIN))

    def body(it, t):
        cand = t | (I32(1) << (I32(30) - it))
        return jnp.where(count_ge(cand) >= k, cand, t)

    return lax.fori_loop(0, 31, body, t)


def _t5_bucket(dist):
    n = jnp.maximum(dist, 0)
    max_exact = REL_BUCKETS // 2
    nf = jnp.maximum(n, 1).astype(F32)
    scale = (REL_BUCKETS - max_exact) / math.log(REL_MAX_DIST / max_exact)
    large = max_exact + (jnp.log(nf * (1.0 / max_exact)) * scale).astype(I32)
    large = jnp.minimum(large, REL_BUCKETS - 1)
    return jnp.where(n < max_exact, n, large)


def _rms_split_kernel(x_ref, g_ref, hh_ref, hl_ref):
    x = x_ref[...]
    y = x * lax.rsqrt(jnp.mean(x * x, axis=-1, keepdims=True) + EPS) * g_ref[...]
    hh = y.astype(BF16)
    hh_ref[...] = hh
    hl_ref[...] = (y - hh.astype(F32)).astype(BF16)


def _rms_split(x2d, g, tm):
    m, d = x2d.shape
    return pl.pallas_call(
        _rms_split_kernel,
        grid=(m // tm,),
        in_specs=[pl.BlockSpec((tm, d), lambda i: (i, 0)), pl.BlockSpec((1, d), lambda i: (0, 0))],
        out_specs=[pl.BlockSpec((tm, d), lambda i: (i, 0))] * 2,
        out_shape=[jax.ShapeDtypeStruct((m, d), BF16)] * 2,
        compiler_params=_params("parallel"),
        name="rms_split",
    )(x2d, g.reshape(1, d))


def _rms_kernel(x_ref, g_ref, o_ref):
    x = x_ref[...]
    o_ref[...] = x * lax.rsqrt(jnp.mean(x * x, axis=-1, keepdims=True) + EPS) * g_ref[...]


def _rms(x2d, g, tm):
    m, d = x2d.shape
    return pl.pallas_call(
        _rms_kernel,
        grid=(m // tm,),
        in_specs=[pl.BlockSpec((tm, d), lambda i: (i, 0)), pl.BlockSpec((1, d), lambda i: (0, 0))],
        out_specs=pl.BlockSpec((tm, d), lambda i: (i, 0)),
        out_shape=jax.ShapeDtypeStruct((m, d), F32),
        compiler_params=_params("parallel"),
        name="rms_final",
    )(x2d, g.reshape(1, d))


def _inproj_kernel(hh_ref, hl_ref, w_ref, o_ref, *, lo_tiles):
    j = pl.program_id(0)
    w = w_ref[...]
    w_hi = w.astype(BF16)

    @pl.when(j < lo_tiles)
    def _():
        o_ref[...] = jnp.dot(hh_ref[...], w_hi, preferred_element_type=F32)

    @pl.when(j >= lo_tiles)
    def _():
        w_lo = (w - w_hi.astype(F32)).astype(BF16)
        hh = hh_ref[...]
        acc = jnp.dot(hh, w_hi, preferred_element_type=F32)
        acc = acc + jnp.dot(hl_ref[...], w_hi, preferred_element_type=F32)
        o_ref[...] = acc + jnp.dot(hh, w_lo, preferred_element_type=F32)


def _inproj(hh, hl, w_p, tm, lo_tiles):
    m, d = hh.shape
    n = w_p.shape[1]
    return pl.pallas_call(
        functools.partial(_inproj_kernel, lo_tiles=lo_tiles),
        grid=(n // PROJ_TN, m // tm),
        in_specs=[pl.BlockSpec((tm, d), lambda j, i: (i, 0)),
                  pl.BlockSpec((tm, d), lambda j, i: (i, 0)),
                  pl.BlockSpec((d, PROJ_TN), lambda j, i: (0, j))],
        out_specs=pl.BlockSpec((tm, PROJ_TN), lambda j, i: (i, j)),
        out_shape=jax.ShapeDtypeStruct((m, n), F32),
        compiler_params=_params("parallel", "parallel"),
        name="inproj",
    )(hh, hl, w_p)


def _relayout_w_in(w):
    q, k, v, ga, qi, wi, ki, u, gs, qc, gc, gm = jnp.split(w, SPLIT_POINTS, axis=1)
    pad = jnp.zeros((w.shape[0], COL_QI - COL_KIWI - IDX_DIM - IDX_HEADS), w.dtype)
    return jnp.concatenate([gm, q, k, v, ga, u, gs, qc, gc, ki, wi, pad, qi], axis=1)


def _memkv_kernel(a_ref, w_ref, o_ref):
    o_ref[...] = _dot(a_ref[...], w_ref[...], False)


def _memkv(mem2d, w):
    m, d = mem2d.shape
    n = w.shape[1]
    tn = 512
    return pl.pallas_call(
        _memkv_kernel,
        grid=(n // tn,),
        in_specs=[pl.BlockSpec((m, d), lambda j: (0, 0)), pl.BlockSpec((d, tn), lambda j: (0, j))],
        out_specs=pl.BlockSpec((m, tn), lambda j: (0, j)),
        out_shape=jax.ShapeDtypeStruct((m, n), F32),
        compiler_params=_params("parallel"),
        name="memkv",
    )(mem2d, w)


def _bias_tile_kernel(rb_ref, o_ref):
    r = lax.broadcasted_iota(I32, (QBLOCK, 2 * QBLOCK), 0)
    c = lax.broadcasted_iota(I32, (QBLOCK, 2 * QBLOCK), 1)
    bucket = _t5_bucket(r - c + QBLOCK)
    for h in range(ATT_HEADS):
        tile = jnp.zeros((QBLOCK, 2 * QBLOCK), F32)
        for b in range(REL_BUCKETS):
            tile = jnp.where(bucket == b, rb_ref[b, h], tile)
        o_ref[h] = tile


def _bias_tiles(rel_bias):
    return pl.pallas_call(
        _bias_tile_kernel,
        in_specs=[pl.BlockSpec(memory_space=pltpu.SMEM)],
        out_specs=pl.BlockSpec(memory_space=pltpu.VMEM),
        out_shape=jax.ShapeDtypeStruct((ATT_HEADS, QBLOCK, 2 * QBLOCK), F32),
        name="bias_tiles",
    )(rel_bias)


def _dsa_prompt_kernel(rb_ref, q_ref, ga_ref, qi_ref, kwq_ref, k_ref, v_ref, kwk_ref, bt_ref, o_ref,
                       ki4_ref, kb_ref, vb_ref, *, seq, topk):
    i = pl.program_id(1)

    @pl.when(i == 0)
    def _():
        kh, kl = _split_bf16(kwk_ref[:, 0:IDX_DIM])
        ki4_ref[...] = jnp.concatenate([kh, kl, kh, kl], axis=1).astype(BF16)
        kb_ref[...] = k_ref[...].astype(BF16)
        vb_ref[...] = v_ref[...].astype(BF16)

    qi = qi_ref[...]
    wi = kwq_ref[:, IDX_DIM:IDX_DIM + IDX_HEADS] * (IDX_HEADS ** -0.5)
    ki4 = ki4_ref[...]
    score = jnp.zeros((QBLOCK, seq), F32)
    for h in range(IDX_HEADS):
        qh, ql = _split_bf16(qi[:, h * IDX_DIM:(h + 1) * IDX_DIM])
        q4 = jnp.concatenate([qh, qh, ql, ql], axis=1).astype(BF16)
        s = lax.dot_general(q4, ki4, NT_DIMS, preferred_element_type=F32)
        score = score + jnp.maximum(s, 0.0) * wi[:, h:h + 1]

    key_pos = lax.broadcasted_iota(I32, (QBLOCK, seq), 1)
    q_pos = i * QBLOCK + lax.broadcasted_iota(I32, (QBLOCK, seq), 0)
    key = jnp.where(key_pos <= q_pos, _sortable_key(score), I32(INT_MIN))
    thr = jnp.maximum(_kth_largest_key(key, topk), I32(INT_MIN + 1))
    mask_add = jnp.where(key >= thr, 0.0, -jnp.inf)

    key_blk = key_pos // QBLOCK
    is_diag = key_blk == i
    is_prev = key_blk == i - 1
    reps = seq // QBLOCK
    q = q_ref[...]
    ga = ga_ref[...]
    for h in range(ATT_HEADS):
        cols = slice(h * HEAD_DIM, (h + 1) * HEAD_DIM)
        t_prev = jnp.concatenate([bt_ref[h, :, 0:QBLOCK]] * reps, axis=1)
        t_diag = jnp.concatenate([bt_ref[h, :, QBLOCK:2 * QBLOCK]] * reps, axis=1)
        bias = jnp.where(is_diag, t_diag, jnp.where(is_prev, t_prev, rb_ref[REL_BUCKETS - 1, h]))
        logits = lax.dot_general(q[:, cols].astype(BF16), kb_ref[:, cols], NT_DIMS,
                                 preferred_element_type=F32)
        logits = logits * (HEAD_DIM ** -0.5) + bias + mask_add
        m = jnp.max(logits, axis=1, keepdims=True)
        p = jnp.exp(logits - m)
        l = jnp.sum(p, axis=1, keepdims=True)
        o = jnp.dot(p.astype(BF16), vb_ref[:, cols], preferred_element_type=F32)
        o_ref[:, cols] = o / l * _silu(ga[:, cols])


def _dsa_prompt(p3, bias_tiles, rel_bias):
    b, s, _ = p3.shape
    topk = min(TOPK_MAX, s // 4)
    row = lambda col, w: pl.BlockSpec((None, QBLOCK, w), lambda bi, i: (bi, i, col // w))
    full = lambda col, w: pl.BlockSpec((None, s, w), lambda bi, i: (bi, 0, col // w))
    return pl.pallas_call(
        functools.partial(_dsa_prompt_kernel, seq=s, topk=topk),
        grid=(b, s // QBLOCK),
        in_specs=[pl.BlockSpec(memory_space=pltpu.SMEM),
                  row(COL_Q, ATT_W), row(COL_GA, ATT_W), row(COL_QI, IDX_W), row(COL_KIWI, LANES),
                  full(COL_K, ATT_W), full(COL_V, ATT_W), full(COL_KIWI, LANES),
                  pl.BlockSpec((ATT_HEADS, QBLOCK, 2 * QBLOCK), lambda bi, i: (0, 0, 0))],
        out_specs=pl.BlockSpec((None, QBLOCK, ATT_W), lambda bi, i: (bi, i, 0)),
        out_shape=jax.ShapeDtypeStruct((b, s, ATT_W), F32),
        scratch_shapes=[pltpu.VMEM((s, 4 * IDX_DIM), BF16),
                        pltpu.VMEM((s, ATT_W), BF16),
                        pltpu.VMEM((s, ATT_W), BF16)],
        compiler_params=_params("parallel", "arbitrary"),
        name="dsa_prompt",
    )(rel_bias, p3, p3, p3, p3, p3, p3, p3, bias_tiles)


def _sidx_kernel(pt_ref, q_ref, w_ref, k0_ref, k1_ref, k2_ref, k3_ref, kn_ref, s_ref, q4_ref, *, nch):
    c = pl.program_id(1)

    @pl.when(c == 0)
    def _():
        qh, ql = _split_bf16(q_ref[...])
        q4_ref[...] = jnp.concatenate([qh, qh, ql, ql], axis=1).astype(BF16)

    def scores(ki):
        n = ki.shape[0]
        kh, kl = _split_bf16(ki)
        ki4 = jnp.concatenate([kh, kl, kh, kl], axis=1).astype(BF16)
        s = lax.dot_general(q4_ref[...], ki4, NT_DIMS, preferred_element_type=F32)
        s = jnp.maximum(s, 0.0) * (w_ref[...] * (IDX_HEADS ** -0.5))
        return jnp.sum(s.reshape(IDX_HEADS, SUBLANES, n), axis=0)

    @pl.when(c < nch)
    def _():
        ki = jnp.concatenate([k0_ref[...], k1_ref[...], k2_ref[...], k3_ref[...]], axis=0)
        s_ref[...] = scores(ki)

    @pl.when(c == nch)
    def _():
        n = kn_ref.shape[0]
        sn = scores(kn_ref[...])
        t_key = lax.broadcasted_iota(I32, (SUBLANES, n), 1)
        t_q = lax.broadcasted_iota(I32, (SUBLANES, n), 0)
        s_ref[...] = jnp.full(s_ref.shape, -jnp.inf, F32)
        s_ref[:, 0:n] = jnp.where(t_key <= t_q, sn, -jnp.inf)


def _sattn_kernel(pt_ref, rb_ref, sf_ref, sc_ref, q_ref, ga_ref,
                  k0_ref, k1_ref, k2_ref, k3_ref, v0_ref, v1_ref, v2_ref, v3_ref, kn_ref, vn_ref,
                  o_ref, thr_ref, m_ref, l_ref, acc_ref, *, nch, past, topk, page):
    c = pl.program_id(1)
    chunk = PAGES_PER_STEP * page

    @pl.when(c == 0)
    def _():
        t = _kth_largest_key(_sortable_key(sf_ref[...]), topk)
        thr_ref[...] = jnp.maximum(t, I32(INT_MIN + 1))
        m_ref[...] = jnp.full(m_ref.shape, -jnp.inf, F32)
        l_ref[...] = jnp.zeros(l_ref.shape, F32)
        acc_ref[...] = jnp.zeros(acc_ref.shape, F32)

    def attend(kf, vf, sc, key_pos0):
        n = kf.shape[0]
        mask_add = jnp.where(_sortable_key(sc) >= thr_ref[...], 0.0, -jnp.inf)
        q_pos = past + lax.broadcasted_iota(I32, (SUBLANES, n), 0)
        key_pos = key_pos0 + lax.broadcasted_iota(I32, (SUBLANES, n), 1)
        bucket = _t5_bucket(q_pos - key_pos)
        q = q_ref[...]
        kb = kf.astype(BF16)
        vb = vf.astype(BF16)
        for h in range(ATT_HEADS):
            cols = slice(h * HEAD_DIM, (h + 1) * HEAD_DIM)
            bias = jnp.zeros((SUBLANES, n), F32)
            for b in range(REL_BUCKETS):
                bias = jnp.where(bucket == b, rb_ref[b, h], bias)
            logits = lax.dot_general(q[:, cols].astype(BF16), kb[:, cols], NT_DIMS,
                                     preferred_element_type=F32)
            logits = logits * (HEAD_DIM ** -0.5) + bias + mask_add
            m_old = m_ref[h]
            m_new = jnp.maximum(m_old, jnp.max(logits, axis=1, keepdims=True))
            m_safe = jnp.where(m_new == -jnp.inf, 0.0, m_new)
            alpha = jnp.exp(m_old - m_safe)
            p = jnp.exp(logits - m_safe[:, 0:1])
            l_ref[h] = alpha * l_ref[h] + jnp.sum(p, axis=1, keepdims=True)
            acc_ref[:, cols] = alpha * acc_ref[:, cols] + jnp.dot(
                p.astype(BF16), vb[:, cols], preferred_element_type=F32)
            m_ref[h] = m_new

    @pl.when(c < nch)
    def _():
        kf = jnp.concatenate([k0_ref[...], k1_ref[...], k2_ref[...], k3_ref[...]], axis=0)
        vf = jnp.concatenate([v0_ref[...], v1_ref[...], v2_ref[...], v3_ref[...]], axis=0)
        attend(kf, vf, sc_ref[...], c * chunk)

    @pl.when(c == nch)
    def _():
        n = kn_ref.shape[0]
        attend(kn_ref[...], vn_ref[...], sc_ref[:, 0:n], past)
        ga = ga_ref[...]
        for h in range(ATT_HEADS):
            cols = slice(h * HEAD_DIM, (h + 1) * HEAD_DIM)
            o_ref[:, cols] = acc_ref[:, cols] / l_ref[h] * _silu(ga[:, cols])


def _dsa_sample(layer, ps3, cache_k, cache_v, cache_kidx, page_table, rel_bias):
    db, t, _ = ps3.shape
    n_pages = page_table.shape[1]
    n_pool, page = cache_kidx.shape[1], cache_kidx.shape[2]
    past = n_pages * page
    topk = min(TOPK_MAX, (past + t) // 4)
    nch = n_pages // PAGES_PER_STEP
    chunk = PAGES_PER_STEP * page
    ntot = (nch + 1) * chunk

    qi = ps3[:, :, COL_QI:COL_QI + IDX_W].reshape(db, t, IDX_HEADS, IDX_DIM)
    qi_hq = qi.transpose(0, 2, 1, 3).reshape(db, IDX_HEADS * t, IDX_DIM)
    wi = ps3[:, :, COL_KIWI + IDX_DIM:COL_KIWI + IDX_DIM + IDX_HEADS]
    wi_hq = wi.transpose(0, 2, 1).reshape(db, IDX_HEADS * t, 1)
    pad_rows = lambda a: jnp.pad(a, ((0, 0), (0, page - t), (0, 0)))
    ki_new = pad_rows(ps3[:, :, COL_KIWI:COL_KIWI + IDX_DIM])
    k_new = pad_rows(ps3[:, :, COL_K:COL_K + ATT_W])
    v_new = pad_rows(ps3[:, :, COL_V:COL_V + ATT_W])
    q = ps3[:, :, COL_Q:COL_Q + ATT_W]
    ga = ps3[:, :, COL_GA:COL_GA + ATT_W]
    ck = cache_k.reshape(cache_k.shape[0], n_pool, page, ATT_W)
    cv = cache_v.reshape(cache_v.shape[0], n_pool, page, ATT_W)

    def page_spec(r, width):
        def imap(b, c, pt):
            return (layer, pt[b, jnp.minimum(c * PAGES_PER_STEP + r, n_pages - 1)], 0, 0)
        return pl.BlockSpec((None, None, page, width), imap)

    per_b = lambda rows, w: pl.BlockSpec((None, rows, w), lambda b, c, pt: (b, 0, 0))

    scores = pl.pallas_call(
        functools.partial(_sidx_kernel, nch=nch),
        grid_spec=pltpu.PrefetchScalarGridSpec(
            num_scalar_prefetch=1,
            grid=(db, nch + 1),
            in_specs=[per_b(IDX_HEADS * t, IDX_DIM), per_b(IDX_HEADS * t, 1)]
                     + [page_spec(r, IDX_DIM) for r in range(PAGES_PER_STEP)]
                     + [per_b(page, IDX_DIM)],
            out_specs=pl.BlockSpec((None, t, chunk), lambda b, c, pt: (b, 0, c)),
            scratch_shapes=[pltpu.VMEM((IDX_HEADS * t, 4 * IDX_DIM), BF16)]),
        out_shape=jax.ShapeDtypeStruct((db, t, ntot), F32),
        compiler_params=_params("parallel", "arbitrary"),
        name="dsa_sample_scores",
    )(page_table, qi_hq, wi_hq, cache_kidx, cache_kidx, cache_kidx, cache_kidx, ki_new)

    return pl.pallas_call(
        functools.partial(_sattn_kernel, nch=nch, past=past, topk=topk, page=page),
        grid_spec=pltpu.PrefetchScalarGridSpec(
            num_scalar_prefetch=1,
            grid=(db, nch + 1),
            in_specs=[pl.BlockSpec(memory_space=pltpu.SMEM),
                      per_b(t, ntot),
                      pl.BlockSpec((None, t, chunk), lambda b, c, pt: (b, 0, c)),
                      per_b(t, ATT_W), per_b(t, ATT_W)]
                     + [page_spec(r, ATT_W) for r in range(PAGES_PER_STEP)] * 2
                     + [per_b(page, ATT_W), per_b(page, ATT_W)],
            out_specs=per_b(t, ATT_W),
            scratch_shapes=[pltpu.VMEM((t, 1), I32),
                            pltpu.VMEM((ATT_HEADS, t, LANES), F32),
                            pltpu.VMEM((ATT_HEADS, t, LANES), F32),
                            pltpu.VMEM((t, ATT_W), F32)]),
        out_shape=jax.ShapeDtypeStruct((db, t, ATT_W), F32),
        compiler_params=_params("parallel", "arbitrary"),
        name="dsa_sample_attn",
    )(page_table, rel_bias, scores, scores, q, ga, ck, ck, ck, ck, cv, cv, cv, cv, k_new, v_new)


def _ssm_param_kernel(are_ref, aim_ref, ldt_ref, bre_ref, bim_ref, lbr_ref, lbi_ref, bbr_ref, bbi_ref):
    lam_re = jnp.minimum(are_ref[...], -1e-4)
    lam_im = aim_ref[...]
    step = jnp.exp(ldt_ref[...])
    mag = jnp.exp(lam_re * step)
    ang = lam_im * step
    lb_re = mag * jnp.cos(ang)
    lb_im = mag * jnp.sin(ang)
    nr = lb_re - 1.0
    ni = lb_im
    den = lam_re * lam_re + lam_im * lam_im
    coef_re = (nr * lam_re + ni * lam_im) / den
    coef_im = (ni * lam_re - nr * lam_im) / den
    lbr_ref[...] = lb_re
    lbi_ref[...] = lb_im
    for c in range(SSM_GROUP):
        br = bre_ref[c]
        bi = bim_ref[c]
        bbr_ref[c] = coef_re * br - coef_im * bi
        bbi_ref[c] = coef_re * bi + coef_im * br


def _ssm_params(a_re, a_im, log_dt, b_re, b_im):
    g, p, c = b_re.shape
    vm = pl.BlockSpec(memory_space=pltpu.VMEM)
    return pl.pallas_call(
        _ssm_param_kernel,
        in_specs=[vm] * 5,
        out_specs=[vm] * 4,
        out_shape=[jax.ShapeDtypeStruct((g, p), F32)] * 2 + [jax.ShapeDtypeStruct((c, g, p), F32)] * 2,
        name="ssm_params",
    )(a_re, a_im, log_dt.reshape(g, 1), b_re.transpose(2, 0, 1), b_im.transpose(2, 0, 1))


def _block_diag_slabs(m):
    g, a, b = m.shape
    eye = jnp.eye(SLAB_GROUPS, dtype=m.dtype)
    m4 = m.reshape(N_SLABS, SLAB_GROUPS, a, b)
    return jnp.einsum('sgab,gh->sgahb', m4, eye).reshape(N_SLABS, SLAB_GROUPS * a, SLAB_GROUPS * b)


def _s5_kernel(u_ref, x0r_ref, x0i_ref, lbr_ref, lbi_ref, wbu_ref, wyr_ref, wyi_ref, d_ref,
               z_ref, xr_ref, xi_ref, zs_ref, *, seg_len, chain, hi):
    nc = SLAB_STATES // LANES
    u = u_ref[...]
    bu = _dot(u, wbu_ref[...], hi)
    for j in range(2 * nc):
        zs_ref[j] = bu[:, j * LANES:(j + 1) * LANES]
    shape = (SCAN_ROWS, SLAB_STATES)
    lbr = jnp.broadcast_to(lbr_ref[...], shape)
    lbi = jnp.broadcast_to(lbi_ref[...], shape)

    def load(t, base):
        rows = pl.ds(t, SCAN_ROWS, stride=seg_len)
        return jnp.concatenate([zs_ref[base + j, rows, :] for j in range(nc)], axis=1)

    def store(t, base, val):
        rows = pl.ds(t, SCAN_ROWS, stride=seg_len)
        for j in range(nc):
            zs_ref[base + j, rows, :] = val[:, j * LANES:(j + 1) * LANES]

    def advance(t, carry):
        xr, xi = carry
        nr = lbr * xr - lbi * xi + load(t, 0)
        ni = lbr * xi + lbi * xr + load(t, nc)
        return nr, ni

    def advance_store(t, carry):
        nr, ni = advance(t, carry)
        store(t, 0, nr)
        store(t, nc, ni)
        return nr, ni

    x0r = x0r_ref[...]
    x0i = x0i_ref[...]
    if chain:
        zero = jnp.zeros(shape, F32)
        er, ei = lax.fori_loop(0, seg_len, advance, (zero, zero))
        pr, pi = lbr, lbi
        for _ in range(int(math.log2(seg_len))):
            pr, pi = pr * pr - pi * pi, 2.0 * pr * pi
        row = lax.broadcasted_iota(I32, shape, 0)
        sr = jnp.where(row == 0, x0r, 0.0)
        si = jnp.where(row == 0, x0i, 0.0)
        for s in range(SCAN_ROWS - 1):
            nr = er + pr * sr - pi * si
            ni = ei + pr * si + pi * sr
            sr = jnp.where(row == s + 1, pltpu.roll(nr, 1, axis=0), sr)
            si = jnp.where(row == s + 1, pltpu.roll(ni, 1, axis=0), si)
        x0r, x0i = sr, si
    xr, xi = lax.fori_loop(0, seg_len, advance_store, (x0r, x0i))
    xr_ref[...] = xr
    xi_ref[...] = xi
    y = d_ref[...] * u
    for j in range(nc):
        blk = slice(j * LANES, (j + 1) * LANES)
        y = y + _dot(zs_ref[j], wyr_ref[blk, :], hi) - _dot(zs_ref[nc + j], wyi_ref[blk, :], hi)
    z_ref[...] = _gelu_tanh(y)


def _s5(p3, x0r, x0i, lbr, lbi, wbu, wyr, wyi, d, *, seg_len, chain, hi):
    nb, r, _ = p3.shape
    assert r == SCAN_ROWS * seg_len and (not chain or seg_len & (seg_len - 1) == 0)
    slab = lambda rows, w: pl.BlockSpec((None, rows, w), lambda b, s: (s, 0, 0))
    st = pl.BlockSpec((None, SCAN_ROWS, SLAB_STATES), lambda b, s: (b, 0, s))
    return pl.pallas_call(
        functools.partial(_s5_kernel, seg_len=seg_len, chain=chain, hi=hi),
        grid=(nb, N_SLABS),
        in_specs=[pl.BlockSpec((None, r, LANES), lambda b, s: (b, 0, COL_U // LANES + s)),
                  st, st, slab(1, SLAB_STATES), slab(1, SLAB_STATES),
                  slab(LANES, 2 * SLAB_STATES), slab(SLAB_STATES, LANES), slab(SLAB_STATES, LANES),
                  slab(1, LANES)],
        out_specs=[pl.BlockSpec((None, r, LANES), lambda b, s: (b, 0, s)), st, st],
        out_shape=[jax.ShapeDtypeStruct((nb, r, SSM_W), F32),
                   jax.ShapeDtypeStruct((nb, SCAN_ROWS, SSM_GROUPS * SSM_STATE), F32),
                   jax.ShapeDtypeStruct((nb, SCAN_ROWS, SSM_GROUPS * SSM_STATE), F32)],
        scratch_shapes=[pltpu.VMEM((2 * SLAB_STATES // LANES, r, LANES), F32)],
        compiler_params=_params("parallel", "parallel"),
        name="s5_scan",
    )(p3, x0r, x0i, lbr, lbi, wbu, wyr, wyi, d)


def _cross_kernel(q_ref, g_ref, mk_ref, mv_ref, o_ref):
    q = q_ref[...]
    g = g_ref[...]
    for h in range(CROSS_HEADS):
        cols = slice(h * CROSS_HD, (h + 1) * CROSS_HD)
        logits = lax.dot_general(q[:, cols].astype(BF16), mk_ref[:, cols].astype(BF16), NT_DIMS,
                                 preferred_element_type=F32) * (CROSS_HD ** -0.5)
        m = jnp.max(logits, axis=1, keepdims=True)
        p = jnp.exp(logits - m)
        l = jnp.sum(p, axis=1, keepdims=True)
        o = jnp.dot(p.astype(BF16), mv_ref[:, cols].astype(BF16), preferred_element_type=F32)
        o_ref[:, cols] = o / l * _silu(g[:, cols])


def _cross(p3, mk, mv, tq):
    nb, s, _ = p3.shape
    mem = mk.shape[1]
    qspec = lambda col: pl.BlockSpec((None, tq, CROSS_W), lambda b, i: (b, i, col // CROSS_W))
    mspec = pl.BlockSpec((None, mem, CROSS_W), lambda b, i: (b, 0, 0))
    return pl.pallas_call(
        _cross_kernel,
        grid=(nb, s // tq),
        in_specs=[qspec(COL_QC), qspec(COL_GC), mspec, mspec],
        out_specs=pl.BlockSpec((None, tq, CROSS_W), lambda b, i: (b, i, 0)),
        out_shape=jax.ShapeDtypeStruct((nb, s, CROSS_W), F32),
        compiler_params=_params("parallel", "parallel"),
        name="cross_attn",
    )(p3, p3, mk, mv)


def _branch_kernel(a_ref, z_ref, c_ref, gs_ref, gm0_ref, gm1_ref, gm2_ref,
                   wglu_ref, bglu_ref, wba_ref, wbs_ref, wbc_ref, o_ref, *, hi):
    gl = _dot(z_ref[...], wglu_ref[...], hi) + bglu_ref[...]
    s_out = gl[:, 0:SSM_W] * _sigmoid(gl[:, SSM_W:2 * SSM_W]) * _silu(gs_ref[...])
    merged = _sigmoid(gm0_ref[...]) * _dot(a_ref[...], wba_ref[...], hi)
    merged = merged + _sigmoid(gm1_ref[...]) * _dot(s_out, wbs_ref[...], hi)
    merged = merged + _sigmoid(gm2_ref[...]) * _dot(c_ref[...], wbc_ref[...], hi)
    o_ref[...] = merged


def _branch_merge(a2d, z2d, c2d, p2d, w_glu, b_glu, w_ba, w_bs, w_bc, tm, hi):
    m = a2d.shape[0]
    rows = lambda w, col=0: pl.BlockSpec((tm, w), lambda i: (i, col // w))
    const = lambda arr: pl.BlockSpec(arr.shape, lambda i: (0, 0), pipeline_mode=pl.Buffered(1))
    b2 = b_glu.reshape(1, -1)
    return pl.pallas_call(
        functools.partial(_branch_kernel, hi=hi),
        grid=(m // tm,),
        in_specs=[rows(ATT_W), rows(SSM_W), rows(CROSS_W), rows(SSM_W, COL_GS),
                  rows(D_MODEL, COL_GM), rows(D_MODEL, COL_GM + D_MODEL), rows(D_MODEL, COL_GM + 2 * D_MODEL),
                  const(w_glu), const(b2), const(w_ba), const(w_bs), const(w_bc)],
        out_specs=rows(D_MODEL),
        out_shape=jax.ShapeDtypeStruct((m, D_MODEL), F32),
        compiler_params=_params("parallel"),
        name="branch_merge",
    )(a2d, z2d, c2d, p2d, p2d, p2d, p2d, w_glu, b2, w_ba, w_bs, w_bc)


def _outproj_kernel(x_ref, m_ref, w_ref, o_ref, *, hi):
    o_ref[...] = x_ref[...] + _dot(m_ref[...], w_ref[...], hi)


def _outproj(x2d, merged, w_out, tm, hi):
    m, d = x2d.shape
    rows = pl.BlockSpec((tm, d), lambda i: (i, 0))
    return pl.pallas_call(
        functools.partial(_outproj_kernel, hi=hi),
        grid=(m // tm,),
        in_specs=[rows, rows, pl.BlockSpec((d, d), lambda i: (0, 0), pipeline_mode=pl.Buffered(1))],
        out_specs=rows,
        out_shape=jax.ShapeDtypeStruct((m, d), F32),
        compiler_params=_params("parallel"),
        name="outproj",
    )(x2d, merged, w_out)


def _mixer_layer(x3, attend, x0r, x0i, mk, mv, norm_g, w_in_p, w_glu, b_glu, w_ba, w_bs, w_bc, w_out,
                 ssm, *, tm, n_scan, seg_len, chain, hi, tq):
    nb, s, d = x3.shape
    m = nb * s
    x2d = x3.reshape(m, d)
    tr = min(tm, 256)
    hh, hl = _rms_split(x2d, norm_g, tr)
    p2d = _inproj(hh, hl, w_in_p, tm, 0 if hi else PROJ_LO_TILES)
    p3 = p2d.reshape(nb, s, PROJ_W)
    a_out = attend(p3)
    z, xr, xi = _s5(p2d.reshape(n_scan, m // n_scan, PROJ_W), x0r, x0i, *ssm,
                    seg_len=seg_len, chain=chain, hi=hi)
    c_out = _cross(p3, mk, mv, tq)
    merged = _branch_merge(a_out.reshape(m, ATT_W), z.reshape(m, SSM_W), c_out.reshape(m, CROSS_W),
                           p2d, w_glu, b_glu, w_ba, w_bs, w_bc, tr, hi)
    y = _outproj(x2d, merged, w_out, tr, hi).reshape(nb, s, d)
    k = p3[:, :, COL_K:COL_K + ATT_W].reshape(nb, s, ATT_HEADS, HEAD_DIM)
    v = p3[:, :, COL_V:COL_V + ATT_W].reshape(nb, s, ATT_HEADS, HEAD_DIM)
    ki = p3[:, :, COL_KIWI:COL_KIWI + IDX_DIM]
    return y, (k, v, ki, xr, xi)


def kernel(x_prompt, x_sample, cache_k, cache_v, cache_kidx, cache_mem_k, cache_mem_v, state_ssm_re, state_ssm_im, page_table, mem_prompt, norm_g, w_in, w_branch_attn, w_branch_ssm, w_branch_cross, w_out, w_mem_kv, ssm_a_re, ssm_a_im, ssm_b_re, ssm_b_im, ssm_c_re, ssm_c_im, ssm_d, ssm_log_dt, w_glu, b_glu, rel_bias, final_norm_g):
    depth = w_in.shape[0]
    b, s, d = x_prompt.shape
    db, t, _ = x_sample.shape
    mem_len = mem_prompt.shape[1]
    n_state = SSM_GROUPS * SSM_STATE
    assert s == SCAN_ROWS * (s // SCAN_ROWS) and db == SCAN_ROWS

    bias_tiles = _bias_tiles(rel_bias)
    zeros_state = jnp.zeros((b, SCAN_ROWS, n_state), F32)
    yp, ys = x_prompt, x_sample
    outs = [[] for _ in range(12)]
    for l in range(depth):
        w_in_p = _relayout_w_in(w_in[l])
        lb_re, lb_im, bb_re, bb_im = _ssm_params(ssm_a_re[l], ssm_a_im[l], ssm_log_dt[l],
                                                 ssm_b_re[l], ssm_b_im[l])
        wbu = jnp.concatenate([_block_diag_slabs(bb_re.transpose(1, 0, 2)),
                               _block_diag_slabs(bb_im.transpose(1, 0, 2))], axis=2)
        wyr = _block_diag_slabs(ssm_c_re[l].transpose(0, 2, 1))
        wyi = _block_diag_slabs(ssm_c_im[l].transpose(0, 2, 1))
        ssm = (lb_re.reshape(N_SLABS, 1, SLAB_STATES), lb_im.reshape(N_SLABS, 1, SLAB_STATES),
               wbu, wyr, wyi, ssm_d[l].reshape(N_SLABS, 1, LANES))

        mkv = _memkv(mem_prompt.reshape(b * mem_len, d), w_mem_kv[l])
        mk_p = mkv[:, :CROSS_W].reshape(b, mem_len, CROSS_W)
        mv_p = mkv[:, CROSS_W:].reshape(b, mem_len, CROSS_W)
        attend_p = functools.partial(_dsa_prompt, bias_tiles=bias_tiles, rel_bias=rel_bias)
        yp, (kp, vp, kip, xr, xi) = _mixer_layer(
            yp, attend_p, zeros_state, zeros_state, mk_p, mv_p, norm_g[l], w_in_p,
            w_glu[l].astype(BF16), b_glu[l], w_branch_attn[l].astype(BF16), w_branch_ssm[l].astype(BF16),
            w_branch_cross[l].astype(BF16), w_out[l].astype(BF16), ssm,
            tm=1024, n_scan=b, seg_len=s // SCAN_ROWS, chain=True, hi=False, tq=512)
        outs[0].append(kp); outs[1].append(vp); outs[2].append(kip)
        outs[3].append(mk_p.reshape(b, mem_len, CROSS_HEADS, CROSS_HD))
        outs[4].append(mv_p.reshape(b, mem_len, CROSS_HEADS, CROSS_HD))
        outs[5].append(xr[:, SCAN_ROWS - 1].reshape(b, SSM_GROUPS, SSM_STATE))
        outs[6].append(xi[:, SCAN_ROWS - 1].reshape(b, SSM_GROUPS, SSM_STATE))

        attend_s = functools.partial(_dsa_sample, l, cache_k=cache_k, cache_v=cache_v,
                                     cache_kidx=cache_kidx, page_table=page_table, rel_bias=rel_bias)
        ys, (kn, vn, kin, sr, si) = _mixer_layer(
            ys, attend_s,
            state_ssm_re[l].reshape(1, db, n_state), state_ssm_im[l].reshape(1, db, n_state),
            cache_mem_k[l].reshape(db, mem_len, CROSS_W), cache_mem_v[l].reshape(db, mem_len, CROSS_W),
            norm_g[l], w_in_p, w_glu[l], b_glu[l], w_branch_attn[l], w_branch_ssm[l],
            w_branch_cross[l], w_out[l], ssm,
            tm=db * t, n_scan=1, seg_len=t, chain=False, hi=True, tq=t)
        outs[7].append(kn); outs[8].append(vn); outs[9].append(kin)
        outs[10].append(sr.reshape(db, SSM_GROUPS, SSM_STATE)); outs[11].append(si.reshape(db, SSM_GROUPS, SSM_STATE))

    y_prompt = _rms(yp.reshape(b * s, d), final_norm_g, 256).reshape(b, s, d)
    y_sample = _rms(ys.reshape(db * t, d), final_norm_g, db * t).reshape(db, t, d)
    return (y_prompt, y_sample) + tuple(jnp.stack(o, axis=0) for o in outs)
```

```python
import functools
import math

import jax
import jax.numpy as jnp
import numpy as np
from jax import lax
from jax.experimental import pallas as pl
from jax.experimental.pallas import tpu as pltpu

F32 = jnp.float32
BF16 = jnp.bfloat16
I32 = jnp.int32
HIGHEST = lax.Precision.HIGHEST
NT_DIMS = (((1,), (1,)), ((), ()))

VMEM_LIMIT_BYTES = 56 * 1024 * 1024
LANES = 128
SUBLANES = 8

D_MODEL = 2048
HEAD_DIM = 128
ATT_HEADS = 6
ATT_W = ATT_HEADS * HEAD_DIM
IDX_HEADS = 16
IDX_DIM = 64
IDX_W = IDX_HEADS * IDX_DIM
TOPK_MAX = 256
QBLOCK = 128
SSM_GROUP = 16
SSM_W = 768
SSM_GROUPS = SSM_W // SSM_GROUP
SSM_STATE = 64
CROSS_HEADS = 4
CROSS_HD = 128
CROSS_W = CROSS_HEADS * CROSS_HD
REL_BUCKETS = 32
REL_MAX_DIST = 128
N_BRANCHES = 3
EPS = 1e-6
SPLIT_WIDTHS = (ATT_W, ATT_W, ATT_W, ATT_W, IDX_W, IDX_HEADS, IDX_DIM,
                SSM_W, SSM_W, CROSS_W, CROSS_W, N_BRANCHES * D_MODEL)
SPLIT_POINTS = tuple(int(c) for c in np.cumsum(SPLIT_WIDTHS)[:-1])

COL_GM = 0
COL_Q = 6144
COL_K = COL_Q + ATT_W
COL_V = COL_K + ATT_W
COL_GA = COL_V + ATT_W
COL_U = COL_GA + ATT_W
COL_GS = COL_U + SSM_W
COL_QC = COL_GS + SSM_W
COL_GC = COL_QC + CROSS_W
COL_KIWI = COL_GC + CROSS_W
COL_QI = 12288
PROJ_W = COL_QI + IDX_W
PROJ_TN = 512
PROJ_LO_TILES = COL_KIWI // PROJ_TN

SLAB_GROUPS = LANES // SSM_GROUP
SLAB_STATES = SLAB_GROUPS * SSM_STATE
N_SLABS = SSM_GROUPS // SLAB_GROUPS
SCAN_ROWS = SUBLANES

KEY_EXTENT_CLASSES = 4
IDX_KEY_CHUNK = 256

PAGES_PER_STEP = 4

INT_MIN = -2 ** 31


def _params(*sem):
    return pltpu.CompilerParams(dimension_semantics=sem, vmem_limit_bytes=VMEM_LIMIT_BYTES)


def _sigmoid(x):
    return 1.0 / (1.0 + jnp.exp(-x))


def _silu(x):
    return x * _sigmoid(x)


def _gelu_tanh(x):
    return 0.5 * x * (1.0 + jnp.tanh(math.sqrt(2.0 / math.pi) * (x + 0.044715 * (x * x * x))))


def _dot(a, b, hi):
    if hi:
        return jnp.dot(a.astype(F32), b.astype(F32), precision=HIGHEST, preferred_element_type=F32)
    return jnp.dot(a.astype(BF16), b.astype(BF16), preferred_element_type=F32)


def _split_bf16(x):
    hi = x.astype(BF16).astype(F32)
    return hi, x - hi


def _sortable_key(score):
    b = lax.bitcast_convert_type(score, I32)
    b = jnp.where(b == I32(INT_MIN), I32(0), b)
    key = b ^ ((b >> 31) & I32(0x7FFFFFFF))
    return jnp.where(score == -jnp.inf, I32(INT_MIN), key)


def _kth_largest_key(key, k):
    def count_ge(c):
        return jnp.sum((key >= c).astype(I32), axis=1, keepdims=True)

    zero = jnp.zeros((key.shape[0], 1), I32)
    t = jnp.where(count_ge(zero) >= k, zero, jnp.full_like(zero, INT_MIN))

    def body(it, t):
        cand = t | (I32(1) << (I32(30) - it))
        return jnp.where(count_ge(cand) >= k, cand, t)

    return lax.fori_loop(0, 31, body, t)


def _t5_bucket(dist):
    n = jnp.maximum(dist, 0)
    max_exact = REL_BUCKETS // 2
    nf = jnp.maximum(n, 1).astype(F32)
    scale = (REL_BUCKETS - max_exact) / math.log(REL_MAX_DIST / max_exact)
    large = max_exact + (jnp.log(nf * (1.0 / max_exact)) * scale).astype(I32)
    large = jnp.minimum(large, REL_BUCKETS - 1)
    return jnp.where(n < max_exact, n, large)


def _rms_split_kernel(x_ref, g_ref, hh_ref, hl_ref):
    x = x_ref[...]
    y = x * lax.rsqrt(jnp.mean(x * x, axis=-1, keepdims=True) + EPS) * g_ref[...]
    hh = y.astype(BF16)
    hh_ref[...] = hh
    hl_ref[...] = (y - hh.astype(F32)).astype(BF16)


def _rms_split(x2d, g, tm):
    m, d = x2d.shape
    return pl.pallas_call(
        _rms_split_kernel,
        grid=(m // tm,),
        in_specs=[pl.BlockSpec((tm, d), lambda i: (i, 0)), pl.BlockSpec((1, d), lambda i: (0, 0))],
        out_specs=[pl.BlockSpec((tm, d), lambda i: (i, 0))] * 2,
        out_shape=[jax.ShapeDtypeStruct((m, d), BF16)] * 2,
        compiler_params=_params("parallel"),
        name="rms_split",
    )(x2d, g.reshape(1, d))


def _rms_kernel(x_ref, g_ref, o_ref):
    x = x_ref[...]
    o_ref[...] = x * lax.rsqrt(jnp.mean(x * x, axis=-1, keepdims=True) + EPS) * g_ref[...]


def _rms(x2d, g, tm):
    m, d = x2d.shape
    return pl.pallas_call(
        _rms_kernel,
        grid=(m // tm,),
        in_specs=[pl.BlockSpec((tm, d), lambda i: (i, 0)), pl.BlockSpec((1, d), lambda i: (0, 0))],
        out_specs=pl.BlockSpec((tm, d), lambda i: (i, 0)),
        out_shape=jax.ShapeDtypeStruct((m, d), F32),
        compiler_params=_params("parallel"),
        name="rms_final",
    )(x2d, g.reshape(1, d))


def _inproj_kernel(hh_ref, hl_ref, w_ref, o_ref, *, lo_tiles):
    j = pl.program_id(0)
    w = w_ref[...]
    w_hi = w.astype(BF16)

    @pl.when(j < lo_tiles)
    def _():
        o_ref[...] = jnp.dot(hh_ref[...], w_hi, preferred_element_type=F32)

    @pl.when(j >= lo_tiles)
    def _():
        w_lo = (w - w_hi.astype(F32)).astype(BF16)
        hh = hh_ref[...]
        acc = jnp.dot(hh, w_hi, preferred_element_type=F32)
        acc = acc + jnp.dot(hl_ref[...], w_hi, preferred_element_type=F32)
        o_ref[...] = acc + jnp.dot(hh, w_lo, preferred_element_type=F32)


def _inproj(hh, hl, w_p, tm, lo_tiles):
    m, d = hh.shape
    n = w_p.shape[1]
    return pl.pallas_call(
        functools.partial(_inproj_kernel, lo_tiles=lo_tiles),
        grid=(n // PROJ_TN, m // tm),
        in_specs=[pl.BlockSpec((tm, d), lambda j, i: (i, 0)),
                  pl.BlockSpec((tm, d), lambda j, i: (i, 0)),
                  pl.BlockSpec((d, PROJ_TN), lambda j, i: (0, j))],
        out_specs=pl.BlockSpec((tm, PROJ_TN), lambda j, i: (i, j)),
        out_shape=jax.ShapeDtypeStruct((m, n), F32),
        compiler_params=_params("parallel", "parallel"),
        name="inproj",
    )(hh, hl, w_p)


def _relayout_w_in(w):
    q, k, v, ga, qi, wi, ki, u, gs, qc, gc, gm = jnp.split(w, SPLIT_POINTS, axis=1)
    pad = jnp.zeros((w.shape[0], COL_QI - COL_KIWI - IDX_DIM - IDX_HEADS), w.dtype)
    return jnp.concatenate([gm, q, k, v, ga, u, gs, qc, gc, ki, wi, pad, qi], axis=1)


def _memkv_kernel(a_ref, w_ref, o_ref):
    o_ref[...] = _dot(a_ref[...], w_ref[...], False)


def _memkv(mem2d, w):
    m, d = mem2d.shape
    n = w.shape[1]
    tn = 512
    return pl.pallas_call(
        _memkv_kernel,
        grid=(n // tn,),
        in_specs=[pl.BlockSpec((m, d), lambda j: (0, 0)), pl.BlockSpec((d, tn), lambda j: (0, j))],
        out_specs=pl.BlockSpec((m, tn), lambda j: (0, j)),
        out_shape=jax.ShapeDtypeStruct((m, n), F32),
        compiler_params=_params("parallel"),
        name="memkv",
    )(mem2d, w)


def _bias_tile_kernel(rb_ref, o_ref):
    r = lax.broadcasted_iota(I32, (QBLOCK, 2 * QBLOCK), 0)
    c = lax.broadcasted_iota(I32, (QBLOCK, 2 * QBLOCK), 1)
    bucket = _t5_bucket(r - c + QBLOCK)
    for h in range(ATT_HEADS):
        tile = jnp.zeros((QBLOCK, 2 * QBLOCK), F32)
        for b in range(REL_BUCKETS):
            tile = jnp.where(bucket == b, rb_ref[b, h], tile)
        o_ref[h] = tile


def _bias_tiles(rel_bias):
    return pl.pallas_call(
        _bias_tile_kernel,
        in_specs=[pl.BlockSpec(memory_space=pltpu.SMEM)],
        out_specs=pl.BlockSpec(memory_space=pltpu.VMEM),
        out_shape=jax.ShapeDtypeStruct((ATT_HEADS, QBLOCK, 2 * QBLOCK), F32),
        name="bias_tiles",
    )(rel_bias)


def _dsa_prompt_kernel(rb_ref, q_ref, ga_ref, qi_ref, kwq_ref, k_ref, v_ref, kwk_ref, bt_ref, o_ref,
                       ki4_ref, kb_ref, vb_ref, *, seq, topk):
    i = pl.program_id(1)

    @pl.when(i == 0)
    def _():
        kh, kl = _split_bf16(kwk_ref[:, 0:IDX_DIM])
        ki4_ref[...] = jnp.concatenate([kh, kl, kh, kl], axis=1).astype(BF16)
        kb_ref[...] = k_ref[...].astype(BF16)
        vb_ref[...] = v_ref[...].astype(BF16)

    n_blk = seq // QBLOCK
    per = n_blk // KEY_EXTENT_CLASSES
    for cls in range(KEY_EXTENT_CLASSES):
        pl.when(i // per == cls)(functools.partial(
            _dsa_prompt_block, i, (cls + 1) * per * QBLOCK, topk,
            rb_ref, q_ref, ga_ref, qi_ref, kwq_ref, bt_ref, o_ref, ki4_ref, kb_ref, vb_ref))


def _dsa_prompt_block(i, nk, topk, rb_ref, q_ref, ga_ref, qi_ref, kwq_ref, bt_ref, o_ref,
                      ki4_ref, kb_ref, vb_ref):
    qi = qi_ref[...]
    wi = kwq_ref[:, IDX_DIM:IDX_DIM + IDX_HEADS] * (IDX_HEADS ** -0.5)
    q4 = []
    for h in range(IDX_HEADS):
        qh, ql = _split_bf16(qi[:, h * IDX_DIM:(h + 1) * IDX_DIM])
        q4.append(jnp.concatenate([qh, qh, ql, ql], axis=1).astype(BF16))
    q4 = jnp.concatenate(q4, axis=0)
    chunks = []
    for c0 in range(0, nk, IDX_KEY_CHUNK):
        s = lax.dot_general(q4, ki4_ref[c0:c0 + IDX_KEY_CHUNK, :], NT_DIMS, preferred_element_type=F32)
        acc = jnp.zeros((QBLOCK, IDX_KEY_CHUNK), F32)
        for h in range(IDX_HEADS):
            acc = acc + jnp.maximum(s[h * QBLOCK:(h + 1) * QBLOCK], 0.0) * wi[:, h:h + 1]
        chunks.append(acc)
    score = jnp.concatenate(chunks, axis=1)

    key_pos = lax.broadcasted_iota(I32, (QBLOCK, nk), 1)
    q_pos = i * QBLOCK + lax.broadcasted_iota(I32, (QBLOCK, nk), 0)
    key = jnp.where(key_pos <= q_pos, _sortable_key(score), I32(INT_MIN))
    thr = jnp.maximum(_kth_largest_key(key, topk), I32(INT_MIN + 1))
    mask_add = jnp.where(key >= thr, 0.0, -jnp.inf)

    key_blk = key_pos // QBLOCK
    is_diag = key_blk == i
    is_prev = key_blk == i - 1
    reps = nk // QBLOCK
    q = q_ref[...]
    ga = ga_ref[...]
    for h in range(ATT_HEADS):
        cols = slice(h * HEAD_DIM, (h + 1) * HEAD_DIM)
        t_prev = jnp.concatenate([bt_ref[h, :, 0:QBLOCK]] * reps, axis=1)
        t_diag = jnp.concatenate([bt_ref[h, :, QBLOCK:2 * QBLOCK]] * reps, axis=1)
        bias = jnp.where(is_diag, t_diag, jnp.where(is_prev, t_prev, rb_ref[REL_BUCKETS - 1, h]))
        logits = lax.dot_general(q[:, cols].astype(BF16), kb_ref[0:nk, cols], NT_DIMS,
                                 preferred_element_type=F32)
        logits = logits * (HEAD_DIM ** -0.5) + bias + mask_add
        m = jnp.max(logits, axis=1, keepdims=True)
        p = jnp.exp(logits - m)
        l = jnp.sum(p, axis=1, keepdims=True)
        o = jnp.dot(p.astype(BF16), vb_ref[0:nk, cols], preferred_element_type=F32)
        o_ref[:, cols] = o / l * _silu(ga[:, cols])


def _dsa_prompt(p3, bias_tiles, rel_bias):
    b, s, _ = p3.shape
    topk = min(TOPK_MAX, s // 4)
    row = lambda col, w: pl.BlockSpec((None, QBLOCK, w), lambda bi, i: (bi, i, col // w))
    full = lambda col, w: pl.BlockSpec((None, s, w), lambda bi, i: (bi, 0, col // w))
    return pl.pallas_call(
        functools.partial(_dsa_prompt_kernel, seq=s, topk=topk),
        grid=(b, s // QBLOCK),
        in_specs=[pl.BlockSpec(memory_space=pltpu.SMEM),
                  row(COL_Q, ATT_W), row(COL_GA, ATT_W), row(COL_QI, IDX_W), row(COL_KIWI, LANES),
                  full(COL_K, ATT_W), full(COL_V, ATT_W), full(COL_KIWI, LANES),
                  pl.BlockSpec((ATT_HEADS, QBLOCK, 2 * QBLOCK), lambda bi, i: (0, 0, 0))],
        out_specs=pl.BlockSpec((None, QBLOCK, ATT_W), lambda bi, i: (bi, i, 0)),
        out_shape=jax.ShapeDtypeStruct((b, s, ATT_W), F32),
        scratch_shapes=[pltpu.VMEM((s, 4 * IDX_DIM), BF16),
                        pltpu.VMEM((s, ATT_W), BF16),
                        pltpu.VMEM((s, ATT_W), BF16)],
        compiler_params=_params("parallel", "arbitrary"),
        name="dsa_prompt",
    )(rel_bias, p3, p3, p3, p3, p3, p3, p3, bias_tiles)


def _sidx_kernel(pt_ref, q_ref, w_ref, k0_ref, k1_ref, k2_ref, k3_ref, kn_ref, s_ref, q4_ref, *, nch):
    c = pl.program_id(1)

    @pl.when(c == 0)
    def _():
        qh, ql = _split_bf16(q_ref[...])
        q4_ref[...] = jnp.concatenate([qh, qh, ql, ql], axis=1).astype(BF16)

    def scores(ki):
        n = ki.shape[0]
        kh, kl = _split_bf16(ki)
        ki4 = jnp.concatenate([kh, kl, kh, kl], axis=1).astype(BF16)
        s = lax.dot_general(q4_ref[...], ki4, NT_DIMS, preferred_element_type=F32)
        s = jnp.maximum(s, 0.0) * (w_ref[...] * (IDX_HEADS ** -0.5))
        return jnp.sum(s.reshape(IDX_HEADS, SUBLANES, n), axis=0)

    @pl.when(c < nch)
    def _():
        ki = jnp.concatenate([k0_ref[...], k1_ref[...], k2_ref[...], k3_ref[...]], axis=0)
        s_ref[...] = scores(ki)

    @pl.when(c == nch)
    def _():
        n = kn_ref.shape[0]
        sn = scores(kn_ref[...])
        t_key = lax.broadcasted_iota(I32, (SUBLANES, n), 1)
        t_q = lax.broadcasted_iota(I32, (SUBLANES, n), 0)
        s_ref[...] = jnp.full(s_ref.shape, -jnp.inf, F32)
        s_ref[:, 0:n] = jnp.where(t_key <= t_q, sn, -jnp.inf)


def _sattn_kernel(pt_ref, rb_ref, sf_ref, sc_ref, q_ref, ga_ref,
                  k0_ref, k1_ref, k2_ref, k3_ref, v0_ref, v1_ref, v2_ref, v3_ref, kn_ref, vn_ref,
                  o_ref, thr_ref, m_ref, l_ref, acc_ref, *, nch, past, topk, page):
    c = pl.program_id(1)
    chunk = PAGES_PER_STEP * page

    @pl.when(c == 0)
    def _():
        t = _kth_largest_key(_sortable_key(sf_ref[...]), topk)
        thr_ref[...] = jnp.maximum(t, I32(INT_MIN + 1))
        m_ref[...] = jnp.full(m_ref.shape, -jnp.inf, F32)
        l_ref[...] = jnp.zeros(l_ref.shape, F32)
        acc_ref[...] = jnp.zeros(acc_ref.shape, F32)

    def attend(k_of, v_of, sc, key_pos0):
        n = sc.shape[1]
        mask_add = jnp.where(_sortable_key(sc) >= thr_ref[...], 0.0, -jnp.inf)
        q_pos = past + lax.broadcasted_iota(I32, (SUBLANES, n), 0)
        key_pos = key_pos0 + lax.broadcasted_iota(I32, (SUBLANES, n), 1)
        bucket = _t5_bucket(q_pos - key_pos)
        q = q_ref[...]
        for h in range(ATT_HEADS):
            cols = slice(h * HEAD_DIM, (h + 1) * HEAD_DIM)
            bias = jnp.zeros((SUBLANES, n), F32)
            for b in range(REL_BUCKETS):
                bias = jnp.where(bucket == b, rb_ref[b, h], bias)
            logits = lax.dot_general(q[:, cols].astype(BF16), k_of(h).astype(BF16), NT_DIMS,
                                     preferred_element_type=F32)
            logits = logits * (HEAD_DIM ** -0.5) + bias + mask_add
            m_old = m_ref[h]
            m_new = jnp.maximum(m_old, jnp.max(logits, axis=1, keepdims=True))
            m_safe = jnp.where(m_new == -jnp.inf, 0.0, m_new)
            alpha = jnp.exp(m_old - m_safe)
            p = jnp.exp(logits - m_safe[:, 0:1])
            l_ref[h] = alpha * l_ref[h] + jnp.sum(p, axis=1, keepdims=True)
            acc_ref[:, cols] = alpha * acc_ref[:, cols] + jnp.dot(
                p.astype(BF16), v_of(h).astype(BF16), preferred_element_type=F32)
            m_ref[h] = m_new

    @pl.when(c < nch)
    def _():
        k_of = lambda h: jnp.concatenate([r[:, h, :] for r in (k0_ref, k1_ref, k2_ref, k3_ref)], axis=0)
        v_of = lambda h: jnp.concatenate([r[:, h, :] for r in (v0_ref, v1_ref, v2_ref, v3_ref)], axis=0)
        attend(k_of, v_of, sc_ref[...], c * chunk)

    @pl.when(c == nch)
    def _():
        n = kn_ref.shape[0]
        head = lambda ref: (lambda h: ref[:, h * HEAD_DIM:(h + 1) * HEAD_DIM])
        attend(head(kn_ref), head(vn_ref), sc_ref[:, 0:n], past)
        ga = ga_ref[...]
        for h in range(ATT_HEADS):
            cols = slice(h * HEAD_DIM, (h + 1) * HEAD_DIM)
            o_ref[:, cols] = acc_ref[:, cols] / l_ref[h] * _silu(ga[:, cols])


def _dsa_sample(layer, ps3, cache_k, cache_v, cache_kidx, page_table, rel_bias):
    db, t, _ = ps3.shape
    n_pages = page_table.shape[1]
    n_pool, page = cache_kidx.shape[1], cache_kidx.shape[2]
    past = n_pages * page
    topk = min(TOPK_MAX, (past + t) // 4)
    nch = n_pages // PAGES_PER_STEP
    chunk = PAGES_PER_STEP * page
    ntot = (nch + 1) * chunk

    qi = ps3[:, :, COL_QI:COL_QI + IDX_W].reshape(db, t, IDX_HEADS, IDX_DIM)
    qi_hq = qi.transpose(0, 2, 1, 3).reshape(db, IDX_HEADS * t, IDX_DIM)
    wi = ps3[:, :, COL_KIWI + IDX_DIM:COL_KIWI + IDX_DIM + IDX_HEADS]
    wi_hq = wi.transpose(0, 2, 1).reshape(db, IDX_HEADS * t, 1)
    pad_rows = lambda a: jnp.pad(a, ((0, 0), (0, page - t), (0, 0)))
    ki_new = pad_rows(ps3[:, :, COL_KIWI:COL_KIWI + IDX_DIM])
    k_new = pad_rows(ps3[:, :, COL_K:COL_K + ATT_W])
    v_new = pad_rows(ps3[:, :, COL_V:COL_V + ATT_W])
    q = ps3[:, :, COL_Q:COL_Q + ATT_W]
    ga = ps3[:, :, COL_GA:COL_GA + ATT_W]

    def page_spec(r, *minor):
        def imap(b, c, pt):
            return (layer, pt[b, jnp.minimum(c * PAGES_PER_STEP + r, n_pages - 1)]) + (0,) * (1 + len(minor))
        return pl.BlockSpec((None, None, page) + minor, imap)

    per_b = lambda rows, w: pl.BlockSpec((None, rows, w), lambda b, c, pt: (b, 0, 0))

    scores = pl.pallas_call(
        functools.partial(_sidx_kernel, nch=nch),
        grid_spec=pltpu.PrefetchScalarGridSpec(
            num_scalar_prefetch=1,
            grid=(db, nch + 1),
            in_specs=[per_b(IDX_HEADS * t, IDX_DIM), per_b(IDX_HEADS * t, 1)]
                     + [page_spec(r, IDX_DIM) for r in range(PAGES_PER_STEP)]
                     + [per_b(page, IDX_DIM)],
            out_specs=pl.BlockSpec((None, t, chunk), lambda b, c, pt: (b, 0, c)),
            scratch_shapes=[pltpu.VMEM((IDX_HEADS * t, 4 * IDX_DIM), BF16)]),
        out_shape=jax.ShapeDtypeStruct((db, t, ntot), F32),
        compiler_params=_params("parallel", "arbitrary"),
        name="dsa_sample_scores",
    )(page_table, qi_hq, wi_hq, cache_kidx, cache_kidx, cache_kidx, cache_kidx, ki_new)

    return pl.pallas_call(
        functools.partial(_sattn_kernel, nch=nch, past=past, topk=topk, page=page),
        grid_spec=pltpu.PrefetchScalarGridSpec(
            num_scalar_prefetch=1,
            grid=(db, nch + 1),
            in_specs=[pl.BlockSpec(memory_space=pltpu.SMEM),
                      per_b(t, ntot),
                      pl.BlockSpec((None, t, chunk), lambda b, c, pt: (b, 0, c)),
                      per_b(t, ATT_W), per_b(t, ATT_W)]
                     + [page_spec(r, ATT_HEADS, HEAD_DIM) for r in range(PAGES_PER_STEP)] * 2
                     + [per_b(page, ATT_W), per_b(page, ATT_W)],
            out_specs=per_b(t, ATT_W),
            scratch_shapes=[pltpu.VMEM((t, 1), I32),
                            pltpu.VMEM((ATT_HEADS, t, LANES), F32),
                            pltpu.VMEM((ATT_HEADS, t, LANES), F32),
                            pltpu.VMEM((t, ATT_W), F32)]),
        out_shape=jax.ShapeDtypeStruct((db, t, ATT_W), F32),
        compiler_params=_params("parallel", "arbitrary"),
        name="dsa_sample_attn",
    )(page_table, rel_bias, scores, scores, q, ga, *([cache_k] * PAGES_PER_STEP), *([cache_v] * PAGES_PER_STEP), k_new, v_new)


def _ssm_param_kernel(are_ref, aim_ref, ldt_ref, bre_ref, bim_ref, lbr_ref, lbi_ref, bbr_ref, bbi_ref):
    lam_re = jnp.minimum(are_ref[...], -1e-4)
    lam_im = aim_ref[...]
    step = jnp.exp(ldt_ref[...])
    mag = jnp.exp(lam_re * step)
    ang = lam_im * step
    lb_re = mag * jnp.cos(ang)
    lb_im = mag * jnp.sin(ang)
    nr = lb_re - 1.0
    ni = lb_im
    den = lam_re * lam_re + lam_im * lam_im
    coef_re = (nr * lam_re + ni * lam_im) / den
    coef_im = (ni * lam_re - nr * lam_im) / den
    lbr_ref[...] = lb_re
    lbi_ref[...] = lb_im
    for c in range(SSM_GROUP):
        br = bre_ref[c]
        bi = bim_ref[c]
        bbr_ref[c] = coef_re * br - coef_im * bi
        bbi_ref[c] = coef_re * bi + coef_im * br


def _ssm_params(a_re, a_im, log_dt, b_re, b_im):
    g, p, c = b_re.shape
    vm = pl.BlockSpec(memory_space=pltpu.VMEM)
    return pl.pallas_call(
        _ssm_param_kernel,
        in_specs=[vm] * 5,
        out_specs=[vm] * 4,
        out_shape=[jax.ShapeDtypeStruct((g, p), F32)] * 2 + [jax.ShapeDtypeStruct((c, g, p), F32)] * 2,
        name="ssm_params",
    )(a_re, a_im, log_dt.reshape(g, 1), b_re.transpose(2, 0, 1), b_im.transpose(2, 0, 1))


def _block_diag_slabs(m):
    g, a, b = m.shape
    eye = jnp.eye(SLAB_GROUPS, dtype=m.dtype)
    m4 = m.reshape(N_SLABS, SLAB_GROUPS, a, b)
    return jnp.einsum('sgab,gh->sgahb', m4, eye).reshape(N_SLABS, SLAB_GROUPS * a, SLAB_GROUPS * b)


def _scan_pitch(seg_len):
    return seg_len + 4 if seg_len % 8 == 0 else seg_len


def _s5_kernel(u_ref, x0r_ref, x0i_ref, lbr_ref, lbi_ref, wbu_ref, wyr_ref, wyi_ref, d_ref,
               z_ref, xr_ref, xi_ref, up_ref, zs_ref, *, seg_len, chain, hi):
    nc = SLAB_STATES // LANES
    pitch = _scan_pitch(seg_len)
    up_ref[...] = jnp.zeros(up_ref.shape, F32)
    for s in range(SCAN_ROWS):
        up_ref[pl.ds(s * pitch, seg_len), :] = u_ref[pl.ds(s * seg_len, seg_len), :]
    u = up_ref[...]
    bu = _dot(u, wbu_ref[...], hi)
    for j in range(2 * nc):
        zs_ref[j] = bu[:, j * LANES:(j + 1) * LANES]
    shape = (SCAN_ROWS, SLAB_STATES)
    lbr = jnp.broadcast_to(lbr_ref[...], shape)
    lbi = jnp.broadcast_to(lbi_ref[...], shape)

    def load(t, base):
        rows = pl.ds(t, SCAN_ROWS, stride=pitch)
        return jnp.concatenate([zs_ref[base + j, rows, :] for j in range(nc)], axis=1)

    def store(t, base, val):
        rows = pl.ds(t, SCAN_ROWS, stride=pitch)
        for j in range(nc):
            zs_ref[base + j, rows, :] = val[:, j * LANES:(j + 1) * LANES]

    def advance(t, carry):
        xr, xi = carry
        nr = lbr * xr - lbi * xi + load(t, 0)
        ni = lbr * xi + lbi * xr + load(t, nc)
        return nr, ni

    def advance_store(t, carry):
        nr, ni = advance(t, carry)
        store(t, 0, nr)
        store(t, nc, ni)
        return nr, ni

    unroll = min(seg_len, 8)
    x0r = x0r_ref[...]
    x0i = x0i_ref[...]
    if chain:
        zero = jnp.zeros(shape, F32)
        er, ei = lax.fori_loop(0, seg_len, advance, (zero, zero), unroll=unroll)
        pr, pi = lbr, lbi
        for _ in range(int(math.log2(seg_len))):
            pr, pi = pr * pr - pi * pi, 2.0 * pr * pi
        row = lax.broadcasted_iota(I32, shape, 0)
        sr = jnp.where(row == 0, x0r, 0.0)
        si = jnp.where(row == 0, x0i, 0.0)
        for s in range(SCAN_ROWS - 1):
            nr = er + pr * sr - pi * si
            ni = ei + pr * si + pi * sr
            sr = jnp.where(row == s + 1, pltpu.roll(nr, 1, axis=0), sr)
            si = jnp.where(row == s + 1, pltpu.roll(ni, 1, axis=0), si)
        x0r, x0i = sr, si
    xr, xi = lax.fori_loop(0, seg_len, advance_store, (x0r, x0i), unroll=unroll)
    xr_ref[...] = xr
    xi_ref[...] = xi
    y = d_ref[...] * u
    for j in range(nc):
        blk = slice(j * LANES, (j + 1) * LANES)
        y = y + _dot(zs_ref[j], wyr_ref[blk, :], hi) - _dot(zs_ref[nc + j], wyi_ref[blk, :], hi)
    up_ref[...] = _gelu_tanh(y)
    for s in range(SCAN_ROWS):
        z_ref[pl.ds(s * seg_len, seg_len), :] = up_ref[pl.ds(s * pitch, seg_len), :]


def _s5(p3, x0r, x0i, lbr, lbi, wbu, wyr, wyi, d, *, seg_len, chain, hi):
    nb, r, _ = p3.shape
    assert r == SCAN_ROWS * seg_len and (not chain or seg_len & (seg_len - 1) == 0)
    rp = SCAN_ROWS * _scan_pitch(seg_len)
    slab = lambda rows, w: pl.BlockSpec((None, rows, w), lambda b, s: (s, 0, 0))
    st = pl.BlockSpec((None, SCAN_ROWS, SLAB_STATES), lambda b, s: (b, 0, s))
    return pl.pallas_call(
        functools.partial(_s5_kernel, seg_len=seg_len, chain=chain, hi=hi),
        grid=(nb, N_SLABS),
        in_specs=[pl.BlockSpec((None, r, LANES), lambda b, s: (b, 0, COL_U // LANES + s)),
                  st, st, slab(1, SLAB_STATES), slab(1, SLAB_STATES),
                  slab(LANES, 2 * SLAB_STATES), slab(SLAB_STATES, LANES), slab(SLAB_STATES, LANES),
                  slab(1, LANES)],
        out_specs=[pl.BlockSpec((None, r, LANES), lambda b, s: (b, 0, s)), st, st],
        out_shape=[jax.ShapeDtypeStruct((nb, r, SSM_W), F32),
                   jax.ShapeDtypeStruct((nb, SCAN_ROWS, SSM_GROUPS * SSM_STATE), F32),
                   jax.ShapeDtypeStruct((nb, SCAN_ROWS, SSM_GROUPS * SSM_STATE), F32)],
        scratch_shapes=[pltpu.VMEM((rp, LANES), F32),
                        pltpu.VMEM((2 * SLAB_STATES // LANES, rp, LANES), F32)],
        compiler_params=_params("parallel", "parallel"),
        name="s5_scan",
    )(p3, x0r, x0i, lbr, lbi, wbu, wyr, wyi, d)


def _cross_kernel(q_ref, g_ref, mk_ref, mv_ref, o_ref):
    q = q_ref[...]
    g = g_ref[...]
    for h in range(CROSS_HEADS):
        cols = slice(h * CROSS_HD, (h + 1) * CROSS_HD)
        logits = lax.dot_general(q[:, cols].astype(BF16), mk_ref[:, cols].astype(BF16), NT_DIMS,
                                 preferred_element_type=F32) * (CROSS_HD ** -0.5)
        m = jnp.max(logits, axis=1, keepdims=True)
        p = jnp.exp(logits - m)
        l = jnp.sum(p, axis=1, keepdims=True)
        o = jnp.dot(p.astype(BF16), mv_ref[:, cols].astype(BF16), preferred_element_type=F32)
        o_ref[:, cols] = o / l * _silu(g[:, cols])


def _cross(p3, mk, mv, tq):
    nb, s, _ = p3.shape
    mem = mk.shape[1]
    qspec = lambda col: pl.BlockSpec((None, tq, CROSS_W), lambda b, i: (b, i, col // CROSS_W))
    mspec = pl.BlockSpec((None, mem, CROSS_W), lambda b, i: (b, 0, 0))
    return pl.pallas_call(
        _cross_kernel,
        grid=(nb, s // tq),
        in_specs=[qspec(COL_QC), qspec(COL_GC), mspec, mspec],
        out_specs=pl.BlockSpec((None, tq, CROSS_W), lambda b, i: (b, i, 0)),
        out_shape=jax.ShapeDtypeStruct((nb, s, CROSS_W), F32),
        compiler_params=_params("parallel", "parallel"),
        name="cross_attn",
    )(p3, p3, mk, mv)


def _branch_kernel(a_ref, z_ref, c_ref, gs_ref, gm0_ref, gm1_ref, gm2_ref,
                   wglu_ref, bglu_ref, wba_ref, wbs_ref, wbc_ref, o_ref, *, hi):
    gl = _dot(z_ref[...], wglu_ref[...], hi) + bglu_ref[...]
    s_out = gl[:, 0:SSM_W] * _sigmoid(gl[:, SSM_W:2 * SSM_W]) * _silu(gs_ref[...])
    merged = _sigmoid(gm0_ref[...]) * _dot(a_ref[...], wba_ref[...], hi)
    merged = merged + _sigmoid(gm1_ref[...]) * _dot(s_out, wbs_ref[...], hi)
    merged = merged + _sigmoid(gm2_ref[...]) * _dot(c_ref[...], wbc_ref[...], hi)
    o_ref[...] = merged


def _branch_merge(a2d, z2d, c2d, p2d, w_glu, b_glu, w_ba, w_bs, w_bc, tm, hi):
    m = a2d.shape[0]
    rows = lambda w, col=0: pl.BlockSpec((tm, w), lambda i: (i, col // w))
    const = lambda arr: pl.BlockSpec(arr.shape, lambda i: (0, 0), pipeline_mode=pl.Buffered(1))
    b2 = b_glu.reshape(1, -1)
    return pl.pallas_call(
        functools.partial(_branch_kernel, hi=hi),
        grid=(m // tm,),
        in_specs=[rows(ATT_W), rows(SSM_W), rows(CROSS_W), rows(SSM_W, COL_GS),
                  rows(D_MODEL, COL_GM), rows(D_MODEL, COL_GM + D_MODEL), rows(D_MODEL, COL_GM + 2 * D_MODEL),
                  const(w_glu), const(b2), const(w_ba), const(w_bs), const(w_bc)],
        out_specs=rows(D_MODEL),
        out_shape=jax.ShapeDtypeStruct((m, D_MODEL), F32),
        compiler_params=_params("parallel"),
        name="branch_merge",
    )(a2d, z2d, c2d, p2d, p2d, p2d, p2d, w_glu, b2, w_ba, w_bs, w_bc)


def _outproj_kernel(x_ref, m_ref, w_ref, o_ref, *, hi):
    o_ref[...] = x_ref[...] + _dot(m_ref[...], w_ref[...], hi)


def _outproj(x2d, merged, w_out, tm, hi):
    m, d = x2d.shape
    rows = pl.BlockSpec((tm, d), lambda i: (i, 0))
    return pl.pallas_call(
        functools.partial(_outproj_kernel, hi=hi),
        grid=(m // tm,),
        in_specs=[rows, rows, pl.BlockSpec((d, d), lambda i: (0, 0), pipeline_mode=pl.Buffered(1))],
        out_specs=rows,
        out_shape=jax.ShapeDtypeStruct((m, d), F32),
        compiler_params=_params("parallel"),
        name="outproj",
    )(x2d, merged, w_out)


def _mixer_layer(x3, attend, x0r, x0i, mk, mv, norm_g, w_in_p, w_glu, b_glu, w_ba, w_bs, w_bc, w_out,
                 ssm, *, tm, n_scan, seg_len, chain, hi, tq):
    nb, s, d = x3.shape
    m = nb * s
    x2d = x3.reshape(m, d)
    tr = min(tm, 256)
    hh, hl = _rms_split(x2d, norm_g, tr)
    p2d = _inproj(hh, hl, w_in_p, tm, 0 if hi else PROJ_LO_TILES)
    p3 = p2d.reshape(nb, s, PROJ_W)
    a_out = attend(p3)
    z, xr, xi = _s5(p2d.reshape(n_scan, m // n_scan, PROJ_W), x0r, x0i, *ssm,
                    seg_len=seg_len, chain=chain, hi=hi)
    c_out = _cross(p3, mk, mv, tq)
    merged = _branch_merge(a_out.reshape(m, ATT_W), z.reshape(m, SSM_W), c_out.reshape(m, CROSS_W),
                           p2d, w_glu, b_glu, w_ba, w_bs, w_bc, tr, hi)
    y = _outproj(x2d, merged, w_out, tr, hi).reshape(nb, s, d)
    k = p3[:, :, COL_K:COL_K + ATT_W].reshape(nb, s, ATT_HEADS, HEAD_DIM)
    v = p3[:, :, COL_V:COL_V + ATT_W].reshape(nb, s, ATT_HEADS, HEAD_DIM)
    ki = p3[:, :, COL_KIWI:COL_KIWI + IDX_DIM]
    return y, (k, v, ki, xr, xi)


def kernel(x_prompt, x_sample, cache_k, cache_v, cache_kidx, cache_mem_k, cache_mem_v, state_ssm_re, state_ssm_im, page_table, mem_prompt, norm_g, w_in, w_branch_attn, w_branch_ssm, w_branch_cross, w_out, w_mem_kv, ssm_a_re, ssm_a_im, ssm_b_re, ssm_b_im, ssm_c_re, ssm_c_im, ssm_d, ssm_log_dt, w_glu, b_glu, rel_bias, final_norm_g):
    depth = w_in.shape[0]
    b, s, d = x_prompt.shape
    db, t, _ = x_sample.shape
    mem_len = mem_prompt.shape[1]
    n_state = SSM_GROUPS * SSM_STATE
    assert s == SCAN_ROWS * (s // SCAN_ROWS) and db == SCAN_ROWS

    bias_tiles = _bias_tiles(rel_bias)
    zeros_state = jnp.zeros((b, SCAN_ROWS, n_state), F32)
    yp, ys = x_prompt, x_sample
    outs = [[] for _ in range(12)]
    for l in range(depth):
        w_in_p = _relayout_w_in(w_in[l])
        lb_re, lb_im, bb_re, bb_im = _ssm_params(ssm_a_re[l], ssm_a_im[l], ssm_log_dt[l],
                                                 ssm_b_re[l], ssm_b_im[l])
        wbu = jnp.concatenate([_block_diag_slabs(bb_re.transpose(1, 0, 2)),
                               _block_diag_slabs(bb_im.transpose(1, 0, 2))], axis=2)
        wyr = _block_diag_slabs(ssm_c_re[l].transpose(0, 2, 1))
        wyi = _block_diag_slabs(ssm_c_im[l].transpose(0, 2, 1))
        ssm = (lb_re.reshape(N_SLABS, 1, SLAB_STATES), lb_im.reshape(N_SLABS, 1, SLAB_STATES),
               wbu, wyr, wyi, ssm_d[l].reshape(N_SLABS, 1, LANES))

        mkv = _memkv(mem_prompt.reshape(b * mem_len, d), w_mem_kv[l])
        mk_p = mkv[:, :CROSS_W].reshape(b, mem_len, CROSS_W)
        mv_p = mkv[:, CROSS_W:].reshape(b, mem_len, CROSS_W)
        attend_p = functools.partial(_dsa_prompt, bias_tiles=bias_tiles, rel_bias=rel_bias)
        yp, (kp, vp, kip, xr, xi) = _mixer_layer(
            yp, attend_p, zeros_state, zeros_state, mk_p, mv_p, norm_g[l], w_in_p,
            w_glu[l].astype(BF16), b_glu[l], w_branch_attn[l].astype(BF16), w_branch_ssm[l].astype(BF16),
            w_branch_cross[l].astype(BF16), w_out[l].astype(BF16), ssm,
            tm=1024, n_scan=b, seg_len=s // SCAN_ROWS, chain=True, hi=False, tq=512)
        outs[0].append(kp); outs[1].append(vp); outs[2].append(kip)
        outs[3].append(mk_p.reshape(b, mem_len, CROSS_HEADS, CROSS_HD))
        outs[4].append(mv_p.reshape(b, mem_len, CROSS_HEADS, CROSS_HD))
        outs[5].append(xr[:, SCAN_ROWS - 1].reshape(b, SSM_GROUPS, SSM_STATE))
        outs[6].append(xi[:, SCAN_ROWS - 1].reshape(b, SSM_GROUPS, SSM_STATE))

        attend_s = functools.partial(_dsa_sample, l, cache_k=cache_k, cache_v=cache_v,
                                     cache_kidx=cache_kidx, page_table=page_table, rel_bias=rel_bias)
        ys, (kn, vn, kin, sr, si) = _mixer_layer(
            ys, attend_s,
            state_ssm_re[l].reshape(1, db, n_state), state_ssm_im[l].reshape(1, db, n_state),
            cache_mem_k[l].reshape(db, mem_len, CROSS_W), cache_mem_v[l].reshape(db, mem_len, CROSS_W),
            norm_g[l], w_in_p, w_glu[l], b_glu[l], w_branch_attn[l], w_branch_ssm[l],
            w_branch_cross[l], w_out[l], ssm,
            tm=db * t, n_scan=1, seg_len=t, chain=False, hi=True, tq=t)
        outs[7].append(kn); outs[8].append(vn); outs[9].append(kin)
        outs[10].append(sr.reshape(db, SSM_GROUPS, SSM_STATE)); outs[11].append(si.reshape(db, SSM_GROUPS, SSM_STATE))

    y_prompt = _rms(yp.reshape(b * s, d), final_norm_g, 256).reshape(b, s, d)
    y_sample = _rms(ys.reshape(db * t, d), final_norm_g, db * t).reshape(db, t, d)
    return (y_prompt, y_sample) + tuple(jnp.stack(o, axis=0) for o in outs)
```

```python
import functools
import math

import jax
import jax.numpy as jnp
import numpy as np
from jax import lax
from jax.experimental import pallas as pl
from jax.experimental.pallas import tpu as pltpu

F32 = jnp.float32
BF16 = jnp.bfloat16
I32 = jnp.int32
HIGHEST = lax.Precision.HIGHEST
NT_DIMS = (((1,), (1,)), ((), ()))

VMEM_LIMIT_BYTES = 56 * 1024 * 1024
LANES = 128
SUBLANES = 8

D_MODEL = 2048
HEAD_DIM = 128
ATT_HEADS = 6
ATT_W = ATT_HEADS * HEAD_DIM
IDX_HEADS = 16
IDX_DIM = 64
IDX_W = IDX_HEADS * IDX_DIM
TOPK_MAX = 256
QBLOCK = 128
SSM_GROUP = 16
SSM_W = 768
SSM_GROUPS = SSM_W // SSM_GROUP
SSM_STATE = 64
CROSS_HEADS = 4
CROSS_HD = 128
CROSS_W = CROSS_HEADS * CROSS_HD
REL_BUCKETS = 32
REL_MAX_DIST = 128
N_BRANCHES = 3
EPS = 1e-6
SPLIT_WIDTHS = (ATT_W, ATT_W, ATT_W, ATT_W, IDX_W, IDX_HEADS, IDX_DIM,
                SSM_W, SSM_W, CROSS_W, CROSS_W, N_BRANCHES * D_MODEL)
SPLIT_POINTS = tuple(int(c) for c in np.cumsum(SPLIT_WIDTHS)[:-1])

COL_GM = 0
COL_Q = 6144
COL_K = COL_Q + ATT_W
COL_V = COL_K + ATT_W
COL_GA = COL_V + ATT_W
COL_U = COL_GA + ATT_W
COL_GS = COL_U + SSM_W
COL_QC = COL_GS + SSM_W
COL_GC = COL_QC + CROSS_W
COL_KIWI = COL_GC + CROSS_W
COL_WI = COL_KIWI
COL_KI = COL_KIWI + IDX_HEADS
COL_QI = 12288
PROJ_W = COL_QI + IDX_W
PROJ_TN = 512
PROJ_ROW_ALIGN = 16
PROJ_LO_TILES = COL_KIWI // PROJ_TN


def _proj_tile_rows():
    ref_start = dict(zip(("q", "k", "v", "ga", "qi", "wi", "ki", "u", "gs", "qc", "gc", "gm"),
                         (0,) + SPLIT_POINTS))
    runs = ((COL_GM, "gm", N_BRANCHES * D_MODEL), (COL_Q, "q", 4 * ATT_W), (COL_U, "u", 2 * SSM_W),
            (COL_QC, "qc", 2 * CROSS_W), (COL_KIWI, "wi", PROJ_TN), (COL_QI, "qi", IDX_W))
    rows = np.zeros(PROJ_W // PROJ_TN, np.int32)
    for col, name, width in runs:
        for t in range(width // PROJ_TN):
            rows[col // PROJ_TN + t] = ref_start[name] + t * PROJ_TN
    assert not (rows % PROJ_ROW_ALIGN).any()
    return rows

SLAB_GROUPS = LANES // SSM_GROUP
SLAB_STATES = SLAB_GROUPS * SSM_STATE
N_SLABS = SSM_GROUPS // SLAB_GROUPS
SCAN_ROWS = SUBLANES

KEY_EXTENT_CLASSES = 8
IDX_KEY_CHUNK = 256

PAGES_PER_STEP = 4

INT_MIN = -2 ** 31


def _params(*sem):
    return pltpu.CompilerParams(dimension_semantics=sem, vmem_limit_bytes=VMEM_LIMIT_BYTES)


def _sigmoid(x):
    return 1.0 / (1.0 + jnp.exp(-x))


def _silu(x):
    return x * _sigmoid(x)


def _gelu_tanh(x):
    return 0.5 * x * (1.0 + jnp.tanh(math.sqrt(2.0 / math.pi) * (x + 0.044715 * (x * x * x))))


def _dot(a, b, hi):
    if hi:
        return jnp.dot(a.astype(F32), b.astype(F32), precision=HIGHEST, preferred_element_type=F32)
    return jnp.dot(a.astype(BF16), b.astype(BF16), preferred_element_type=F32)


def _split_bf16(x):
    hi = x.astype(BF16).astype(F32)
    return hi, x - hi


def _sortable_key(score):
    b = lax.bitcast_convert_type(score, I32)
    b = jnp.where(b == I32(INT_MIN), I32(0), b)
    key = b ^ ((b >> 31) & I32(0x7FFFFFFF))
    return jnp.where(score == -jnp.inf, I32(INT_MIN), key)


def _kth_largest_key(key, k):
    def count_ge(c):
        return jnp.sum((key >= c).astype(I32), axis=1, keepdims=True)

    zero = jnp.zeros((key.shape[0], 1), I32)
    t = jnp.where(count_ge(zero) >= k, zero, jnp.full_like(zero, INT_MIN))

    def body(it, t):
        cand = t | (I32(1) << (I32(30) - it))
        return jnp.where(count_ge(cand) >= k, cand, t)

    return lax.fori_loop(0, 31, body, t)


def _t5_bucket(dist):
    n = jnp.maximum(dist, 0)
    max_exact = REL_BUCKETS // 2
    nf = jnp.maximum(n, 1).astype(F32)
    scale = (REL_BUCKETS - max_exact) / math.log(REL_MAX_DIST / max_exact)
    large = max_exact + (jnp.log(nf * (1.0 / max_exact)) * scale).astype(I32)
    large = jnp.minimum(large, REL_BUCKETS - 1)
    return jnp.where(n < max_exact, n, large)


def _rms_split_kernel(x_ref, g_ref, hh_ref, hl_ref):
    x = x_ref[...]
    y = x * lax.rsqrt(jnp.mean(x * x, axis=-1, keepdims=True) + EPS) * g_ref[...]
    hh = y.astype(BF16)
    hh_ref[...] = hh
    hl_ref[...] = (y - hh.astype(F32)).astype(BF16)


def _rms_split(x2d, g, tm):
    m, d = x2d.shape
    return pl.pallas_call(
        _rms_split_kernel,
        grid=(m // tm,),
        in_specs=[pl.BlockSpec((tm, d), lambda i: (i, 0)), pl.BlockSpec((1, d), lambda i: (0, 0))],
        out_specs=[pl.BlockSpec((tm, d), lambda i: (i, 0))] * 2,
        out_shape=[jax.ShapeDtypeStruct((m, d), BF16)] * 2,
        compiler_params=_params("parallel"),
        name="rms_split",
    )(x2d, g.reshape(1, d))


def _rms_kernel(x_ref, g_ref, o_ref):
    x = x_ref[...]
    o_ref[...] = x * lax.rsqrt(jnp.mean(x * x, axis=-1, keepdims=True) + EPS) * g_ref[...]


def _rms(x2d, g, tm):
    m, d = x2d.shape
    return pl.pallas_call(
        _rms_kernel,
        grid=(m // tm,),
        in_specs=[pl.BlockSpec((tm, d), lambda i: (i, 0)), pl.BlockSpec((1, d), lambda i: (0, 0))],
        out_specs=pl.BlockSpec((tm, d), lambda i: (i, 0)),
        out_shape=jax.ShapeDtypeStruct((m, d), F32),
        compiler_params=_params("parallel"),
        name="rms_final",
    )(x2d, g.reshape(1, d))


def _inproj_kernel(rows_ref, hh_ref, hl_ref, wt_ref, o_ref, *, lo_tiles):
    j = pl.program_id(1)
    w = wt_ref[...]
    w_hi = w.astype(BF16)

    @pl.when(j < lo_tiles)
    def _():
        o_ref[...] = lax.dot_general(hh_ref[...], w_hi, NT_DIMS, preferred_element_type=F32)

    @pl.when(j >= lo_tiles)
    def _():
        w_lo = (w - w_hi.astype(F32)).astype(BF16)
        hh = hh_ref[...]
        acc = lax.dot_general(hh, w_hi, NT_DIMS, preferred_element_type=F32)
        acc = acc + lax.dot_general(hl_ref[...], w_hi, NT_DIMS, preferred_element_type=F32)
        o_ref[...] = acc + lax.dot_general(hh, w_lo, NT_DIMS, preferred_element_type=F32)


def _inproj(hh, hl, w_in_t, layer, tm, lo_tiles):
    m, d = hh.shape
    n_tiles = PROJ_W // PROJ_TN
    resident = lambda: pl.BlockSpec((tm, d), lambda i, j, rows: (i, 0), pipeline_mode=pl.Buffered(1))
    return pl.pallas_call(
        functools.partial(_inproj_kernel, lo_tiles=lo_tiles),
        grid_spec=pltpu.PrefetchScalarGridSpec(
            num_scalar_prefetch=1,
            grid=(m // tm, n_tiles),
            in_specs=[resident(), resident(),
                      pl.BlockSpec((None, pl.Element(PROJ_TN), pl.Element(d)),
                                   lambda i, j, rows: (layer, rows[j] * PROJ_ROW_ALIGN, 0))],
            out_specs=pl.BlockSpec((tm, PROJ_TN), lambda i, j, rows: (i, j))),
        out_shape=jax.ShapeDtypeStruct((m, PROJ_W), F32),
        compiler_params=_params("parallel", "arbitrary"),
        name="inproj",
    )(jnp.asarray(_proj_tile_rows() // PROJ_ROW_ALIGN), hh, hl, w_in_t)


def _memkv_kernel(a_ref, w_ref, o_ref):
    o_ref[...] = _dot(a_ref[...], w_ref[...], False)


def _memkv(mem2d, w):
    m, d = mem2d.shape
    n = w.shape[1]
    tn = 512
    return pl.pallas_call(
        _memkv_kernel,
        grid=(n // tn,),
        in_specs=[pl.BlockSpec((m, d), lambda j: (0, 0)), pl.BlockSpec((d, tn), lambda j: (0, j))],
        out_specs=pl.BlockSpec((m, tn), lambda j: (0, j)),
        out_shape=jax.ShapeDtypeStruct((m, n), F32),
        compiler_params=_params("parallel"),
        name="memkv",
    )(mem2d, w)


def _bias_tile_kernel(rb_ref, o_ref):
    r = lax.broadcasted_iota(I32, (QBLOCK, 2 * QBLOCK), 0)
    c = lax.broadcasted_iota(I32, (QBLOCK, 2 * QBLOCK), 1)
    bucket = _t5_bucket(r - c + QBLOCK)
    for h in range(ATT_HEADS):
        tile = jnp.zeros((QBLOCK, 2 * QBLOCK), F32)
        for b in range(REL_BUCKETS):
            tile = jnp.where(bucket == b, rb_ref[b, h], tile)
        o_ref[h] = tile


def _bias_tiles(rel_bias):
    return pl.pallas_call(
        _bias_tile_kernel,
        in_specs=[pl.BlockSpec(memory_space=pltpu.SMEM)],
        out_specs=pl.BlockSpec(memory_space=pltpu.VMEM),
        out_shape=jax.ShapeDtypeStruct((ATT_HEADS, QBLOCK, 2 * QBLOCK), F32),
        name="bias_tiles",
    )(rel_bias)


def _dsa_prompt_kernel(rb_ref, q_ref, ga_ref, qi_ref, kwq_ref, k_ref, v_ref, kwk_ref, bt_ref, o_ref,
                       ki4_ref, kb_ref, vb_ref, *, seq, topk):
    i = pl.program_id(1)

    @pl.when(i == 0)
    def _():
        kh, kl = _split_bf16(kwk_ref[:, IDX_HEADS:IDX_HEADS + IDX_DIM])
        ki4_ref[...] = jnp.concatenate([kh, kl, kh, kl], axis=1).astype(BF16)
        kb_ref[...] = k_ref[...].astype(BF16)
        vb_ref[...] = v_ref[...].astype(BF16)

    n_blk = seq // QBLOCK
    per = n_blk // KEY_EXTENT_CLASSES
    for cls in range(KEY_EXTENT_CLASSES):
        pl.when(i // per == cls)(functools.partial(
            _dsa_prompt_block, i, (cls + 1) * per * QBLOCK, topk,
            rb_ref, q_ref, ga_ref, qi_ref, kwq_ref, bt_ref, o_ref, ki4_ref, kb_ref, vb_ref))


def _dsa_prompt_block(i, nk, topk, rb_ref, q_ref, ga_ref, qi_ref, kwq_ref, bt_ref, o_ref,
                      ki4_ref, kb_ref, vb_ref):
    qi = qi_ref[...]
    wi = kwq_ref[:, 0:IDX_HEADS] * (IDX_HEADS ** -0.5)
    q4 = []
    for h in range(IDX_HEADS):
        qh, ql = _split_bf16(qi[:, h * IDX_DIM:(h + 1) * IDX_DIM])
        q4.append(jnp.concatenate([qh, qh, ql, ql], axis=1).astype(BF16))
    q4 = jnp.concatenate(q4, axis=0)
    chunks = []
    for c0 in range(0, nk, IDX_KEY_CHUNK):
        s = lax.dot_general(q4, ki4_ref[c0:c0 + IDX_KEY_CHUNK, :], NT_DIMS, preferred_element_type=F32)
        acc = jnp.zeros((QBLOCK, IDX_KEY_CHUNK), F32)
        for h in range(IDX_HEADS):
            acc = acc + jnp.maximum(s[h * QBLOCK:(h + 1) * QBLOCK], 0.0) * wi[:, h:h + 1]
        chunks.append(acc)
    score = jnp.concatenate(chunks, axis=1)

    key_pos = lax.broadcasted_iota(I32, (QBLOCK, nk), 1)
    q_pos = i * QBLOCK + lax.broadcasted_iota(I32, (QBLOCK, nk), 0)
    key = jnp.where(key_pos <= q_pos, _sortable_key(score), I32(INT_MIN))
    thr = jnp.maximum(_kth_largest_key(key, topk), I32(INT_MIN + 1))
    mask_add = jnp.where(key >= thr, 0.0, -jnp.inf)

    key_blk = key_pos // QBLOCK
    is_diag = key_blk == i
    is_prev = key_blk == i - 1
    reps = nk // QBLOCK
    q = q_ref[...]
    ga = ga_ref[...]
    for h in range(ATT_HEADS):
        cols = slice(h * HEAD_DIM, (h + 1) * HEAD_DIM)
        t_prev = jnp.concatenate([bt_ref[h, :, 0:QBLOCK]] * reps, axis=1)
        t_diag = jnp.concatenate([bt_ref[h, :, QBLOCK:2 * QBLOCK]] * reps, axis=1)
        bias = jnp.where(is_diag, t_diag, jnp.where(is_prev, t_prev, rb_ref[REL_BUCKETS - 1, h]))
        logits = lax.dot_general(q[:, cols].astype(BF16), kb_ref[0:nk, cols], NT_DIMS,
                                 preferred_element_type=F32)
        logits = logits * (HEAD_DIM ** -0.5) + bias + mask_add
        m = jnp.max(logits, axis=1, keepdims=True)
        p = jnp.exp(logits - m)
        l = jnp.sum(p, axis=1, keepdims=True)
        o = jnp.dot(p.astype(BF16), vb_ref[0:nk, cols], preferred_element_type=F32)
        o_ref[:, cols] = o / l * _silu(ga[:, cols])


def _dsa_prompt(p3, bias_tiles, rel_bias):
    b, s, _ = p3.shape
    topk = min(TOPK_MAX, s // 4)
    row = lambda col, w: pl.BlockSpec((None, QBLOCK, w), lambda bi, i: (bi, i, col // w))
    full = lambda col, w: pl.BlockSpec((None, s, w), lambda bi, i: (bi, 0, col // w))
    return pl.pallas_call(
        functools.partial(_dsa_prompt_kernel, seq=s, topk=topk),
        grid=(b, s // QBLOCK),
        in_specs=[pl.BlockSpec(memory_space=pltpu.SMEM),
                  row(COL_Q, ATT_W), row(COL_GA, ATT_W), row(COL_QI, IDX_W), row(COL_KIWI, LANES),
                  full(COL_K, ATT_W), full(COL_V, ATT_W), full(COL_KIWI, LANES),
                  pl.BlockSpec((ATT_HEADS, QBLOCK, 2 * QBLOCK), lambda bi, i: (0, 0, 0))],
        out_specs=pl.BlockSpec((None, QBLOCK, ATT_W), lambda bi, i: (bi, i, 0)),
        out_shape=jax.ShapeDtypeStruct((b, s, ATT_W), F32),
        scratch_shapes=[pltpu.VMEM((s, 4 * IDX_DIM), BF16),
                        pltpu.VMEM((s, ATT_W), BF16),
                        pltpu.VMEM((s, ATT_W), BF16)],
        compiler_params=_params("parallel", "arbitrary"),
        name="dsa_prompt",
    )(rel_bias, p3, p3, p3, p3, p3, p3, p3, bias_tiles)


def _sidx_kernel(pt_ref, q_ref, w_ref, k0_ref, k1_ref, k2_ref, k3_ref, kn_ref, s_ref, q4_ref, *, nch):
    c = pl.program_id(1)

    @pl.when(c == 0)
    def _():
        qh, ql = _split_bf16(q_ref[...])
        q4_ref[...] = jnp.concatenate([qh, qh, ql, ql], axis=1).astype(BF16)

    def scores(ki_t):
        n = ki_t.shape[1]
        kh, kl = _split_bf16(ki_t)
        ki4_t = jnp.concatenate([kh, kl, kh, kl], axis=0).astype(BF16)
        s = jnp.dot(q4_ref[...], ki4_t, preferred_element_type=F32)
        s = jnp.maximum(s, 0.0) * (w_ref[...] * (IDX_HEADS ** -0.5))
        return jnp.sum(s.reshape(IDX_HEADS, SUBLANES, n), axis=0)

    @pl.when(c < nch)
    def _():
        ki_t = jnp.concatenate([k0_ref[...], k1_ref[...], k2_ref[...], k3_ref[...]], axis=1)
        s_ref[...] = scores(ki_t)

    @pl.when(c == nch)
    def _():
        n = kn_ref.shape[1]
        sn = scores(kn_ref[...])
        t_key = lax.broadcasted_iota(I32, (SUBLANES, n), 1)
        t_q = lax.broadcasted_iota(I32, (SUBLANES, n), 0)
        s_ref[...] = jnp.full(s_ref.shape, -jnp.inf, F32)
        s_ref[:, 0:n] = jnp.where(t_key <= t_q, sn, -jnp.inf)


def _sattn_kernel(pt_ref, rb_ref, sf_ref, sc_ref, q_ref, ga_ref,
                  k0_ref, k1_ref, k2_ref, k3_ref, v0_ref, v1_ref, v2_ref, v3_ref, kn_ref, vn_ref,
                  o_ref, thr_ref, m_ref, l_ref, acc_ref, *, nch, past, topk, page):
    c = pl.program_id(1)
    chunk = PAGES_PER_STEP * page

    @pl.when(c == 0)
    def _():
        t = _kth_largest_key(_sortable_key(sf_ref[...]), topk)
        thr_ref[...] = jnp.maximum(t, I32(INT_MIN + 1))
        m_ref[...] = jnp.full(m_ref.shape, -jnp.inf, F32)
        l_ref[...] = jnp.zeros(l_ref.shape, F32)
        acc_ref[...] = jnp.zeros(acc_ref.shape, F32)

    def attend(k_of, v_of, sc, key_pos0):
        n = sc.shape[1]
        mask_add = jnp.where(_sortable_key(sc) >= thr_ref[...], 0.0, -jnp.inf)
        q_pos = past + lax.broadcasted_iota(I32, (SUBLANES, n), 0)
        key_pos = key_pos0 + lax.broadcasted_iota(I32, (SUBLANES, n), 1)
        bucket = _t5_bucket(q_pos - key_pos)
        q = q_ref[...]
        for h in range(ATT_HEADS):
            cols = slice(h * HEAD_DIM, (h + 1) * HEAD_DIM)
            bias = jnp.zeros((SUBLANES, n), F32)
            for b in range(REL_BUCKETS):
                bias = jnp.where(bucket == b, rb_ref[b, h], bias)
            logits = lax.dot_general(q[:, cols].astype(BF16), k_of(h).astype(BF16), NT_DIMS,
                                     preferred_element_type=F32)
            logits = logits * (HEAD_DIM ** -0.5) + bias + mask_add
            m_old = m_ref[h]
            m_new = jnp.maximum(m_old, jnp.max(logits, axis=1, keepdims=True))
            m_safe = jnp.where(m_new == -jnp.inf, 0.0, m_new)
            alpha = jnp.exp(m_old - m_safe)
            p = jnp.exp(logits - m_safe[:, 0:1])
            l_ref[h] = alpha * l_ref[h] + jnp.sum(p, axis=1, keepdims=True)
            acc_ref[:, cols] = alpha * acc_ref[:, cols] + jnp.dot(
                p.astype(BF16), v_of(h).astype(BF16), preferred_element_type=F32)
            m_ref[h] = m_new

    @pl.when(c < nch)
    def _():
        k_of = lambda h: jnp.concatenate([r[h] for r in (k0_ref, k1_ref, k2_ref, k3_ref)], axis=0)
        v_of = lambda h: jnp.concatenate([r[h] for r in (v0_ref, v1_ref, v2_ref, v3_ref)], axis=0)
        attend(k_of, v_of, sc_ref[...], c * chunk)

    @pl.when(c == nch)
    def _():
        n = kn_ref.shape[0]
        head = lambda ref: (lambda h: ref[:, h * HEAD_DIM:(h + 1) * HEAD_DIM])
        attend(head(kn_ref), head(vn_ref), sc_ref[:, 0:n], past)
        ga = ga_ref[...]
        for h in range(ATT_HEADS):
            cols = slice(h * HEAD_DIM, (h + 1) * HEAD_DIM)
            o_ref[:, cols] = acc_ref[:, cols] / l_ref[h] * _silu(ga[:, cols])


def _dsa_sample(layer, ps3, cache_k_t, cache_v_t, cache_kidx_t, page_table, rel_bias):
    db, t, _ = ps3.shape
    n_pages = page_table.shape[1]
    page = cache_kidx_t.shape[3]
    past = n_pages * page
    topk = min(TOPK_MAX, (past + t) // 4)
    nch = n_pages // PAGES_PER_STEP
    chunk = PAGES_PER_STEP * page
    ntot = (nch + 1) * chunk

    qi = ps3[:, :, COL_QI:COL_QI + IDX_W].reshape(db, t, IDX_HEADS, IDX_DIM)
    qi_hq = qi.transpose(0, 2, 1, 3).reshape(db, IDX_HEADS * t, IDX_DIM)
    wi_hq = ps3[:, :, COL_WI:COL_WI + IDX_HEADS].transpose(0, 2, 1).reshape(db, IDX_HEADS * t, 1)
    pad_rows = lambda a: jnp.pad(a, ((0, 0), (0, page - t), (0, 0)))
    ki_new_t = pad_rows(ps3[:, :, COL_KI:COL_KI + IDX_DIM]).transpose(0, 2, 1)
    k_new = pad_rows(ps3[:, :, COL_K:COL_K + ATT_W])
    v_new = pad_rows(ps3[:, :, COL_V:COL_V + ATT_W])
    q = ps3[:, :, COL_Q:COL_Q + ATT_W]
    ga = ps3[:, :, COL_GA:COL_GA + ATT_W]

    def page_spec(r, *dims):
        def imap(b, c, pt):
            return (layer, pt[b, jnp.minimum(c * PAGES_PER_STEP + r, n_pages - 1)]) + (0,) * len(dims)
        return pl.BlockSpec((None, None) + dims, imap)

    per_b = lambda rows, w: pl.BlockSpec((None, rows, w), lambda b, c, pt: (b, 0, 0))

    scores = pl.pallas_call(
        functools.partial(_sidx_kernel, nch=nch),
        grid_spec=pltpu.PrefetchScalarGridSpec(
            num_scalar_prefetch=1,
            grid=(db, nch + 1),
            in_specs=[per_b(IDX_HEADS * t, IDX_DIM), per_b(IDX_HEADS * t, 1)]
                     + [page_spec(r, IDX_DIM, page) for r in range(PAGES_PER_STEP)]
                     + [per_b(IDX_DIM, page)],
            out_specs=pl.BlockSpec((None, t, chunk), lambda b, c, pt: (b, 0, c)),
            scratch_shapes=[pltpu.VMEM((IDX_HEADS * t, 4 * IDX_DIM), BF16)]),
        out_shape=jax.ShapeDtypeStruct((db, t, ntot), F32),
        compiler_params=_params("parallel", "arbitrary"),
        name="dsa_sample_scores",
    )(page_table, qi_hq, wi_hq, *([cache_kidx_t] * PAGES_PER_STEP), ki_new_t)

    return pl.pallas_call(
        functools.partial(_sattn_kernel, nch=nch, past=past, topk=topk, page=page),
        grid_spec=pltpu.PrefetchScalarGridSpec(
            num_scalar_prefetch=1,
            grid=(db, nch + 1),
            in_specs=[pl.BlockSpec(memory_space=pltpu.SMEM),
                      per_b(t, ntot),
                      pl.BlockSpec((None, t, chunk), lambda b, c, pt: (b, 0, c)),
                      per_b(t, ATT_W), per_b(t, ATT_W)]
                     + [page_spec(r, ATT_HEADS, page, HEAD_DIM) for r in range(PAGES_PER_STEP)] * 2
                     + [per_b(page, ATT_W), per_b(page, ATT_W)],
            out_specs=per_b(t, ATT_W),
            scratch_shapes=[pltpu.VMEM((t, 1), I32),
                            pltpu.VMEM((ATT_HEADS, t, LANES), F32),
                            pltpu.VMEM((ATT_HEADS, t, LANES), F32),
                            pltpu.VMEM((t, ATT_W), F32)]),
        out_shape=jax.ShapeDtypeStruct((db, t, ATT_W), F32),
        compiler_params=_params("parallel", "arbitrary"),
        name="dsa_sample_attn",
    )(page_table, rel_bias, scores, scores, q, ga,
      *([cache_k_t] * PAGES_PER_STEP), *([cache_v_t] * PAGES_PER_STEP), k_new, v_new)


def _ssm_param_kernel(are_ref, aim_ref, ldt_ref, bre_ref, bim_ref, lbr_ref, lbi_ref, bbr_ref, bbi_ref):
    lam_re = jnp.minimum(are_ref[...], -1e-4)
    lam_im = aim_ref[...]
    step = jnp.exp(ldt_ref[...])
    mag = jnp.exp(lam_re * step)
    ang = lam_im * step
    lb_re = mag * jnp.cos(ang)
    lb_im = mag * jnp.sin(ang)
    nr = lb_re - 1.0
    ni = lb_im
    den = lam_re * lam_re + lam_im * lam_im
    coef_re = (nr * lam_re + ni * lam_im) / den
    coef_im = (ni * lam_re - nr * lam_im) / den
    lbr_ref[...] = lb_re
    lbi_ref[...] = lb_im
    for c in range(SSM_GROUP):
        br = bre_ref[c]
        bi = bim_ref[c]
        bbr_ref[c] = coef_re * br - coef_im * bi
        bbi_ref[c] = coef_re * bi + coef_im * br


def _ssm_params(a_re, a_im, log_dt, b_re, b_im):
    g, p, c = b_re.shape
    vm = pl.BlockSpec(memory_space=pltpu.VMEM)
    return pl.pallas_call(
        _ssm_param_kernel,
        in_specs=[vm] * 5,
        out_specs=[vm] * 4,
        out_shape=[jax.ShapeDtypeStruct((g, p), F32)] * 2 + [jax.ShapeDtypeStruct((c, g, p), F32)] * 2,
        name="ssm_params",
    )(a_re, a_im, log_dt.reshape(g, 1), b_re.transpose(2, 0, 1), b_im.transpose(2, 0, 1))


def _block_diag_slabs(m):
    g, a, b = m.shape
    eye = jnp.eye(SLAB_GROUPS, dtype=m.dtype)
    m4 = m.reshape(N_SLABS, SLAB_GROUPS, a, b)
    return jnp.einsum('sgab,gh->sgahb', m4, eye).reshape(N_SLABS, SLAB_GROUPS * a, SLAB_GROUPS * b)


def _scan_pitch(seg_len):
    return seg_len + 4 if seg_len % 8 == 0 else seg_len


def _s5_kernel(u_ref, x0r_ref, x0i_ref, lbr_ref, lbi_ref, wbu_ref, wyr_ref, wyi_ref, d_ref,
               z_ref, xr_ref, xi_ref, up_ref, zs_ref, *, seg_len, chain, hi):
    nc = SLAB_STATES // LANES
    pitch = _scan_pitch(seg_len)
    up_ref[...] = jnp.zeros(up_ref.shape, F32)
    for s in range(SCAN_ROWS):
        up_ref[pl.ds(s * pitch, seg_len), :] = u_ref[pl.ds(s * seg_len, seg_len), :]
    u = up_ref[...]
    bu = _dot(u, wbu_ref[...], hi)
    for j in range(2 * nc):
        zs_ref[j] = bu[:, j * LANES:(j + 1) * LANES]
    shape = (SCAN_ROWS, SLAB_STATES)
    lbr = jnp.broadcast_to(lbr_ref[...], shape)
    lbi = jnp.broadcast_to(lbi_ref[...], shape)

    def load(t, base):
        rows = pl.ds(t, SCAN_ROWS, stride=pitch)
        return jnp.concatenate([zs_ref[base + j, rows, :] for j in range(nc)], axis=1)

    def store(t, base, val):
        rows = pl.ds(t, SCAN_ROWS, stride=pitch)
        for j in range(nc):
            zs_ref[base + j, rows, :] = val[:, j * LANES:(j + 1) * LANES]

    def advance(t, carry):
        xr, xi = carry
        nr = lbr * xr - lbi * xi + load(t, 0)
        ni = lbr * xi + lbi * xr + load(t, nc)
        return nr, ni

    def advance_store(t, carry):
        nr, ni = advance(t, carry)
        store(t, 0, nr)
        store(t, nc, ni)
        return nr, ni

    unroll = min(seg_len, 8)
    x0r = x0r_ref[...]
    x0i = x0i_ref[...]
    if chain:
        zero = jnp.zeros(shape, F32)
        er, ei = lax.fori_loop(0, seg_len, advance, (zero, zero), unroll=unroll)
        pr, pi = lbr, lbi
        for _ in range(int(math.log2(seg_len))):
            pr, pi = pr * pr - pi * pi, 2.0 * pr * pi
        row = lax.broadcasted_iota(I32, shape, 0)
        sr = jnp.where(row == 0, x0r, 0.0)
        si = jnp.where(row == 0, x0i, 0.0)
        for s in range(SCAN_ROWS - 1):
            nr = er + pr * sr - pi * si
            ni = ei + pr * si + pi * sr
            sr = jnp.where(row == s + 1, pltpu.roll(nr, 1, axis=0), sr)
            si = jnp.where(row == s + 1, pltpu.roll(ni, 1, axis=0), si)
        x0r, x0i = sr, si
    xr, xi = lax.fori_loop(0, seg_len, advance_store, (x0r, x0i), unroll=unroll)
    xr_ref[...] = xr
    xi_ref[...] = xi
    y = d_ref[...] * u
    for j in range(nc):
        blk = slice(j * LANES, (j + 1) * LANES)
        y = y + _dot(zs_ref[j], wyr_ref[blk, :], hi) - _dot(zs_ref[nc + j], wyi_ref[blk, :], hi)
    up_ref[...] = _gelu_tanh(y)
    for s in range(SCAN_ROWS):
        z_ref[pl.ds(s * seg_len, seg_len), :] = up_ref[pl.ds(s * pitch, seg_len), :]


def _s5(p3, x0r, x0i, lbr, lbi, wbu, wyr, wyi, d, *, seg_len, chain, hi):
    nb, r, _ = p3.shape
    assert r == SCAN_ROWS * seg_len and (not chain or seg_len & (seg_len - 1) == 0)
    rp = SCAN_ROWS * _scan_pitch(seg_len)
    slab = lambda rows, w: pl.BlockSpec((None, rows, w), lambda b, s: (s, 0, 0))
    st = pl.BlockSpec((None, SCAN_ROWS, SLAB_STATES), lambda b, s: (b, 0, s))
    return pl.pallas_call(
        functools.partial(_s5_kernel, seg_len=seg_len, chain=chain, hi=hi),
        grid=(nb, N_SLABS),
        in_specs=[pl.BlockSpec((None, r, LANES), lambda b, s: (b, 0, COL_U // LANES + s)),
                  st, st, slab(1, SLAB_STATES), slab(1, SLAB_STATES),
                  slab(LANES, 2 * SLAB_STATES), slab(SLAB_STATES, LANES), slab(SLAB_STATES, LANES),
                  slab(1, LANES)],
        out_specs=[pl.BlockSpec((None, r, LANES), lambda b, s: (b, 0, s)), st, st],
        out_shape=[jax.ShapeDtypeStruct((nb, r, SSM_W), F32),
                   jax.ShapeDtypeStruct((nb, SCAN_ROWS, SSM_GROUPS * SSM_STATE), F32),
                   jax.ShapeDtypeStruct((nb, SCAN_ROWS, SSM_GROUPS * SSM_STATE), F32)],
        scratch_shapes=[pltpu.VMEM((rp, LANES), F32),
                        pltpu.VMEM((2 * SLAB_STATES // LANES, rp, LANES), F32)],
        compiler_params=_params("parallel", "parallel"),
        name="s5_scan",
    )(p3, x0r, x0i, lbr, lbi, wbu, wyr, wyi, d)


def _cross_kernel(q_ref, g_ref, mk_ref, mv_ref, o_ref):
    q = q_ref[...]
    g = g_ref[...]
    for h in range(CROSS_HEADS):
        cols = slice(h * CROSS_HD, (h + 1) * CROSS_HD)
        logits = lax.dot_general(q[:, cols].astype(BF16), mk_ref[:, cols].astype(BF16), NT_DIMS,
                                 preferred_element_type=F32) * (CROSS_HD ** -0.5)
        m = jnp.max(logits, axis=1, keepdims=True)
        p = jnp.exp(logits - m)
        l = jnp.sum(p, axis=1, keepdims=True)
        o = jnp.dot(p.astype(BF16), mv_ref[:, cols].astype(BF16), preferred_element_type=F32)
        o_ref[:, cols] = o / l * _silu(g[:, cols])


def _cross(p3, mk, mv, tq):
    nb, s, _ = p3.shape
    mem = mk.shape[1]
    qspec = lambda col: pl.BlockSpec((None, tq, CROSS_W), lambda b, i: (b, i, col // CROSS_W))
    mspec = pl.BlockSpec((None, mem, CROSS_W), lambda b, i: (b, 0, 0))
    return pl.pallas_call(
        _cross_kernel,
        grid=(nb, s // tq),
        in_specs=[qspec(COL_QC), qspec(COL_GC), mspec, mspec],
        out_specs=pl.BlockSpec((None, tq, CROSS_W), lambda b, i: (b, i, 0)),
        out_shape=jax.ShapeDtypeStruct((nb, s, CROSS_W), F32),
        compiler_params=_params("parallel", "parallel"),
        name="cross_attn",
    )(p3, p3, mk, mv)


def _branch_kernel(a_ref, z_ref, c_ref, gs_ref, gm0_ref, gm1_ref, gm2_ref,
                   wglu_ref, bglu_ref, wba_ref, wbs_ref, wbc_ref, o_ref, *, hi):
    gl = _dot(z_ref[...], wglu_ref[...], hi) + bglu_ref[...]
    s_out = gl[:, 0:SSM_W] * _sigmoid(gl[:, SSM_W:2 * SSM_W]) * _silu(gs_ref[...])
    merged = _sigmoid(gm0_ref[...]) * _dot(a_ref[...], wba_ref[...], hi)
    merged = merged + _sigmoid(gm1_ref[...]) * _dot(s_out, wbs_ref[...], hi)
    merged = merged + _sigmoid(gm2_ref[...]) * _dot(c_ref[...], wbc_ref[...], hi)
    o_ref[...] = merged


def _branch_merge(a2d, z2d, c2d, p2d, w_glu, b_glu, w_ba, w_bs, w_bc, tm, hi):
    m = a2d.shape[0]
    rows = lambda w, col=0: pl.BlockSpec((tm, w), lambda i: (i, col // w))
    const = lambda arr: pl.BlockSpec(arr.shape, lambda i: (0, 0), pipeline_mode=pl.Buffered(1))
    b2 = b_glu.reshape(1, -1)
    return pl.pallas_call(
        functools.partial(_branch_kernel, hi=hi),
        grid=(m // tm,),
        in_specs=[rows(ATT_W), rows(SSM_W), rows(CROSS_W), rows(SSM_W, COL_GS),
                  rows(D_MODEL, COL_GM), rows(D_MODEL, COL_GM + D_MODEL), rows(D_MODEL, COL_GM + 2 * D_MODEL),
                  const(w_glu), const(b2), const(w_ba), const(w_bs), const(w_bc)],
        out_specs=rows(D_MODEL),
        out_shape=jax.ShapeDtypeStruct((m, D_MODEL), F32),
        compiler_params=_params("parallel"),
        name="branch_merge",
    )(a2d, z2d, c2d, p2d, p2d, p2d, p2d, w_glu, b2, w_ba, w_bs, w_bc)


def _outproj_kernel(x_ref, m_ref, w_ref, o_ref, *, hi):
    o_ref[...] = x_ref[...] + _dot(m_ref[...], w_ref[...], hi)


def _outproj(x2d, merged, w_out, tm, hi):
    m, d = x2d.shape
    rows = pl.BlockSpec((tm, d), lambda i: (i, 0))
    return pl.pallas_call(
        functools.partial(_outproj_kernel, hi=hi),
        grid=(m // tm,),
        in_specs=[rows, rows, pl.BlockSpec((d, d), lambda i: (0, 0), pipeline_mode=pl.Buffered(1))],
        out_specs=rows,
        out_shape=jax.ShapeDtypeStruct((m, d), F32),
        compiler_params=_params("parallel"),
        name="outproj",
    )(x2d, merged, w_out)


def _mixer_layer(x3, attend, x0r, x0i, mk, mv, norm_g, w_in_t, layer, w_glu, b_glu, w_ba, w_bs, w_bc, w_out,
                 ssm, *, tm, n_scan, seg_len, chain, hi, tq):
    nb, s, d = x3.shape
    m = nb * s
    x2d = x3.reshape(m, d)
    tr = min(tm, 256)
    hh, hl = _rms_split(x2d, norm_g, tr)
    p2d = _inproj(hh, hl, w_in_t, layer, tm, 0 if hi else PROJ_LO_TILES)
    p3 = p2d.reshape(nb, s, PROJ_W)
    a_out = attend(p3)
    z, xr, xi = _s5(p2d.reshape(n_scan, m // n_scan, PROJ_W), x0r, x0i, *ssm,
                    seg_len=seg_len, chain=chain, hi=hi)
    c_out = _cross(p3, mk, mv, tq)
    merged = _branch_merge(a_out.reshape(m, ATT_W), z.reshape(m, SSM_W), c_out.reshape(m, CROSS_W),
                           p2d, w_glu, b_glu, w_ba, w_bs, w_bc, tr, hi)
    y = _outproj(x2d, merged, w_out, tr, hi).reshape(nb, s, d)
    k = p3[:, :, COL_K:COL_K + ATT_W].reshape(nb, s, ATT_HEADS, HEAD_DIM)
    v = p3[:, :, COL_V:COL_V + ATT_W].reshape(nb, s, ATT_HEADS, HEAD_DIM)
    ki = p3[:, :, COL_KI:COL_KI + IDX_DIM]
    return y, (k, v, ki, xr, xi)


def kernel(x_prompt, x_sample, cache_k, cache_v, cache_kidx, cache_mem_k, cache_mem_v, state_ssm_re, state_ssm_im, page_table, mem_prompt, norm_g, w_in, w_branch_attn, w_branch_ssm, w_branch_cross, w_out, w_mem_kv, ssm_a_re, ssm_a_im, ssm_b_re, ssm_b_im, ssm_c_re, ssm_c_im, ssm_d, ssm_log_dt, w_glu, b_glu, rel_bias, final_norm_g):
    depth = w_in.shape[0]
    b, s, d = x_prompt.shape
    db, t, _ = x_sample.shape
    mem_len = mem_prompt.shape[1]
    n_state = SSM_GROUPS * SSM_STATE
    assert s == SCAN_ROWS * (s // SCAN_ROWS) and db == SCAN_ROWS

    bias_tiles = _bias_tiles(rel_bias)
    w_in_t = jnp.transpose(w_in, (0, 2, 1))
    cache_k_t = jnp.transpose(cache_k, (0, 1, 3, 2, 4))
    cache_v_t = jnp.transpose(cache_v, (0, 1, 3, 2, 4))
    cache_kidx_t = jnp.transpose(cache_kidx, (0, 1, 3, 2))
    zeros_state = jnp.zeros((b, SCAN_ROWS, n_state), F32)
    yp, ys = x_prompt, x_sample
    outs = [[] for _ in range(12)]
    for l in range(depth):
        lb_re, lb_im, bb_re, bb_im = _ssm_params(ssm_a_re[l], ssm_a_im[l], ssm_log_dt[l],
                                                 ssm_b_re[l], ssm_b_im[l])
        wbu = jnp.concatenate([_block_diag_slabs(bb_re.transpose(1, 0, 2)),
                               _block_diag_slabs(bb_im.transpose(1, 0, 2))], axis=2)
        wyr = _block_diag_slabs(ssm_c_re[l].transpose(0, 2, 1))
        wyi = _block_diag_slabs(ssm_c_im[l].transpose(0, 2, 1))
        ssm = (lb_re.reshape(N_SLABS, 1, SLAB_STATES), lb_im.reshape(N_SLABS, 1, SLAB_STATES),
               wbu, wyr, wyi, ssm_d[l].reshape(N_SLABS, 1, LANES))

        mkv = _memkv(mem_prompt.reshape(b * mem_len, d), w_mem_kv[l])
        mk_p = mkv[:, :CROSS_W].reshape(b, mem_len, CROSS_W)
        mv_p = mkv[:, CROSS_W:].reshape(b, mem_len, CROSS_W)
        attend_p = functools.partial(_dsa_prompt, bias_tiles=bias_tiles, rel_bias=rel_bias)
        yp, (kp, vp, kip, xr, xi) = _mixer_layer(
            yp, attend_p, zeros_state, zeros_state, mk_p, mv_p, norm_g[l], w_in_t, l,
            w_glu[l].astype(BF16), b_glu[l], w_branch_attn[l].astype(BF16), w_branch_ssm[l].astype(BF16),
            w_branch_cross[l].astype(BF16), w_out[l].astype(BF16), ssm,
            tm=2048, n_scan=b, seg_len=s // SCAN_ROWS, chain=True, hi=False, tq=512)
        outs[0].append(kp); outs[1].append(vp); outs[2].append(kip)
        outs[3].append(mk_p.reshape(b, mem_len, CROSS_HEADS, CROSS_HD))
        outs[4].append(mv_p.reshape(b, mem_len, CROSS_HEADS, CROSS_HD))
        outs[5].append(xr[:, SCAN_ROWS - 1].reshape(b, SSM_GROUPS, SSM_STATE))
        outs[6].append(xi[:, SCAN_ROWS - 1].reshape(b, SSM_GROUPS, SSM_STATE))

        attend_s = functools.partial(_dsa_sample, l, cache_k_t=cache_k_t, cache_v_t=cache_v_t,
                                     cache_kidx_t=cache_kidx_t, page_table=page_table, rel_bias=rel_bias)
        ys, (kn, vn, kin, sr, si) = _mixer_layer(
            ys, attend_s,
            state_ssm_re[l].reshape(1, db, n_state), state_ssm_im[l].reshape(1, db, n_state),
            cache_mem_k[l].reshape(db, mem_len, CROSS_W), cache_mem_v[l].reshape(db, mem_len, CROSS_W),
            norm_g[l], w_in_t, l, w_glu[l], b_glu[l], w_branch_attn[l], w_branch_ssm[l],
            w_branch_cross[l], w_out[l], ssm,
            tm=db * t, n_scan=1, seg_len=t, chain=False, hi=True, tq=t)
        outs[7].append(kn); outs[8].append(vn); outs[9].append(kin)
        outs[10].append(sr.reshape(db, SSM_GROUPS, SSM_STATE)); outs[11].append(si.reshape(db, SSM_GROUPS, SSM_STATE))

    y_prompt = _rms(yp.reshape(b * s, d), final_norm_g, 256).reshape(b, s, d)
    y_sample = _rms(ys.reshape(db * t, d), final_norm_g, db * t).reshape(db, t, d)
    return (y_prompt, y_sample) + tuple(jnp.stack(o, axis=0) for o in outs)
```

```python
import functools
import math

import jax
import jax.numpy as jnp
import numpy as np
from jax import lax
from jax.experimental import pallas as pl
from jax.experimental.pallas import tpu as pltpu

F32 = jnp.float32
BF16 = jnp.bfloat16
I32 = jnp.int32
HIGHEST = lax.Precision.HIGHEST
NT_DIMS = (((1,), (1,)), ((), ()))

VMEM_LIMIT_BYTES = 56 * 1024 * 1024
LANES = 128
SUBLANES = 8

D_MODEL = 2048
HEAD_DIM = 128
ATT_HEADS = 6
ATT_W = ATT_HEADS * HEAD_DIM
IDX_HEADS = 16
IDX_DIM = 64
IDX_W = IDX_HEADS * IDX_DIM
TOPK_MAX = 256
QBLOCK = 128
SSM_GROUP = 16
SSM_W = 768
SSM_GROUPS = SSM_W // SSM_GROUP
SSM_STATE = 64
CROSS_HEADS = 4
CROSS_HD = 128
CROSS_W = CROSS_HEADS * CROSS_HD
REL_BUCKETS = 32
REL_MAX_DIST = 128
N_BRANCHES = 3
EPS = 1e-6
SPLIT_WIDTHS = (ATT_W, ATT_W, ATT_W, ATT_W, IDX_W, IDX_HEADS, IDX_DIM,
                SSM_W, SSM_W, CROSS_W, CROSS_W, N_BRANCHES * D_MODEL)
SPLIT_POINTS = tuple(int(c) for c in np.cumsum(SPLIT_WIDTHS)[:-1])

COL_GM = 0
COL_Q = 6144
COL_K = COL_Q + ATT_W
COL_V = COL_K + ATT_W
COL_GA = COL_V + ATT_W
COL_U = COL_GA + ATT_W
COL_GS = COL_U + SSM_W
COL_QC = COL_GS + SSM_W
COL_GC = COL_QC + CROSS_W
COL_KIWI = COL_GC + CROSS_W
COL_WI = COL_KIWI
COL_KI = COL_KIWI + IDX_HEADS
COL_QI = 12288
PROJ_W = COL_QI + IDX_W
PROJ_TN = 512
PROJ_ROW_ALIGN = 16
PROJ_LO_TILES = COL_KIWI // PROJ_TN


def _proj_tile_rows():
    ref_start = dict(zip(("q", "k", "v", "ga", "qi", "wi", "ki", "u", "gs", "qc", "gc", "gm"),
                         (0,) + SPLIT_POINTS))
    runs = ((COL_GM, "gm", N_BRANCHES * D_MODEL), (COL_Q, "q", 4 * ATT_W), (COL_U, "u", 2 * SSM_W),
            (COL_QC, "qc", 2 * CROSS_W), (COL_KIWI, "wi", PROJ_TN), (COL_QI, "qi", IDX_W))
    rows = np.zeros(PROJ_W // PROJ_TN, np.int32)
    for col, name, width in runs:
        for t in range(width // PROJ_TN):
            rows[col // PROJ_TN + t] = ref_start[name] + t * PROJ_TN
    assert not (rows % PROJ_ROW_ALIGN).any()
    return rows

SLAB_GROUPS = LANES // SSM_GROUP
SLAB_STATES = SLAB_GROUPS * SSM_STATE
N_SLABS = SSM_GROUPS // SLAB_GROUPS
SCAN_ROWS = SUBLANES

KEY_EXTENT_CLASSES = 8
COUNT_FOLD_ROWS = 64

PAGES_PER_STEP = 8

INT_MIN = -2 ** 31


def _params(*sem):
    return pltpu.CompilerParams(dimension_semantics=sem, vmem_limit_bytes=VMEM_LIMIT_BYTES)


def _sigmoid(x):
    return 1.0 / (1.0 + jnp.exp(-x))


def _silu(x):
    return x * _sigmoid(x)


def _gelu_tanh(x):
    return 0.5 * x * (1.0 + jnp.tanh(math.sqrt(2.0 / math.pi) * (x + 0.044715 * (x * x * x))))


def _dot(a, b, hi):
    if hi:
        return jnp.dot(a.astype(F32), b.astype(F32), precision=HIGHEST, preferred_element_type=F32)
    return jnp.dot(a.astype(BF16), b.astype(BF16), preferred_element_type=F32)


def _split_bf16(x):
    hi = x.astype(BF16).astype(F32)
    return hi, x - hi


def _sortable_key(score):
    b = lax.bitcast_convert_type(score, I32)
    b = jnp.where(b == I32(INT_MIN), I32(0), b)
    key = b ^ ((b >> 31) & I32(0x7FFFFFFF))
    return jnp.where(score == -jnp.inf, I32(INT_MIN), key)


def _kth_largest_key(key, k, axis=1):
    def count_ge(c):
        ones = (key >= c).astype(I32)
        if axis == 0:
            while ones.shape[0] > COUNT_FOLD_ROWS and ones.shape[0] % (2 * SUBLANES) == 0:
                half = ones.shape[0] // 2
                ones = ones[:half] + ones[half:]
            return jnp.sum(ones.astype(F32), axis=0, keepdims=True).astype(I32)
        return jnp.sum(ones, axis=axis, keepdims=True)

    zero = jnp.zeros((key.shape[0], 1) if axis == 1 else (1, key.shape[1]), I32)
    t = jnp.where(count_ge(zero) >= k, zero, jnp.full_like(zero, INT_MIN))

    def body(it, t):
        cand = t | (I32(1) << (I32(30) - it))
        return jnp.where(count_ge(cand) >= k, cand, t)

    return lax.fori_loop(0, 31, body, t)


def _t5_bucket(dist):
    n = jnp.maximum(dist, 0)
    max_exact = REL_BUCKETS // 2
    nf = jnp.maximum(n, 1).astype(F32)
    scale = (REL_BUCKETS - max_exact) / math.log(REL_MAX_DIST / max_exact)
    large = max_exact + (jnp.log(nf * (1.0 / max_exact)) * scale).astype(I32)
    large = jnp.minimum(large, REL_BUCKETS - 1)
    return jnp.where(n < max_exact, n, large)


def _rms_split_kernel(x_ref, g_ref, hh_ref, hl_ref):
    x = x_ref[...]
    y = x * lax.rsqrt(jnp.mean(x * x, axis=-1, keepdims=True) + EPS) * g_ref[...]
    hh = y.astype(BF16)
    hh_ref[...] = hh
    hl_ref[...] = (y - hh.astype(F32)).astype(BF16)


def _rms_split(x2d, g, tm):
    m, d = x2d.shape
    return pl.pallas_call(
        _rms_split_kernel,
        grid=(m // tm,),
        in_specs=[pl.BlockSpec((tm, d), lambda i: (i, 0)), pl.BlockSpec((1, d), lambda i: (0, 0))],
        out_specs=[pl.BlockSpec((tm, d), lambda i: (i, 0))] * 2,
        out_shape=[jax.ShapeDtypeStruct((m, d), BF16)] * 2,
        compiler_params=_params("parallel"),
        name="rms_split",
    )(x2d, g.reshape(1, d))


def _rms_kernel(x_ref, g_ref, o_ref):
    x = x_ref[...]
    o_ref[...] = x * lax.rsqrt(jnp.mean(x * x, axis=-1, keepdims=True) + EPS) * g_ref[...]


def _rms(x2d, g, tm):
    m, d = x2d.shape
    return pl.pallas_call(
        _rms_kernel,
        grid=(m // tm,),
        in_specs=[pl.BlockSpec((tm, d), lambda i: (i, 0)), pl.BlockSpec((1, d), lambda i: (0, 0))],
        out_specs=pl.BlockSpec((tm, d), lambda i: (i, 0)),
        out_shape=jax.ShapeDtypeStruct((m, d), F32),
        compiler_params=_params("parallel"),
        name="rms_final",
    )(x2d, g.reshape(1, d))


def _inproj_kernel(rows_ref, hh_ref, hl_ref, wt_ref, o_ref, *, lo_tiles):
    j = pl.program_id(1)
    w = wt_ref[...]
    w_hi = w.astype(BF16)

    @pl.when(j < lo_tiles)
    def _():
        o_ref[...] = lax.dot_general(hh_ref[...], w_hi, NT_DIMS, preferred_element_type=F32)

    @pl.when(j >= lo_tiles)
    def _():
        w_lo = (w - w_hi.astype(F32)).astype(BF16)
        hh = hh_ref[...]
        acc = lax.dot_general(hh, w_hi, NT_DIMS, preferred_element_type=F32)
        acc = acc + lax.dot_general(hl_ref[...], w_hi, NT_DIMS, preferred_element_type=F32)
        o_ref[...] = acc + lax.dot_general(hh, w_lo, NT_DIMS, preferred_element_type=F32)


def _inproj(hh, hl, w_in_t, layer, tm, lo_tiles):
    m, d = hh.shape
    n_tiles = PROJ_W // PROJ_TN
    resident = lambda: pl.BlockSpec((tm, d), lambda i, j, rows: (i, 0), pipeline_mode=pl.Buffered(1))
    return pl.pallas_call(
        functools.partial(_inproj_kernel, lo_tiles=lo_tiles),
        grid_spec=pltpu.PrefetchScalarGridSpec(
            num_scalar_prefetch=1,
            grid=(m // tm, n_tiles),
            in_specs=[resident(), resident(),
                      pl.BlockSpec((None, pl.Element(PROJ_TN), pl.Element(d)),
                                   lambda i, j, rows: (layer, rows[j] * PROJ_ROW_ALIGN, 0))],
            out_specs=pl.BlockSpec((tm, PROJ_TN), lambda i, j, rows: (i, j))),
        out_shape=jax.ShapeDtypeStruct((m, PROJ_W), F32),
        compiler_params=_params("parallel", "arbitrary"),
        name="inproj",
    )(jnp.asarray(_proj_tile_rows() // PROJ_ROW_ALIGN), hh, hl, w_in_t)


def _memkv_kernel(a_ref, w_ref, o_ref):
    o_ref[...] = _dot(a_ref[...], w_ref[...], False)


def _memkv(mem2d, w):
    m, d = mem2d.shape
    n = w.shape[1]
    tn = 512
    return pl.pallas_call(
        _memkv_kernel,
        grid=(n // tn,),
        in_specs=[pl.BlockSpec((m, d), lambda j: (0, 0)), pl.BlockSpec((d, tn), lambda j: (0, j))],
        out_specs=pl.BlockSpec((m, tn), lambda j: (0, j)),
        out_shape=jax.ShapeDtypeStruct((m, n), F32),
        compiler_params=_params("parallel"),
        name="memkv",
    )(mem2d, w)


def _bias_tile_kernel(rb_ref, o_ref):
    c = lax.broadcasted_iota(I32, (2 * QBLOCK, QBLOCK), 0)
    r = lax.broadcasted_iota(I32, (2 * QBLOCK, QBLOCK), 1)
    bucket = _t5_bucket(r - c + QBLOCK)
    for h in range(ATT_HEADS):
        tile = jnp.zeros((2 * QBLOCK, QBLOCK), F32)
        for b in range(REL_BUCKETS):
            tile = jnp.where(bucket == b, rb_ref[b, h], tile)
        o_ref[h] = tile


def _bias_tiles(rel_bias):
    return pl.pallas_call(
        _bias_tile_kernel,
        in_specs=[pl.BlockSpec(memory_space=pltpu.SMEM)],
        out_specs=pl.BlockSpec(memory_space=pltpu.VMEM),
        out_shape=jax.ShapeDtypeStruct((ATT_HEADS, 2 * QBLOCK, QBLOCK), F32),
        name="bias_tiles",
    )(rel_bias)


def _dsa_prompt_kernel(rb_ref, q_ref, ga_ref, qi_ref, kwq_ref, k_ref, v_ref, kwk_ref, bt_ref, o_ref,
                       ki4_ref, kb_ref, vt_ref, *, seq, topk):
    i = pl.program_id(1)

    @pl.when(i == 0)
    def _():
        kh, kl = _split_bf16(kwk_ref[:, IDX_HEADS:IDX_HEADS + IDX_DIM])
        ki4_ref[...] = jnp.concatenate([kh, kl, kh, kl], axis=1).astype(BF16)
        kb_ref[...] = k_ref[...].astype(BF16)
        for kb in range(seq // QBLOCK):
            for h in range(ATT_HEADS):
                rows = slice(kb * QBLOCK, (kb + 1) * QBLOCK)
                cols = slice(h * HEAD_DIM, (h + 1) * HEAD_DIM)
                vt_ref[cols, rows] = v_ref[rows, cols].T.astype(BF16)

    n_blk = seq // QBLOCK
    per = n_blk // KEY_EXTENT_CLASSES
    for cls in range(KEY_EXTENT_CLASSES):
        pl.when(i // per == cls)(functools.partial(
            _dsa_prompt_block, i, (cls + 1) * per * QBLOCK, topk,
            rb_ref, q_ref, ga_ref, qi_ref, kwq_ref, bt_ref, o_ref, ki4_ref, kb_ref, vt_ref))


def _dsa_prompt_block(i, nk, topk, rb_ref, q_ref, ga_ref, qi_ref, kwq_ref, bt_ref, o_ref,
                      ki4_ref, kb_ref, vt_ref):
    qi = qi_ref[...]
    wi_t = kwq_ref[...].T * (IDX_HEADS ** -0.5)
    score = jnp.zeros((nk, QBLOCK), F32)
    for h0 in range(0, IDX_HEADS, 2):
        q4 = []
        for h in (h0, h0 + 1):
            qh, ql = _split_bf16(qi[:, h * IDX_DIM:(h + 1) * IDX_DIM])
            q4.append(jnp.concatenate([qh, qh, ql, ql], axis=1).astype(BF16))
        s = lax.dot_general(ki4_ref[0:nk, :], jnp.concatenate(q4, axis=0), NT_DIMS,
                            preferred_element_type=F32)
        score = score + jnp.maximum(s[:, 0:QBLOCK], 0.0) * wi_t[h0:h0 + 1, :]
        score = score + jnp.maximum(s[:, QBLOCK:2 * QBLOCK], 0.0) * wi_t[h0 + 1:h0 + 2, :]

    key_pos = lax.broadcasted_iota(I32, (nk, QBLOCK), 0)
    q_pos = i * QBLOCK + lax.broadcasted_iota(I32, (nk, QBLOCK), 1)
    key = jnp.where(key_pos <= q_pos, _sortable_key(score), I32(INT_MIN))
    thr = jnp.maximum(_kth_largest_key(key, topk, axis=0), I32(INT_MIN + 1))
    mask_add = jnp.where(key >= thr, 0.0, -jnp.inf)

    key_blk = key_pos // QBLOCK
    is_diag = key_blk == i
    is_prev = key_blk == i - 1
    reps = nk // QBLOCK
    q = q_ref[...]
    ga = ga_ref[...]
    for h in range(ATT_HEADS):
        cols = slice(h * HEAD_DIM, (h + 1) * HEAD_DIM)
        t_prev = jnp.concatenate([bt_ref[h, 0:QBLOCK, :]] * reps, axis=0)
        t_diag = jnp.concatenate([bt_ref[h, QBLOCK:2 * QBLOCK, :]] * reps, axis=0)
        bias = jnp.where(is_diag, t_diag, jnp.where(is_prev, t_prev, rb_ref[REL_BUCKETS - 1, h]))
        logits = lax.dot_general(kb_ref[0:nk, cols], q[:, cols].astype(BF16), NT_DIMS,
                                 preferred_element_type=F32)
        logits = logits * (HEAD_DIM ** -0.5) + bias + mask_add
        m = jnp.max(logits, axis=0, keepdims=True)
        p = jnp.exp(logits - m)
        l = jnp.sum(p, axis=0, keepdims=True)
        o_t = jnp.dot(vt_ref[cols, 0:nk], p.astype(BF16), preferred_element_type=F32)
        o_ref[:, cols] = (o_t / l).T * _silu(ga[:, cols])


def _dsa_prompt(p3, bias_tiles, rel_bias):
    b, s, _ = p3.shape
    topk = min(TOPK_MAX, s // 4)
    row = lambda col, w: pl.BlockSpec((None, QBLOCK, w), lambda bi, i: (bi, i, col // w))
    full = lambda col, w: pl.BlockSpec((None, s, w), lambda bi, i: (bi, 0, col // w))
    return pl.pallas_call(
        functools.partial(_dsa_prompt_kernel, seq=s, topk=topk),
        grid=(b, s // QBLOCK),
        in_specs=[pl.BlockSpec(memory_space=pltpu.SMEM),
                  row(COL_Q, ATT_W), row(COL_GA, ATT_W), row(COL_QI, IDX_W), row(COL_KIWI, LANES),
                  full(COL_K, ATT_W), full(COL_V, ATT_W), full(COL_KIWI, LANES),
                  pl.BlockSpec((ATT_HEADS, 2 * QBLOCK, QBLOCK), lambda bi, i: (0, 0, 0))],
        out_specs=pl.BlockSpec((None, QBLOCK, ATT_W), lambda bi, i: (bi, i, 0)),
        out_shape=jax.ShapeDtypeStruct((b, s, ATT_W), F32),
        scratch_shapes=[pltpu.VMEM((s, 4 * IDX_DIM), BF16),
                        pltpu.VMEM((s, ATT_W), BF16),
                        pltpu.VMEM((ATT_W, s), BF16)],
        compiler_params=_params("parallel", "arbitrary"),
        name="dsa_prompt",
    )(rel_bias, p3, p3, p3, p3, p3, p3, p3, bias_tiles)


def _sidx_kernel(pt_ref, q_ref, w_ref, *rest, nch):
    k_refs = rest[:PAGES_PER_STEP]
    kn_ref, s_ref, q4_ref = rest[PAGES_PER_STEP:]
    c = pl.program_id(1)

    @pl.when(c == 0)
    def _():
        qh, ql = _split_bf16(q_ref[...])
        q4_ref[...] = jnp.concatenate([qh, qh, ql, ql], axis=1).astype(BF16)

    def scores(ki_t):
        n = ki_t.shape[1]
        kh, kl = _split_bf16(ki_t)
        ki4_t = jnp.concatenate([kh, kl, kh, kl], axis=0).astype(BF16)
        s = jnp.dot(q4_ref[...], ki4_t, preferred_element_type=F32)
        s = jnp.maximum(s, 0.0) * (w_ref[...] * (IDX_HEADS ** -0.5))
        return jnp.sum(s.reshape(IDX_HEADS, SUBLANES, n), axis=0)

    @pl.when(c < nch)
    def _():
        s_ref[...] = scores(jnp.concatenate([r[...] for r in k_refs], axis=1))

    @pl.when(c == nch)
    def _():
        n = kn_ref.shape[1]
        sn = scores(kn_ref[...])
        t_key = lax.broadcasted_iota(I32, (SUBLANES, n), 1)
        t_q = lax.broadcasted_iota(I32, (SUBLANES, n), 0)
        s_ref[...] = jnp.full(s_ref.shape, -jnp.inf, F32)
        s_ref[:, 0:n] = jnp.where(t_key <= t_q, sn, -jnp.inf)


def _sattn_kernel(pt_ref, rb_ref, sf_ref, sc_ref, q_ref, ga_ref, *rest, nch, past, topk, page):
    k_refs = rest[:PAGES_PER_STEP]
    v_refs = rest[PAGES_PER_STEP:2 * PAGES_PER_STEP]
    kn_ref, vn_ref, o_ref, thr_ref, m_ref, l_ref, acc_ref = rest[2 * PAGES_PER_STEP:]
    c = pl.program_id(1)
    chunk = PAGES_PER_STEP * page

    @pl.when(c == 0)
    def _():
        t = _kth_largest_key(_sortable_key(sf_ref[...]), topk)
        thr_ref[...] = jnp.maximum(t, I32(INT_MIN + 1))
        m_ref[...] = jnp.full(m_ref.shape, -jnp.inf, F32)
        l_ref[...] = jnp.zeros(l_ref.shape, F32)
        acc_ref[...] = jnp.zeros(acc_ref.shape, F32)

    def attend(k_of, v_of, sc, key_pos0):
        n = sc.shape[1]
        mask_add = jnp.where(_sortable_key(sc) >= thr_ref[...], 0.0, -jnp.inf)
        q_pos = past + lax.broadcasted_iota(I32, (SUBLANES, n), 0)
        key_pos = key_pos0 + lax.broadcasted_iota(I32, (SUBLANES, n), 1)
        bucket = _t5_bucket(q_pos - key_pos)
        q = q_ref[...]
        rows = []
        for h in range(ATT_HEADS):
            cols = slice(h * HEAD_DIM, (h + 1) * HEAD_DIM)
            bias = jnp.zeros((SUBLANES, n), F32)
            for b in range(REL_BUCKETS):
                bias = jnp.where(bucket == b, rb_ref[b, h], bias)
            lg = lax.dot_general(q[:, cols].astype(BF16), k_of(h).astype(BF16), NT_DIMS,
                                 preferred_element_type=F32)
            rows.append(lg * (HEAD_DIM ** -0.5) + bias + mask_add)
        logits = jnp.concatenate(rows, axis=0)
        m_old = m_ref[...]
        m_new = jnp.maximum(m_old, jnp.max(logits, axis=1, keepdims=True))
        m_safe = jnp.where(m_new == -jnp.inf, 0.0, m_new)
        alpha = jnp.exp(m_old - m_safe)
        p = jnp.exp(logits - m_safe[:, 0:1])
        l_ref[...] = alpha * l_ref[...] + jnp.sum(p, axis=1, keepdims=True)
        m_ref[...] = m_new
        for h in range(ATT_HEADS):
            cols = slice(h * HEAD_DIM, (h + 1) * HEAD_DIM)
            rs = slice(h * SUBLANES, (h + 1) * SUBLANES)
            acc_ref[:, cols] = alpha[rs] * acc_ref[:, cols] + jnp.dot(
                p[rs].astype(BF16), v_of(h).astype(BF16), preferred_element_type=F32)

    @pl.when(c < nch)
    def _():
        k_of = lambda h: jnp.concatenate([r[h] for r in k_refs], axis=0)
        v_of = lambda h: jnp.concatenate([r[h] for r in v_refs], axis=0)
        attend(k_of, v_of, sc_ref[...], c * chunk)

    @pl.when(c == nch)
    def _():
        n = kn_ref.shape[0]
        head = lambda ref: (lambda h: ref[:, h * HEAD_DIM:(h + 1) * HEAD_DIM])
        attend(head(kn_ref), head(vn_ref), sc_ref[:, 0:n], past)
        ga = ga_ref[...]
        l = l_ref[...]
        for h in range(ATT_HEADS):
            cols = slice(h * HEAD_DIM, (h + 1) * HEAD_DIM)
            o_ref[:, cols] = acc_ref[:, cols] / l[h * SUBLANES:(h + 1) * SUBLANES] * _silu(ga[:, cols])


def _dsa_sample(layer, ps3, cache_k_t, cache_v_t, cache_kidx_t, page_table, rel_bias):
    db, t, _ = ps3.shape
    n_pages = page_table.shape[1]
    page = cache_kidx_t.shape[3]
    past = n_pages * page
    topk = min(TOPK_MAX, (past + t) // 4)
    nch = n_pages // PAGES_PER_STEP
    chunk = PAGES_PER_STEP * page
    ntot = (nch + 1) * chunk

    qi = ps3[:, :, COL_QI:COL_QI + IDX_W].reshape(db, t, IDX_HEADS, IDX_DIM)
    qi_hq = qi.transpose(0, 2, 1, 3).reshape(db, IDX_HEADS * t, IDX_DIM)
    wi_hq = ps3[:, :, COL_WI:COL_WI + IDX_HEADS].transpose(0, 2, 1).reshape(db, IDX_HEADS * t, 1)
    pad_rows = lambda a: jnp.pad(a, ((0, 0), (0, page - t), (0, 0)))
    ki_new_t = pad_rows(ps3[:, :, COL_KI:COL_KI + IDX_DIM]).transpose(0, 2, 1)
    k_new = pad_rows(ps3[:, :, COL_K:COL_K + ATT_W])
    v_new = pad_rows(ps3[:, :, COL_V:COL_V + ATT_W])
    q = ps3[:, :, COL_Q:COL_Q + ATT_W]
    ga = ps3[:, :, COL_GA:COL_GA + ATT_W]

    def page_spec(r, *dims):
        def imap(b, c, pt):
            return (layer, pt[b, jnp.minimum(c * PAGES_PER_STEP + r, n_pages - 1)]) + (0,) * len(dims)
        return pl.BlockSpec((None, None) + dims, imap)

    per_b = lambda rows, w: pl.BlockSpec((None, rows, w), lambda b, c, pt: (b, 0, 0))

    scores = pl.pallas_call(
        functools.partial(_sidx_kernel, nch=nch),
        grid_spec=pltpu.PrefetchScalarGridSpec(
            num_scalar_prefetch=1,
            grid=(db, nch + 1),
            in_specs=[per_b(IDX_HEADS * t, IDX_DIM), per_b(IDX_HEADS * t, 1)]
                     + [page_spec(r, IDX_DIM, page) for r in range(PAGES_PER_STEP)]
                     + [per_b(IDX_DIM, page)],
            out_specs=pl.BlockSpec((None, t, chunk), lambda b, c, pt: (b, 0, c)),
            scratch_shapes=[pltpu.VMEM((IDX_HEADS * t, 4 * IDX_DIM), BF16)]),
        out_shape=jax.ShapeDtypeStruct((db, t, ntot), F32),
        compiler_params=_params("parallel", "arbitrary"),
        name="dsa_sample_scores",
    )(page_table, qi_hq, wi_hq, *([cache_kidx_t] * PAGES_PER_STEP), ki_new_t)

    return pl.pallas_call(
        functools.partial(_sattn_kernel, nch=nch, past=past, topk=topk, page=page),
        grid_spec=pltpu.PrefetchScalarGridSpec(
            num_scalar_prefetch=1,
            grid=(db, nch + 1),
            in_specs=[pl.BlockSpec(memory_space=pltpu.SMEM),
                      per_b(t, ntot),
                      pl.BlockSpec((None, t, chunk), lambda b, c, pt: (b, 0, c)),
                      per_b(t, ATT_W), per_b(t, ATT_W)]
                     + [page_spec(r, ATT_HEADS, page, HEAD_DIM) for r in range(PAGES_PER_STEP)] * 2
                     + [per_b(page, ATT_W), per_b(page, ATT_W)],
            out_specs=per_b(t, ATT_W),
            scratch_shapes=[pltpu.VMEM((t, 1), I32),
                            pltpu.VMEM((ATT_HEADS * t, LANES), F32),
                            pltpu.VMEM((ATT_HEADS * t, LANES), F32),
                            pltpu.VMEM((t, ATT_W), F32)]),
        out_shape=jax.ShapeDtypeStruct((db, t, ATT_W), F32),
        compiler_params=_params("parallel", "arbitrary"),
        name="dsa_sample_attn",
    )(page_table, rel_bias, scores, scores, q, ga,
      *([cache_k_t] * PAGES_PER_STEP), *([cache_v_t] * PAGES_PER_STEP), k_new, v_new)


def _ssm_param_kernel(are_ref, aim_ref, ldt_ref, bre_ref, bim_ref, lbr_ref, lbi_ref, bbr_ref, bbi_ref):
    lam_re = jnp.minimum(are_ref[...], -1e-4)
    lam_im = aim_ref[...]
    step = jnp.exp(ldt_ref[...])
    mag = jnp.exp(lam_re * step)
    ang = lam_im * step
    lb_re = mag * jnp.cos(ang)
    lb_im = mag * jnp.sin(ang)
    nr = lb_re - 1.0
    ni = lb_im
    den = lam_re * lam_re + lam_im * lam_im
    coef_re = (nr * lam_re + ni * lam_im) / den
    coef_im = (ni * lam_re - nr * lam_im) / den
    lbr_ref[...] = lb_re
    lbi_ref[...] = lb_im
    for c in range(SSM_GROUP):
        br = bre_ref[c]
        bi = bim_ref[c]
        bbr_ref[c] = coef_re * br - coef_im * bi
        bbi_ref[c] = coef_re * bi + coef_im * br


def _ssm_params(a_re, a_im, log_dt, b_re, b_im):
    g, p, c = b_re.shape
    vm = pl.BlockSpec(memory_space=pltpu.VMEM)
    return pl.pallas_call(
        _ssm_param_kernel,
        in_specs=[vm] * 5,
        out_specs=[vm] * 4,
        out_shape=[jax.ShapeDtypeStruct((g, p), F32)] * 2 + [jax.ShapeDtypeStruct((c, g, p), F32)] * 2,
        name="ssm_params",
    )(a_re, a_im, log_dt.reshape(g, 1), b_re.transpose(2, 0, 1), b_im.transpose(2, 0, 1))


def _block_diag_slabs(m):
    g, a, b = m.shape
    eye = jnp.eye(SLAB_GROUPS, dtype=m.dtype)
    m4 = m.reshape(N_SLABS, SLAB_GROUPS, a, b)
    return jnp.einsum('sgab,gh->sgahb', m4, eye).reshape(N_SLABS, SLAB_GROUPS * a, SLAB_GROUPS * b)


def _scan_pitch(seg_len):
    return seg_len + 4 if seg_len % 8 == 0 else seg_len


def _s5_kernel(u_ref, x0r_ref, x0i_ref, lbr_ref, lbi_ref, wbu_ref, wyr_ref, wyi_ref, d_ref,
               z_ref, xr_ref, xi_ref, up_ref, zs_ref, *, seg_len, chain, hi):
    nc = SLAB_STATES // LANES
    pitch = _scan_pitch(seg_len)
    up_ref[...] = jnp.zeros(up_ref.shape, F32)
    for s in range(SCAN_ROWS):
        up_ref[pl.ds(s * pitch, seg_len), :] = u_ref[pl.ds(s * seg_len, seg_len), :]
    u = up_ref[...]
    bu = _dot(u, wbu_ref[...], hi)
    for j in range(2 * nc):
        zs_ref[j] = bu[:, j * LANES:(j + 1) * LANES]
    shape = (SCAN_ROWS, SLAB_STATES)
    lbr = jnp.broadcast_to(lbr_ref[...], shape)
    lbi = jnp.broadcast_to(lbi_ref[...], shape)

    def load(t, base):
        rows = pl.ds(t, SCAN_ROWS, stride=pitch)
        return jnp.concatenate([zs_ref[base + j, rows, :] for j in range(nc)], axis=1)

    def store(t, base, val):
        rows = pl.ds(t, SCAN_ROWS, stride=pitch)
        for j in range(nc):
            zs_ref[base + j, rows, :] = val[:, j * LANES:(j + 1) * LANES]

    def advance(t, carry):
        xr, xi = carry
        nr = lbr * xr - lbi * xi + load(t, 0)
        ni = lbr * xi + lbi * xr + load(t, nc)
        return nr, ni

    def advance_store(t, carry):
        nr, ni = advance(t, carry)
        store(t, 0, nr)
        store(t, nc, ni)
        return nr, ni

    unroll = min(seg_len, 8)
    x0r = x0r_ref[...]
    x0i = x0i_ref[...]
    if chain:
        zero = jnp.zeros(shape, F32)
        er, ei = lax.fori_loop(0, seg_len, advance, (zero, zero), unroll=unroll)
        pr, pi = lbr, lbi
        for _ in range(int(math.log2(seg_len))):
            pr, pi = pr * pr - pi * pi, 2.0 * pr * pi
        row = lax.broadcasted_iota(I32, shape, 0)
        sr = jnp.where(row == 0, x0r, 0.0)
        si = jnp.where(row == 0, x0i, 0.0)
        for s in range(SCAN_ROWS - 1):
            nr = er + pr * sr - pi * si
            ni = ei + pr * si + pi * sr
            sr = jnp.where(row == s + 1, pltpu.roll(nr, 1, axis=0), sr)
            si = jnp.where(row == s + 1, pltpu.roll(ni, 1, axis=0), si)
        x0r, x0i = sr, si
    xr, xi = lax.fori_loop(0, seg_len, advance_store, (x0r, x0i), unroll=unroll)
    xr_ref[...] = xr
    xi_ref[...] = xi
    y = d_ref[...] * u
    for j in range(nc):
        blk = slice(j * LANES, (j + 1) * LANES)
        y = y + _dot(zs_ref[j], wyr_ref[blk, :], hi) - _dot(zs_ref[nc + j], wyi_ref[blk, :], hi)
    up_ref[...] = _gelu_tanh(y)
    for s in range(SCAN_ROWS):
        z_ref[pl.ds(s * seg_len, seg_len), :] = up_ref[pl.ds(s * pitch, seg_len), :]


def _s5(p3, x0r, x0i, lbr, lbi, wbu, wyr, wyi, d, *, seg_len, chain, hi):
    nb, r, _ = p3.shape
    assert r == SCAN_ROWS * seg_len and (not chain or seg_len & (seg_len - 1) == 0)
    rp = SCAN_ROWS * _scan_pitch(seg_len)
    slab = lambda rows, w: pl.BlockSpec((None, rows, w), lambda b, s: (s, 0, 0))
    st = pl.BlockSpec((None, SCAN_ROWS, SLAB_STATES), lambda b, s: (b, 0, s))
    return pl.pallas_call(
        functools.partial(_s5_kernel, seg_len=seg_len, chain=chain, hi=hi),
        grid=(nb, N_SLABS),
        in_specs=[pl.BlockSpec((None, r, LANES), lambda b, s: (b, 0, COL_U // LANES + s)),
                  st, st, slab(1, SLAB_STATES), slab(1, SLAB_STATES),
                  slab(LANES, 2 * SLAB_STATES), slab(SLAB_STATES, LANES), slab(SLAB_STATES, LANES),
                  slab(1, LANES)],
        out_specs=[pl.BlockSpec((None, r, LANES), lambda b, s: (b, 0, s)), st, st],
        out_shape=[jax.ShapeDtypeStruct((nb, r, SSM_W), F32),
                   jax.ShapeDtypeStruct((nb, SCAN_ROWS, SSM_GROUPS * SSM_STATE), F32),
                   jax.ShapeDtypeStruct((nb, SCAN_ROWS, SSM_GROUPS * SSM_STATE), F32)],
        scratch_shapes=[pltpu.VMEM((rp, LANES), F32),
                        pltpu.VMEM((2 * SLAB_STATES // LANES, rp, LANES), F32)],
        compiler_params=_params("parallel", "parallel"),
        name="s5_scan",
    )(p3, x0r, x0i, lbr, lbi, wbu, wyr, wyi, d)


def _cross_kernel(q_ref, g_ref, mk_ref, mv_ref, o_ref):
    q = q_ref[...]
    g = g_ref[...]
    for h in range(CROSS_HEADS):
        cols = slice(h * CROSS_HD, (h + 1) * CROSS_HD)
        logits = lax.dot_general(q[:, cols].astype(BF16), mk_ref[:, cols].astype(BF16), NT_DIMS,
                                 preferred_element_type=F32) * (CROSS_HD ** -0.5)
        m = jnp.max(logits, axis=1, keepdims=True)
        p = jnp.exp(logits - m)
        l = jnp.sum(p, axis=1, keepdims=True)
        o = jnp.dot(p.astype(BF16), mv_ref[:, cols].astype(BF16), preferred_element_type=F32)
        o_ref[:, cols] = o / l * _silu(g[:, cols])


def _cross(p3, mk, mv, tq):
    nb, s, _ = p3.shape
    mem = mk.shape[1]
    qspec = lambda col: pl.BlockSpec((None, tq, CROSS_W), lambda b, i: (b, i, col // CROSS_W))
    mspec = pl.BlockSpec((None, mem, CROSS_W), lambda b, i: (b, 0, 0))
    return pl.pallas_call(
        _cross_kernel,
        grid=(nb, s // tq),
        in_specs=[qspec(COL_QC), qspec(COL_GC), mspec, mspec],
        out_specs=pl.BlockSpec((None, tq, CROSS_W), lambda b, i: (b, i, 0)),
        out_shape=jax.ShapeDtypeStruct((nb, s, CROSS_W), F32),
        compiler_params=_params("parallel", "parallel"),
        name="cross_attn",
    )(p3, p3, mk, mv)


def _branch_kernel(a_ref, z_ref, c_ref, gs_ref, gm0_ref, gm1_ref, gm2_ref,
                   wglu_ref, bglu_ref, wba_ref, wbs_ref, wbc_ref, o_ref, *, hi):
    gl = _dot(z_ref[...], wglu_ref[...], hi) + bglu_ref[...]
    s_out = gl[:, 0:SSM_W] * _sigmoid(gl[:, SSM_W:2 * SSM_W]) * _silu(gs_ref[...])
    merged = _sigmoid(gm0_ref[...]) * _dot(a_ref[...], wba_ref[...], hi)
    merged = merged + _sigmoid(gm1_ref[...]) * _dot(s_out, wbs_ref[...], hi)
    merged = merged + _sigmoid(gm2_ref[...]) * _dot(c_ref[...], wbc_ref[...], hi)
    o_ref[...] = merged


def _branch_merge(a2d, z2d, c2d, p2d, w_glu, b_glu, w_ba, w_bs, w_bc, tm, hi):
    m = a2d.shape[0]
    rows = lambda w, col=0: pl.BlockSpec((tm, w), lambda i: (i, col // w))
    const = lambda arr: pl.BlockSpec(arr.shape, lambda i: (0, 0), pipeline_mode=pl.Buffered(1))
    b2 = b_glu.reshape(1, -1)
    return pl.pallas_call(
        functools.partial(_branch_kernel, hi=hi),
        grid=(m // tm,),
        in_specs=[rows(ATT_W), rows(SSM_W), rows(CROSS_W), rows(SSM_W, COL_GS),
                  rows(D_MODEL, COL_GM), rows(D_MODEL, COL_GM + D_MODEL), rows(D_MODEL, COL_GM + 2 * D_MODEL),
                  const(w_glu), const(b2), const(w_ba), const(w_bs), const(w_bc)],
        out_specs=rows(D_MODEL),
        out_shape=jax.ShapeDtypeStruct((m, D_MODEL), F32),
        compiler_params=_params("parallel"),
        name="branch_merge",
    )(a2d, z2d, c2d, p2d, p2d, p2d, p2d, w_glu, b2, w_ba, w_bs, w_bc)


def _outproj_kernel(x_ref, m_ref, w_ref, o_ref, *, hi):
    o_ref[...] = x_ref[...] + _dot(m_ref[...], w_ref[...], hi)


def _outproj(x2d, merged, w_out, tm, hi):
    m, d = x2d.shape
    rows = pl.BlockSpec((tm, d), lambda i: (i, 0))
    return pl.pallas_call(
        functools.partial(_outproj_kernel, hi=hi),
        grid=(m // tm,),
        in_specs=[rows, rows, pl.BlockSpec((d, d), lambda i: (0, 0), pipeline_mode=pl.Buffered(1))],
        out_specs=rows,
        out_shape=jax.ShapeDtypeStruct((m, d), F32),
        compiler_params=_params("parallel"),
        name="outproj",
    )(x2d, merged, w_out)


def _mixer_layer(x3, attend, x0r, x0i, mk, mv, norm_g, w_in_t, layer, w_glu, b_glu, w_ba, w_bs, w_bc, w_out,
                 ssm, *, tm, n_scan, seg_len, chain, hi, tq):
    nb, s, d = x3.shape
    m = nb * s
    x2d = x3.reshape(m, d)
    tr = min(tm, 256)
    hh, hl = _rms_split(x2d, norm_g, tr)
    p2d = _inproj(hh, hl, w_in_t, layer, tm, 0 if hi else PROJ_LO_TILES)
    p3 = p2d.reshape(nb, s, PROJ_W)
    a_out = attend(p3)
    z, xr, xi = _s5(p2d.reshape(n_scan, m // n_scan, PROJ_W), x0r, x0i, *ssm,
                    seg_len=seg_len, chain=chain, hi=hi)
    c_out = _cross(p3, mk, mv, tq)
    merged = _branch_merge(a_out.reshape(m, ATT_W), z.reshape(m, SSM_W), c_out.reshape(m, CROSS_W),
                           p2d, w_glu, b_glu, w_ba, w_bs, w_bc, tr, hi)
    y = _outproj(x2d, merged, w_out, tr, hi).reshape(nb, s, d)
    k = p3[:, :, COL_K:COL_K + ATT_W].reshape(nb, s, ATT_HEADS, HEAD_DIM)
    v = p3[:, :, COL_V:COL_V + ATT_W].reshape(nb, s, ATT_HEADS, HEAD_DIM)
    ki = p3[:, :, COL_KI:COL_KI + IDX_DIM]
    return y, (k, v, ki, xr, xi)


def kernel(x_prompt, x_sample, cache_k, cache_v, cache_kidx, cache_mem_k, cache_mem_v, state_ssm_re, state_ssm_im, page_table, mem_prompt, norm_g, w_in, w_branch_attn, w_branch_ssm, w_branch_cross, w_out, w_mem_kv, ssm_a_re, ssm_a_im, ssm_b_re, ssm_b_im, ssm_c_re, ssm_c_im, ssm_d, ssm_log_dt, w_glu, b_glu, rel_bias, final_norm_g):
    depth = w_in.shape[0]
    b, s, d = x_prompt.shape
    db, t, _ = x_sample.shape
    mem_len = mem_prompt.shape[1]
    n_state = SSM_GROUPS * SSM_STATE
    assert s == SCAN_ROWS * (s // SCAN_ROWS) and db == SCAN_ROWS

    bias_tiles = _bias_tiles(rel_bias)
    w_in_t = jnp.transpose(w_in, (0, 2, 1))
    cache_k_t = jnp.transpose(cache_k, (0, 1, 3, 2, 4))
    cache_v_t = jnp.transpose(cache_v, (0, 1, 3, 2, 4))
    cache_kidx_t = jnp.transpose(cache_kidx, (0, 1, 3, 2))
    zeros_state = jnp.zeros((b, SCAN_ROWS, n_state), F32)
    yp, ys = x_prompt, x_sample
    outs = [[] for _ in range(12)]
    for l in range(depth):
        lb_re, lb_im, bb_re, bb_im = _ssm_params(ssm_a_re[l], ssm_a_im[l], ssm_log_dt[l],
                                                 ssm_b_re[l], ssm_b_im[l])
        wbu = jnp.concatenate([_block_diag_slabs(bb_re.transpose(1, 0, 2)),
                               _block_diag_slabs(bb_im.transpose(1, 0, 2))], axis=2)
        wyr = _block_diag_slabs(ssm_c_re[l].transpose(0, 2, 1))
        wyi = _block_diag_slabs(ssm_c_im[l].transpose(0, 2, 1))
        ssm = (lb_re.reshape(N_SLABS, 1, SLAB_STATES), lb_im.reshape(N_SLABS, 1, SLAB_STATES),
               wbu, wyr, wyi, ssm_d[l].reshape(N_SLABS, 1, LANES))

        mkv = _memkv(mem_prompt.reshape(b * mem_len, d), w_mem_kv[l])
        mk_p = mkv[:, :CROSS_W].reshape(b, mem_len, CROSS_W)
        mv_p = mkv[:, CROSS_W:].reshape(b, mem_len, CROSS_W)
        attend_p = functools.partial(_dsa_prompt, bias_tiles=bias_tiles, rel_bias=rel_bias)
        yp, (kp, vp, kip, xr, xi) = _mixer_layer(
            yp, attend_p, zeros_state, zeros_state, mk_p, mv_p, norm_g[l], w_in_t, l,
            w_glu[l].astype(BF16), b_glu[l], w_branch_attn[l].astype(BF16), w_branch_ssm[l].astype(BF16),
            w_branch_cross[l].astype(BF16), w_out[l].astype(BF16), ssm,
            tm=2048, n_scan=b, seg_len=s // SCAN_ROWS, chain=True, hi=False, tq=512)
        outs[0].append(kp); outs[1].append(vp); outs[2].append(kip)
        outs[3].append(mk_p.reshape(b, mem_len, CROSS_HEADS, CROSS_HD))
        outs[4].append(mv_p.reshape(b, mem_len, CROSS_HEADS, CROSS_HD))
        outs[5].append(xr[:, SCAN_ROWS - 1].reshape(b, SSM_GROUPS, SSM_STATE))
        outs[6].append(xi[:, SCAN_ROWS - 1].reshape(b, SSM_GROUPS, SSM_STATE))

        attend_s = functools.partial(_dsa_sample, l, cache_k_t=cache_k_t, cache_v_t=cache_v_t,
                                     cache_kidx_t=cache_kidx_t, page_table=page_table, rel_bias=rel_bias)
        ys, (kn, vn, kin, sr, si) = _mixer_layer(
            ys, attend_s,
            state_ssm_re[l].reshape(1, db, n_state), state_ssm_im[l].reshape(1, db, n_state),
            cache_mem_k[l].reshape(db, mem_len, CROSS_W), cache_mem_v[l].reshape(db, mem_len, CROSS_W),
            norm_g[l], w_in_t, l, w_glu[l], b_glu[l], w_branch_attn[l], w_branch_ssm[l],
            w_branch_cross[l], w_out[l], ssm,
            tm=db * t, n_scan=1, seg_len=t, chain=False, hi=True, tq=t)
        outs[7].append(kn); outs[8].append(vn); outs[9].append(kin)
        outs[10].append(sr.reshape(db, SSM_GROUPS, SSM_STATE)); outs[11].append(si.reshape(db, SSM_GROUPS, SSM_STATE))

    y_prompt = _rms(yp.reshape(b * s, d), final_norm_g, 256).reshape(b, s, d)
    y_sample = _rms(ys.reshape(db * t, d), final_norm_g, db * t).reshape(db, t, d)
    return (y_prompt, y_sample) + tuple(jnp.stack(o, axis=0) for o in outs)
```

```python
import functools
import math

import jax
import jax.numpy as jnp
import numpy as np
from jax import lax
from jax.experimental import pallas as pl
from jax.experimental.pallas import tpu as pltpu

F32 = jnp.float32
BF16 = jnp.bfloat16
I32 = jnp.int32
HIGHEST = lax.Precision.HIGHEST
NT_DIMS = (((1,), (1,)), ((), ()))

VMEM_LIMIT_BYTES = 56 * 1024 * 1024
LANES = 128
SUBLANES = 8

D_MODEL = 2048
HEAD_DIM = 128
ATT_HEADS = 6
ATT_W = ATT_HEADS * HEAD_DIM
IDX_HEADS = 16
IDX_DIM = 64
IDX_W = IDX_HEADS * IDX_DIM
TOPK_MAX = 256
QBLOCK = 128
SSM_GROUP = 16
SSM_W = 768
SSM_GROUPS = SSM_W // SSM_GROUP
SSM_STATE = 64
CROSS_HEADS = 4
CROSS_HD = 128
CROSS_W = CROSS_HEADS * CROSS_HD
REL_BUCKETS = 32
REL_MAX_DIST = 128
N_BRANCHES = 3
EPS = 1e-6
SPLIT_WIDTHS = (ATT_W, ATT_W, ATT_W, ATT_W, IDX_W, IDX_HEADS, IDX_DIM,
                SSM_W, SSM_W, CROSS_W, CROSS_W, N_BRANCHES * D_MODEL)
SPLIT_POINTS = tuple(int(c) for c in np.cumsum(SPLIT_WIDTHS)[:-1])

COL_GM = 0
COL_Q = 6144
COL_K = COL_Q + ATT_W
COL_V = COL_K + ATT_W
COL_GA = COL_V + ATT_W
COL_U = COL_GA + ATT_W
COL_GS = COL_U + SSM_W
COL_QC = COL_GS + SSM_W
COL_GC = COL_QC + CROSS_W
COL_KIWI = COL_GC + CROSS_W
COL_WI = COL_KIWI
COL_KI = COL_KIWI + IDX_HEADS
COL_QI = 12288
PROJ_W = COL_QI + IDX_W
PROJ_TN = 512
PROJ_ROW_ALIGN = 16
PROJ_LO_TILES = COL_KIWI // PROJ_TN


def _proj_tile_rows():
    ref_start = dict(zip(("q", "k", "v", "ga", "qi", "wi", "ki", "u", "gs", "qc", "gc", "gm"),
                         (0,) + SPLIT_POINTS))
    runs = ((COL_GM, "gm", N_BRANCHES * D_MODEL), (COL_Q, "q", 4 * ATT_W), (COL_U, "u", 2 * SSM_W),
            (COL_QC, "qc", 2 * CROSS_W), (COL_KIWI, "wi", PROJ_TN), (COL_QI, "qi", IDX_W))
    rows = np.zeros(PROJ_W // PROJ_TN, np.int32)
    for col, name, width in runs:
        for t in range(width // PROJ_TN):
            rows[col // PROJ_TN + t] = ref_start[name] + t * PROJ_TN
    assert not (rows % PROJ_ROW_ALIGN).any()
    return rows

SLAB_GROUPS = LANES // SSM_GROUP
SLAB_STATES = SLAB_GROUPS * SSM_STATE
N_SLABS = SSM_GROUPS // SLAB_GROUPS
SCAN_ROWS = SUBLANES

KEY_EXTENT_CLASSES = 8
COUNT_FOLD_ROWS = 64

PAGES_PER_STEP = 8
IDX_PAGES_PER_STEP = 32
PAGE_BUFFERS = 2

INT_MIN = -2 ** 31
LOWEST_F32_KEY = -0x7F800000


def _params(*sem):
    return pltpu.CompilerParams(dimension_semantics=sem, vmem_limit_bytes=VMEM_LIMIT_BYTES)


def _sigmoid(x):
    return 1.0 / (1.0 + jnp.exp(-x))


def _silu(x):
    return x * _sigmoid(x)


def _gelu_tanh(x):
    return 0.5 * x * (1.0 + jnp.tanh(math.sqrt(2.0 / math.pi) * (x + 0.044715 * (x * x * x))))


def _dot(a, b, hi):
    if hi:
        return jnp.dot(a.astype(F32), b.astype(F32), precision=HIGHEST, preferred_element_type=F32)
    return jnp.dot(a.astype(BF16), b.astype(BF16), preferred_element_type=F32)


def _split_bf16(x):
    hi = x.astype(BF16).astype(F32)
    return hi, x - hi


def _ordered_bits_to_float(key):
    return lax.bitcast_convert_type(key ^ ((key >> 31) & I32(0x7FFFFFFF)), F32)


def _kth_largest(score, k, axis=1):
    def count_ge(cand_bits):
        ones = (score >= _ordered_bits_to_float(cand_bits)).astype(I32)
        if axis == 0:
            while ones.shape[0] > COUNT_FOLD_ROWS and ones.shape[0] % (2 * SUBLANES) == 0:
                half = ones.shape[0] // 2
                ones = ones[:half] + ones[half:]
            return jnp.sum(ones.astype(F32), axis=0, keepdims=True).astype(I32)
        return jnp.sum(ones, axis=axis, keepdims=True)

    zero = jnp.zeros((score.shape[0], 1) if axis == 1 else (1, score.shape[1]), I32)
    t = jnp.where(count_ge(zero) >= k, zero, jnp.full_like(zero, INT_MIN))

    def body(it, t):
        cand = t | (I32(1) << (I32(30) - it))
        return jnp.where(count_ge(cand) >= k, cand, t)

    t = lax.fori_loop(0, 31, body, t)
    return _ordered_bits_to_float(jnp.maximum(t, I32(LOWEST_F32_KEY)))


def _t5_bucket(dist):
    n = jnp.maximum(dist, 0)
    max_exact = REL_BUCKETS // 2
    nf = jnp.maximum(n, 1).astype(F32)
    scale = (REL_BUCKETS - max_exact) / math.log(REL_MAX_DIST / max_exact)
    large = max_exact + jnp.floor(jnp.log(nf * (1.0 / max_exact)) * scale).astype(I32)
    large = jnp.minimum(large, REL_BUCKETS - 1)
    return jnp.where(n < max_exact, n, large)


def _rms_split_kernel(x_ref, g_ref, hh_ref, hl_ref):
    x = x_ref[...]
    y = x * lax.rsqrt(jnp.mean(x * x, axis=-1, keepdims=True) + EPS) * g_ref[...]
    hh = y.astype(BF16)
    hh_ref[...] = hh
    hl_ref[...] = (y - hh.astype(F32)).astype(BF16)


def _rms_split(x2d, g, tm):
    m, d = x2d.shape
    return pl.pallas_call(
        _rms_split_kernel,
        grid=(m // tm,),
        in_specs=[pl.BlockSpec((tm, d), lambda i: (i, 0)), pl.BlockSpec((1, d), lambda i: (0, 0))],
        out_specs=[pl.BlockSpec((tm, d), lambda i: (i, 0))] * 2,
        out_shape=[jax.ShapeDtypeStruct((m, d), BF16)] * 2,
        compiler_params=_params("parallel"),
        name="rms_split",
    )(x2d, g.reshape(1, d))


def _inproj_kernel(rows_ref, hh_ref, hl_ref, wt_ref, o_ref, *, lo_tiles):
    j = pl.program_id(1)
    w = wt_ref[...]
    w_hi = w.astype(BF16)

    @pl.when(j < lo_tiles)
    def _():
        o_ref[...] = lax.dot_general(hh_ref[...], w_hi, NT_DIMS, preferred_element_type=F32)

    @pl.when(j >= lo_tiles)
    def _():
        w_lo = (w - w_hi.astype(F32)).astype(BF16)
        hh = hh_ref[...]
        acc = lax.dot_general(hh, w_hi, NT_DIMS, preferred_element_type=F32)
        acc = acc + lax.dot_general(hl_ref[...], w_hi, NT_DIMS, preferred_element_type=F32)
        o_ref[...] = acc + lax.dot_general(hh, w_lo, NT_DIMS, preferred_element_type=F32)


def _inproj(hh, hl, w_in_t, layer, tm, lo_tiles):
    m, d = hh.shape
    n_tiles = PROJ_W // PROJ_TN
    resident = lambda: pl.BlockSpec((tm, d), lambda i, j, rows: (i, 0), pipeline_mode=pl.Buffered(1))
    return pl.pallas_call(
        functools.partial(_inproj_kernel, lo_tiles=lo_tiles),
        grid_spec=pltpu.PrefetchScalarGridSpec(
            num_scalar_prefetch=1,
            grid=(m // tm, n_tiles),
            in_specs=[resident(), resident(),
                      pl.BlockSpec((None, pl.Element(PROJ_TN), pl.Element(d)),
                                   lambda i, j, rows: (layer, rows[j] * PROJ_ROW_ALIGN, 0))],
            out_specs=pl.BlockSpec((tm, PROJ_TN), lambda i, j, rows: (i, j))),
        out_shape=jax.ShapeDtypeStruct((m, PROJ_W), F32),
        compiler_params=_params("parallel", "arbitrary"),
        name="inproj",
    )(jnp.asarray(_proj_tile_rows() // PROJ_ROW_ALIGN), hh, hl, w_in_t)


def _memkv_kernel(a_ref, w_ref, o_ref):
    o_ref[...] = _dot(a_ref[...], w_ref[...], False)


def _memkv(mem2d, w):
    m, d = mem2d.shape
    n = w.shape[1]
    tn = 512
    return pl.pallas_call(
        _memkv_kernel,
        grid=(n // tn,),
        in_specs=[pl.BlockSpec((m, d), lambda j: (0, 0)), pl.BlockSpec((d, tn), lambda j: (0, j))],
        out_specs=pl.BlockSpec((m, tn), lambda j: (0, j)),
        out_shape=jax.ShapeDtypeStruct((m, n), F32),
        compiler_params=_params("parallel"),
        name="memkv",
    )(mem2d, w)


def _bias_tile_kernel(rb_ref, o_ref):
    c = lax.broadcasted_iota(I32, (2 * QBLOCK, QBLOCK), 0)
    r = lax.broadcasted_iota(I32, (2 * QBLOCK, QBLOCK), 1)
    bucket = _t5_bucket(r - c + QBLOCK)
    for h in range(ATT_HEADS):
        tile = jnp.zeros((2 * QBLOCK, QBLOCK), F32)
        for b in range(REL_BUCKETS):
            tile = jnp.where(bucket == b, rb_ref[b, h], tile)
        o_ref[h] = tile


def _bias_tiles(rel_bias):
    return pl.pallas_call(
        _bias_tile_kernel,
        in_specs=[pl.BlockSpec(memory_space=pltpu.SMEM)],
        out_specs=pl.BlockSpec(memory_space=pltpu.VMEM),
        out_shape=jax.ShapeDtypeStruct((ATT_HEADS, 2 * QBLOCK, QBLOCK), F32),
        name="bias_tiles",
    )(rel_bias)


def _dsa_prompt_kernel(rb_ref, q_ref, ga_ref, qi_ref, kwq_ref, k_ref, v_ref, kwk_ref, bt_ref, o_ref,
                       ki4_ref, kb_ref, vt_ref, *, seq, topk):
    i = pl.program_id(1)

    @pl.when(i == 0)
    def _():
        kh, kl = _split_bf16(kwk_ref[:, IDX_HEADS:IDX_HEADS + IDX_DIM])
        ki4_ref[...] = jnp.concatenate([kh, kl, kh, kl], axis=1).astype(BF16)
        kb_ref[...] = k_ref[...].astype(BF16)
        for kb in range(seq // QBLOCK):
            for h in range(ATT_HEADS):
                rows = slice(kb * QBLOCK, (kb + 1) * QBLOCK)
                cols = slice(h * HEAD_DIM, (h + 1) * HEAD_DIM)
                vt_ref[cols, rows] = v_ref[rows, cols].T.astype(BF16)

    n_blk = seq // QBLOCK
    per = n_blk // KEY_EXTENT_CLASSES
    for cls in range(KEY_EXTENT_CLASSES):
        pl.when(i // per == cls)(functools.partial(
            _dsa_prompt_block, i, (cls + 1) * per * QBLOCK, per, topk,
            rb_ref, q_ref, ga_ref, qi_ref, kwq_ref, bt_ref, o_ref, ki4_ref, kb_ref, vt_ref))


def _dsa_prompt_block(i, nk, per, topk, rb_ref, q_ref, ga_ref, qi_ref, kwq_ref, bt_ref, o_ref,
                      ki4_ref, kb_ref, vt_ref):
    qi = qi_ref[...]
    wi_t = kwq_ref[...].T * (IDX_HEADS ** -0.5)
    score = jnp.zeros((nk, QBLOCK), F32)
    for h0 in range(0, IDX_HEADS, 2):
        q4 = []
        for h in (h0, h0 + 1):
            qh, ql = _split_bf16(qi[:, h * IDX_DIM:(h + 1) * IDX_DIM])
            q4.append(jnp.concatenate([qh, qh, ql, ql], axis=1).astype(BF16))
        s = lax.dot_general(ki4_ref[0:nk, :], jnp.concatenate(q4, axis=0), NT_DIMS,
                            preferred_element_type=F32)
        score = score + jnp.maximum(s[:, 0:QBLOCK], 0.0) * wi_t[h0:h0 + 1, :]
        score = score + jnp.maximum(s[:, QBLOCK:2 * QBLOCK], 0.0) * wi_t[h0 + 1:h0 + 2, :]

    key_pos = lax.broadcasted_iota(I32, (nk, QBLOCK), 0)
    q_pos = i * QBLOCK + lax.broadcasted_iota(I32, (nk, QBLOCK), 1)
    score = jnp.where(key_pos <= q_pos, score, -jnp.inf)
    mask_add = jnp.where(score >= _kth_largest(score, topk, axis=0), 0.0, -jnp.inf)

    near = min(nk, (per + 1) * QBLOCK)
    key_blk = key_pos[nk - near:, :] // QBLOCK
    is_diag = key_blk == i
    is_prev = key_blk == i - 1
    reps = near // QBLOCK
    q = q_ref[...] * (HEAD_DIM ** -0.5)
    ga = ga_ref[...]
    for h in range(ATT_HEADS):
        cols = slice(h * HEAD_DIM, (h + 1) * HEAD_DIM)
        far_bias = rb_ref[REL_BUCKETS - 1, h]
        t_prev = jnp.concatenate([bt_ref[h, 0:QBLOCK, :]] * reps, axis=0)
        t_diag = jnp.concatenate([bt_ref[h, QBLOCK:2 * QBLOCK, :]] * reps, axis=0)
        bias = jnp.where(is_diag, t_diag, jnp.where(is_prev, t_prev, far_bias))
        if nk > near:
            bias = jnp.concatenate([jnp.full((nk - near, QBLOCK), far_bias, F32), bias], axis=0)
        logits = lax.dot_general(kb_ref[0:nk, cols], q[:, cols].astype(BF16), NT_DIMS,
                                 preferred_element_type=F32)
        logits = logits + bias + mask_add
        m = jnp.max(logits, axis=0, keepdims=True)
        p = jnp.exp(logits - m)
        l = jnp.sum(p, axis=0, keepdims=True)
        o_t = jnp.dot(vt_ref[cols, 0:nk], p.astype(BF16), preferred_element_type=F32)
        o_ref[:, cols] = (o_t / l).T * _silu(ga[:, cols])


def _dsa_prompt(p3, bias_tiles, rel_bias):
    b, s, _ = p3.shape
    topk = min(TOPK_MAX, s // 4)
    row = lambda col, w: pl.BlockSpec((None, QBLOCK, w), lambda bi, i: (bi, i, col // w))
    full = lambda col, w: pl.BlockSpec((None, s, w), lambda bi, i: (bi, 0, col // w))
    return pl.pallas_call(
        functools.partial(_dsa_prompt_kernel, seq=s, topk=topk),
        grid=(b, s // QBLOCK),
        in_specs=[pl.BlockSpec(memory_space=pltpu.SMEM),
                  row(COL_Q, ATT_W), row(COL_GA, ATT_W), row(COL_QI, IDX_W), row(COL_KIWI, LANES),
                  full(COL_K, ATT_W), full(COL_V, ATT_W), full(COL_KIWI, LANES),
                  pl.BlockSpec((ATT_HEADS, 2 * QBLOCK, QBLOCK), lambda bi, i: (0, 0, 0))],
        out_specs=pl.BlockSpec((None, QBLOCK, ATT_W), lambda bi, i: (bi, i, 0)),
        out_shape=jax.ShapeDtypeStruct((b, s, ATT_W), F32),
        scratch_shapes=[pltpu.VMEM((s, 4 * IDX_DIM), BF16),
                        pltpu.VMEM((s, ATT_W), BF16),
                        pltpu.VMEM((ATT_W, s), BF16)],
        compiler_params=_params("parallel", "arbitrary"),
        name="dsa_prompt",
    )(rel_bias, p3, p3, p3, p3, p3, p3, p3, bias_tiles)


def _sidx_kernel(pt_ref, q_ref, w_ref, *rest, nch):
    k_refs = rest[:IDX_PAGES_PER_STEP]
    kn_ref, s_ref, q4_ref = rest[IDX_PAGES_PER_STEP:]
    c = pl.program_id(1)

    @pl.when(c == 0)
    def _():
        qh, ql = _split_bf16(q_ref[...])
        q4_ref[...] = jnp.concatenate([qh, qh, ql, ql], axis=1).astype(BF16)

    def scores(ki_t):
        n = ki_t.shape[1]
        kh, kl = _split_bf16(ki_t)
        ki4_t = jnp.concatenate([kh, kl, kh, kl], axis=0).astype(BF16)
        s = jnp.dot(q4_ref[...], ki4_t, preferred_element_type=F32)
        s = jnp.maximum(s, 0.0) * (w_ref[...] * (IDX_HEADS ** -0.5))
        return jnp.sum(s.reshape(IDX_HEADS, SUBLANES, n), axis=0)

    @pl.when(c < nch)
    def _():
        s_ref[...] = scores(jnp.concatenate([r[...] for r in k_refs], axis=1))

    @pl.when(c == nch)
    def _():
        n = kn_ref.shape[1]
        sn = scores(kn_ref[...])
        t_key = lax.broadcasted_iota(I32, (SUBLANES, n), 1)
        t_q = lax.broadcasted_iota(I32, (SUBLANES, n), 0)
        s_ref[...] = jnp.full(s_ref.shape, -jnp.inf, F32)
        s_ref[:, 0:n] = jnp.where(t_key <= t_q, sn, -jnp.inf)


def _sattn_kernel(pt_ref, rb_ref, sf_ref, sc_ref, q_ref, ga_ref, *rest, nch, past, topk, page):
    k_refs = rest[:PAGES_PER_STEP]
    v_refs = rest[PAGES_PER_STEP:2 * PAGES_PER_STEP]
    kn_ref, vn_ref, o_ref, thr_ref, m_ref, l_ref, acc_ref = rest[2 * PAGES_PER_STEP:]
    c = pl.program_id(1)
    chunk = PAGES_PER_STEP * page

    @pl.when(c == 0)
    def _():
        thr_ref[...] = _kth_largest(sf_ref[...], topk)
        m_ref[...] = jnp.full(m_ref.shape, -jnp.inf, F32)
        l_ref[...] = jnp.zeros(l_ref.shape, F32)
        acc_ref[...] = jnp.zeros(acc_ref.shape, F32)

    def attend(k_of, v_of, sc, key_pos0):
        n = sc.shape[1]
        mask_add = jnp.where(sc >= thr_ref[...], 0.0, -jnp.inf)
        q_pos = past + lax.broadcasted_iota(I32, (SUBLANES, n), 0)
        key_pos = key_pos0 + lax.broadcasted_iota(I32, (SUBLANES, n), 1)
        bucket = _t5_bucket(q_pos - key_pos)
        q = q_ref[...]
        rows = []
        for h in range(ATT_HEADS):
            cols = slice(h * HEAD_DIM, (h + 1) * HEAD_DIM)
            bias = jnp.zeros((SUBLANES, n), F32)
            for b in range(REL_BUCKETS):
                bias = jnp.where(bucket == b, rb_ref[b, h], bias)
            lg = lax.dot_general(q[:, cols].astype(BF16), k_of(h).astype(BF16), NT_DIMS,
                                 preferred_element_type=F32)
            rows.append(lg * (HEAD_DIM ** -0.5) + bias + mask_add)
        logits = jnp.concatenate(rows, axis=0)
        m_old = m_ref[...]
        m_new = jnp.maximum(m_old, jnp.max(logits, axis=1, keepdims=True))
        m_safe = jnp.where(m_new == -jnp.inf, 0.0, m_new)
        alpha = jnp.exp(m_old - m_safe)
        p = jnp.exp(logits - m_safe[:, 0:1])
        l_ref[...] = alpha * l_ref[...] + jnp.sum(p, axis=1, keepdims=True)
        m_ref[...] = m_new
        for h in range(ATT_HEADS):
            cols = slice(h * HEAD_DIM, (h + 1) * HEAD_DIM)
            rs = slice(h * SUBLANES, (h + 1) * SUBLANES)
            acc_ref[:, cols] = alpha[rs] * acc_ref[:, cols] + jnp.dot(
                p[rs].astype(BF16), v_of(h).astype(BF16), preferred_element_type=F32)

    @pl.when(c < nch)
    def _():
        k_of = lambda h: jnp.concatenate([r[h] for r in k_refs], axis=0)
        v_of = lambda h: jnp.concatenate([r[h] for r in v_refs], axis=0)
        attend(k_of, v_of, sc_ref[...], c * chunk)

    @pl.when(c == nch)
    def _():
        n = kn_ref.shape[0]
        head = lambda ref: (lambda h: ref[:, h * HEAD_DIM:(h + 1) * HEAD_DIM])
        attend(head(kn_ref), head(vn_ref), sc_ref[:, 0:n], past)
        ga = ga_ref[...]
        l = l_ref[...]
        for h in range(ATT_HEADS):
            cols = slice(h * HEAD_DIM, (h + 1) * HEAD_DIM)
            o_ref[:, cols] = acc_ref[:, cols] / l[h * SUBLANES:(h + 1) * SUBLANES] * _silu(ga[:, cols])


def _dsa_sample(layer, ps3, cache_k_t, cache_v_t, cache_kidx_t, page_table, rel_bias):
    db, t, _ = ps3.shape
    n_pages = page_table.shape[1]
    page = cache_kidx_t.shape[3]
    past = n_pages * page
    topk = min(TOPK_MAX, (past + t) // 4)
    nch = n_pages // PAGES_PER_STEP
    chunk = PAGES_PER_STEP * page
    nch_idx = n_pages // IDX_PAGES_PER_STEP
    chunk_idx = IDX_PAGES_PER_STEP * page
    ntot = (nch_idx + 1) * chunk_idx
    assert chunk_idx % chunk == 0 and t == SUBLANES

    qi = ps3[:, :, COL_QI:COL_QI + IDX_W].reshape(db, t, IDX_HEADS, IDX_DIM)
    qi_hq = qi.transpose(0, 2, 1, 3).reshape(db, IDX_HEADS * t, IDX_DIM)
    wi_hq = ps3[:, :, COL_WI:COL_WI + IDX_HEADS].transpose(0, 2, 1).reshape(db, IDX_HEADS * t, 1)
    pad_rows = lambda a: jnp.pad(a, ((0, 0), (0, page - t), (0, 0)))
    ki_new_t = pad_rows(ps3[:, :, COL_KI:COL_KI + IDX_DIM]).transpose(0, 2, 1)
    k_new = pad_rows(ps3[:, :, COL_K:COL_K + ATT_W])
    v_new = pad_rows(ps3[:, :, COL_V:COL_V + ATT_W])
    q = ps3[:, :, COL_Q:COL_Q + ATT_W]
    ga = ps3[:, :, COL_GA:COL_GA + ATT_W]

    def page_spec(r, per_step, dims, **kw):
        def imap(b, c, pt):
            return (layer, pt[b, jnp.minimum(c * per_step + r, n_pages - 1)]) + (0,) * len(dims)
        return pl.BlockSpec((None, None) + dims, imap, **kw)

    per_b = lambda rows, w: pl.BlockSpec((None, rows, w), lambda b, c, pt: (b, 0, 0))

    scores = pl.pallas_call(
        functools.partial(_sidx_kernel, nch=nch_idx),
        grid_spec=pltpu.PrefetchScalarGridSpec(
            num_scalar_prefetch=1,
            grid=(db, nch_idx + 1),
            in_specs=[per_b(IDX_HEADS * t, IDX_DIM), per_b(IDX_HEADS * t, 1)]
                     + [page_spec(r, IDX_PAGES_PER_STEP, (IDX_DIM, page)) for r in range(IDX_PAGES_PER_STEP)]
                     + [per_b(IDX_DIM, page)],
            out_specs=pl.BlockSpec((None, t, chunk_idx), lambda b, c, pt: (b, 0, c)),
            scratch_shapes=[pltpu.VMEM((IDX_HEADS * t, 4 * IDX_DIM), BF16)]),
        out_shape=jax.ShapeDtypeStruct((db, t, ntot), F32),
        compiler_params=_params("parallel", "arbitrary"),
        name="dsa_sample_scores",
    )(page_table, qi_hq, wi_hq, *([cache_kidx_t] * IDX_PAGES_PER_STEP), ki_new_t)

    return pl.pallas_call(
        functools.partial(_sattn_kernel, nch=nch, past=past, topk=topk, page=page),
        grid_spec=pltpu.PrefetchScalarGridSpec(
            num_scalar_prefetch=1,
            grid=(db, nch + 1),
            in_specs=[pl.BlockSpec(memory_space=pltpu.SMEM),
                      per_b(t, ntot),
                      pl.BlockSpec((None, t, chunk), lambda b, c, pt: (b, 0, c)),
                      per_b(t, ATT_W), per_b(t, ATT_W)]
                     + [page_spec(r, PAGES_PER_STEP, (ATT_HEADS, page, HEAD_DIM),
                                  pipeline_mode=pl.Buffered(PAGE_BUFFERS)) for r in range(PAGES_PER_STEP)] * 2
                     + [per_b(page, ATT_W), per_b(page, ATT_W)],
            out_specs=per_b(t, ATT_W),
            scratch_shapes=[pltpu.VMEM((t, 1), F32),
                            pltpu.VMEM((ATT_HEADS * t, LANES), F32),
                            pltpu.VMEM((ATT_HEADS * t, LANES), F32),
                            pltpu.VMEM((t, ATT_W), F32)]),
        out_shape=jax.ShapeDtypeStruct((db, t, ATT_W), F32),
        compiler_params=_params("parallel", "arbitrary"),
        name="dsa_sample_attn",
    )(page_table, rel_bias, scores, scores, q, ga,
      *([cache_k_t] * PAGES_PER_STEP), *([cache_v_t] * PAGES_PER_STEP), k_new, v_new)


def _ssm_param_kernel(are_ref, aim_ref, ldt_ref, bre_ref, bim_ref, lbr_ref, lbi_ref, bbr_ref, bbi_ref):
    lam_re = jnp.minimum(are_ref[...], -1e-4)
    lam_im = aim_ref[...]
    step = jnp.exp(ldt_ref[...])
    mag = jnp.exp(lam_re * step)
    ang = lam_im * step
    lb_re = mag * jnp.cos(ang)
    lb_im = mag * jnp.sin(ang)
    nr = lb_re - 1.0
    ni = lb_im
    den = lam_re * lam_re + lam_im * lam_im
    coef_re = (nr * lam_re + ni * lam_im) / den
    coef_im = (ni * lam_re - nr * lam_im) / den
    lbr_ref[...] = lb_re
    lbi_ref[...] = lb_im
    for c in range(SSM_GROUP):
        br = bre_ref[c]
        bi = bim_ref[c]
        bbr_ref[c] = coef_re * br - coef_im * bi
        bbi_ref[c] = coef_re * bi + coef_im * br


def _ssm_params(a_re, a_im, log_dt, b_re, b_im):
    g, p, c = b_re.shape
    vm = pl.BlockSpec(memory_space=pltpu.VMEM)
    return pl.pallas_call(
        _ssm_param_kernel,
        in_specs=[vm] * 5,
        out_specs=[vm] * 4,
        out_shape=[jax.ShapeDtypeStruct((g, p), F32)] * 2 + [jax.ShapeDtypeStruct((c, g, p), F32)] * 2,
        name="ssm_params",
    )(a_re, a_im, log_dt.reshape(g, 1), b_re.transpose(2, 0, 1), b_im.transpose(2, 0, 1))


def _block_diag_slabs(m):
    g, a, b = m.shape
    eye = jnp.eye(SLAB_GROUPS, dtype=m.dtype)
    m4 = m.reshape(N_SLABS, SLAB_GROUPS, a, b)
    return jnp.einsum('sgab,gh->sgahb', m4, eye).reshape(N_SLABS, SLAB_GROUPS * a, SLAB_GROUPS * b)


def _scan_pitch(seg_len):
    return seg_len + 4 if seg_len % 8 == 0 else seg_len


def _s5_kernel(u_ref, x0r_ref, x0i_ref, lbr_ref, lbi_ref, wbu_ref, wyr_ref, wyi_ref, d_ref,
               z_ref, xr_ref, xi_ref, up_ref, zs_ref, *, seg_len, chain, hi):
    nc = SLAB_STATES // LANES
    pitch = _scan_pitch(seg_len)
    up_ref[...] = jnp.zeros(up_ref.shape, F32)
    for s in range(SCAN_ROWS):
        up_ref[pl.ds(s * pitch, seg_len), :] = u_ref[pl.ds(s * seg_len, seg_len), :]
    u = up_ref[...]
    bu = _dot(u, wbu_ref[...], hi)
    for j in range(2 * nc):
        zs_ref[j] = bu[:, j * LANES:(j + 1) * LANES]
    shape = (SCAN_ROWS, SLAB_STATES)
    lbr = jnp.broadcast_to(lbr_ref[...], shape)
    lbi = jnp.broadcast_to(lbi_ref[...], shape)

    def load(t, base):
        rows = pl.ds(t, SCAN_ROWS, stride=pitch)
        return jnp.concatenate([zs_ref[base + j, rows, :] for j in range(nc)], axis=1)

    def store(t, base, val):
        rows = pl.ds(t, SCAN_ROWS, stride=pitch)
        for j in range(nc):
            zs_ref[base + j, rows, :] = val[:, j * LANES:(j + 1) * LANES]

    def advance(t, carry):
        xr, xi = carry
        nr = lbr * xr - lbi * xi + load(t, 0)
        ni = lbr * xi + lbi * xr + load(t, nc)
        return nr, ni

    def advance_store(t, carry):
        nr, ni = advance(t, carry)
        store(t, 0, nr)
        store(t, nc, ni)
        return nr, ni

    unroll = min(seg_len, 8)
    x0r = x0r_ref[...]
    x0i = x0i_ref[...]
    if chain:
        zero = jnp.zeros(shape, F32)
        er, ei = lax.fori_loop(0, seg_len, advance, (zero, zero), unroll=unroll)
        pr, pi = lbr, lbi
        for _ in range(int(math.log2(seg_len))):
            pr, pi = pr * pr - pi * pi, 2.0 * pr * pi
        row = lax.broadcasted_iota(I32, shape, 0)
        sr = jnp.where(row == 0, x0r, 0.0)
        si = jnp.where(row == 0, x0i, 0.0)
        for s in range(SCAN_ROWS - 1):
            nr = er + pr * sr - pi * si
            ni = ei + pr * si + pi * sr
            sr = jnp.where(row == s + 1, pltpu.roll(nr, 1, axis=0), sr)
            si = jnp.where(row == s + 1, pltpu.roll(ni, 1, axis=0), si)
        x0r, x0i = sr, si
    xr, xi = lax.fori_loop(0, seg_len, advance_store, (x0r, x0i), unroll=unroll)
    xr_ref[...] = xr
    xi_ref[...] = xi
    y = d_ref[...] * u
    for j in range(nc):
        blk = slice(j * LANES, (j + 1) * LANES)
        y = y + _dot(zs_ref[j], wyr_ref[blk, :], hi) - _dot(zs_ref[nc + j], wyi_ref[blk, :], hi)
    up_ref[...] = _gelu_tanh(y)
    for s in range(SCAN_ROWS):
        z_ref[pl.ds(s * seg_len, seg_len), :] = up_ref[pl.ds(s * pitch, seg_len), :]


def _s5(p3, x0r, x0i, lbr, lbi, wbu, wyr, wyi, d, *, seg_len, chain, hi):
    nb, r, _ = p3.shape
    assert r == SCAN_ROWS * seg_len and (not chain or seg_len & (seg_len - 1) == 0)
    rp = SCAN_ROWS * _scan_pitch(seg_len)
    slab = lambda rows, w: pl.BlockSpec((None, rows, w), lambda b, s: (s, 0, 0))
    st = pl.BlockSpec((None, SCAN_ROWS, SLAB_STATES), lambda b, s: (b, 0, s))
    return pl.pallas_call(
        functools.partial(_s5_kernel, seg_len=seg_len, chain=chain, hi=hi),
        grid=(nb, N_SLABS),
        in_specs=[pl.BlockSpec((None, r, LANES), lambda b, s: (b, 0, COL_U // LANES + s)),
                  st, st, slab(1, SLAB_STATES), slab(1, SLAB_STATES),
                  slab(LANES, 2 * SLAB_STATES), slab(SLAB_STATES, LANES), slab(SLAB_STATES, LANES),
                  slab(1, LANES)],
        out_specs=[pl.BlockSpec((None, r, LANES), lambda b, s: (b, 0, s)), st, st],
        out_shape=[jax.ShapeDtypeStruct((nb, r, SSM_W), F32),
                   jax.ShapeDtypeStruct((nb, SCAN_ROWS, SSM_GROUPS * SSM_STATE), F32),
                   jax.ShapeDtypeStruct((nb, SCAN_ROWS, SSM_GROUPS * SSM_STATE), F32)],
        scratch_shapes=[pltpu.VMEM((rp, LANES), F32),
                        pltpu.VMEM((2 * SLAB_STATES // LANES, rp, LANES), F32)],
        compiler_params=_params("parallel", "parallel"),
        name="s5_scan",
    )(p3, x0r, x0i, lbr, lbi, wbu, wyr, wyi, d)


def _cross_kernel(q_ref, g_ref, mk_ref, mv_ref, o_ref):
    q = q_ref[...]
    g = g_ref[...]
    for h in range(CROSS_HEADS):
        cols = slice(h * CROSS_HD, (h + 1) * CROSS_HD)
        logits = lax.dot_general(q[:, cols].astype(BF16), mk_ref[:, cols].astype(BF16), NT_DIMS,
                                 preferred_element_type=F32) * (CROSS_HD ** -0.5)
        m = jnp.max(logits, axis=1, keepdims=True)
        p = jnp.exp(logits - m)
        l = jnp.sum(p, axis=1, keepdims=True)
        o = jnp.dot(p.astype(BF16), mv_ref[:, cols].astype(BF16), preferred_element_type=F32)
        o_ref[:, cols] = o / l * _silu(g[:, cols])


def _cross(p3, mk, mv, tq):
    nb, s, _ = p3.shape
    mem = mk.shape[1]
    qspec = lambda col: pl.BlockSpec((None, tq, CROSS_W), lambda b, i: (b, i, col // CROSS_W))
    mspec = pl.BlockSpec((None, mem, CROSS_W), lambda b, i: (b, 0, 0))
    return pl.pallas_call(
        _cross_kernel,
        grid=(nb, s // tq),
        in_specs=[qspec(COL_QC), qspec(COL_GC), mspec, mspec],
        out_specs=pl.BlockSpec((None, tq, CROSS_W), lambda b, i: (b, i, 0)),
        out_shape=jax.ShapeDtypeStruct((nb, s, CROSS_W), F32),
        compiler_params=_params("parallel", "parallel"),
        name="cross_attn",
    )(p3, p3, mk, mv)


def _merge_out_kernel(x_ref, a_ref, z_ref, c_ref, gs_ref, gm0_ref, gm1_ref, gm2_ref,
                      wglu_ref, bglu_ref, wba_ref, wbs_ref, wbc_ref, wout_ref, gfin_ref, o_ref, *, hi, final):
    gl = _dot(z_ref[...], wglu_ref[...], hi) + bglu_ref[...]
    s_out = gl[:, 0:SSM_W] * _sigmoid(gl[:, SSM_W:2 * SSM_W]) * _silu(gs_ref[...])
    merged = _sigmoid(gm0_ref[...]) * _dot(a_ref[...], wba_ref[...], hi)
    merged = merged + _sigmoid(gm1_ref[...]) * _dot(s_out, wbs_ref[...], hi)
    merged = merged + _sigmoid(gm2_ref[...]) * _dot(c_ref[...], wbc_ref[...], hi)
    y = x_ref[...] + _dot(merged, wout_ref[...], hi)
    if final:
        y = y * lax.rsqrt(jnp.mean(y * y, axis=-1, keepdims=True) + EPS) * gfin_ref[...]
    o_ref[...] = y


def _merge_out(x2d, a2d, z2d, c2d, p2d, w_glu, b_glu, w_ba, w_bs, w_bc, w_out, g_final, tm, hi, final):
    m, d = x2d.shape
    rows = lambda w, col=0: pl.BlockSpec((tm, w), lambda i: (i, col // w))
    const = lambda arr: pl.BlockSpec(arr.shape, lambda i: (0, 0), pipeline_mode=pl.Buffered(1))
    b2 = b_glu.reshape(1, -1)
    g2 = g_final.reshape(1, d)
    return pl.pallas_call(
        functools.partial(_merge_out_kernel, hi=hi, final=final),
        grid=(m // tm,),
        in_specs=[rows(d), rows(ATT_W), rows(SSM_W), rows(CROSS_W), rows(SSM_W, COL_GS),
                  rows(D_MODEL, COL_GM), rows(D_MODEL, COL_GM + D_MODEL), rows(D_MODEL, COL_GM + 2 * D_MODEL),
                  const(w_glu), const(b2), const(w_ba), const(w_bs), const(w_bc), const(w_out), const(g2)],
        out_specs=rows(d),
        out_shape=jax.ShapeDtypeStruct((m, d), F32),
        compiler_params=_params("parallel"),
        name="merge_out",
    )(x2d, a2d, z2d, c2d, p2d, p2d, p2d, p2d, w_glu, b2, w_ba, w_bs, w_bc, w_out, g2)


def _mixer_layer(x3, attend, x0r, x0i, mk, mv, norm_g, w_in_t, layer, w_glu, b_glu, w_ba, w_bs, w_bc, w_out,
                 ssm, g_final, *, tm, n_scan, seg_len, chain, hi, tq, final):
    nb, s, d = x3.shape
    m = nb * s
    x2d = x3.reshape(m, d)
    tr = min(tm, 256)
    hh, hl = _rms_split(x2d, norm_g, tr)
    p2d = _inproj(hh, hl, w_in_t, layer, tm, 0 if hi else PROJ_LO_TILES)
    p3 = p2d.reshape(nb, s, PROJ_W)
    a_out = attend(p3)
    z, xr, xi = _s5(p2d.reshape(n_scan, m // n_scan, PROJ_W), x0r, x0i, *ssm,
                    seg_len=seg_len, chain=chain, hi=hi)
    c_out = _cross(p3, mk, mv, tq)
    y = _merge_out(x2d, a_out.reshape(m, ATT_W), z.reshape(m, SSM_W), c_out.reshape(m, CROSS_W), p2d,
                   w_glu, b_glu, w_ba, w_bs, w_bc, w_out, g_final, tr, hi, final).reshape(nb, s, d)
    k = p3[:, :, COL_K:COL_K + ATT_W].reshape(nb, s, ATT_HEADS, HEAD_DIM)
    v = p3[:, :, COL_V:COL_V + ATT_W].reshape(nb, s, ATT_HEADS, HEAD_DIM)
    ki = p3[:, :, COL_KI:COL_KI + IDX_DIM]
    return y, (k, v, ki, xr, xi)


def kernel(x_prompt, x_sample, cache_k, cache_v, cache_kidx, cache_mem_k, cache_mem_v, state_ssm_re, state_ssm_im, page_table, mem_prompt, norm_g, w_in, w_branch_attn, w_branch_ssm, w_branch_cross, w_out, w_mem_kv, ssm_a_re, ssm_a_im, ssm_b_re, ssm_b_im, ssm_c_re, ssm_c_im, ssm_d, ssm_log_dt, w_glu, b_glu, rel_bias, final_norm_g):
    depth = w_in.shape[0]
    b, s, d = x_prompt.shape
    db, t, _ = x_sample.shape
    mem_len = mem_prompt.shape[1]
    n_state = SSM_GROUPS * SSM_STATE
    assert s == SCAN_ROWS * (s // SCAN_ROWS) and db == SCAN_ROWS

    bias_tiles = _bias_tiles(rel_bias)
    w_in_t = jnp.transpose(w_in, (0, 2, 1))
    cache_k_t = jnp.transpose(cache_k, (0, 1, 3, 2, 4))
    cache_v_t = jnp.transpose(cache_v, (0, 1, 3, 2, 4))
    cache_kidx_t = jnp.transpose(cache_kidx, (0, 1, 3, 2))
    zeros_state = jnp.zeros((b, SCAN_ROWS, n_state), F32)
    yp, ys = x_prompt, x_sample
    outs = [[] for _ in range(12)]
    for l in range(depth):
        lb_re, lb_im, bb_re, bb_im = _ssm_params(ssm_a_re[l], ssm_a_im[l], ssm_log_dt[l],
                                                 ssm_b_re[l], ssm_b_im[l])
        wbu = jnp.concatenate([_block_diag_slabs(bb_re.transpose(1, 0, 2)),
                               _block_diag_slabs(bb_im.transpose(1, 0, 2))], axis=2)
        wyr = _block_diag_slabs(ssm_c_re[l].transpose(0, 2, 1))
        wyi = _block_diag_slabs(ssm_c_im[l].transpose(0, 2, 1))
        ssm = (lb_re.reshape(N_SLABS, 1, SLAB_STATES), lb_im.reshape(N_SLABS, 1, SLAB_STATES),
               wbu, wyr, wyi, ssm_d[l].reshape(N_SLABS, 1, LANES))

        mkv = _memkv(mem_prompt.reshape(b * mem_len, d), w_mem_kv[l])
        mk_p = mkv[:, :CROSS_W].reshape(b, mem_len, CROSS_W)
        mv_p = mkv[:, CROSS_W:].reshape(b, mem_len, CROSS_W)
        attend_p = functools.partial(_dsa_prompt, bias_tiles=bias_tiles, rel_bias=rel_bias)
        yp, (kp, vp, kip, xr, xi) = _mixer_layer(
            yp, attend_p, zeros_state, zeros_state, mk_p, mv_p, norm_g[l], w_in_t, l,
            w_glu[l].astype(BF16), b_glu[l], w_branch_attn[l].astype(BF16), w_branch_ssm[l].astype(BF16),
            w_branch_cross[l].astype(BF16), w_out[l].astype(BF16), ssm, final_norm_g,
            tm=2048, n_scan=b, seg_len=s // SCAN_ROWS, chain=True, hi=False, tq=512, final=l == depth - 1)
        outs[0].append(kp); outs[1].append(vp); outs[2].append(kip)
        outs[3].append(mk_p.reshape(b, mem_len, CROSS_HEADS, CROSS_HD))
        outs[4].append(mv_p.reshape(b, mem_len, CROSS_HEADS, CROSS_HD))
        outs[5].append(xr[:, SCAN_ROWS - 1].reshape(b, SSM_GROUPS, SSM_STATE))
        outs[6].append(xi[:, SCAN_ROWS - 1].reshape(b, SSM_GROUPS, SSM_STATE))

        attend_s = functools.partial(_dsa_sample, l, cache_k_t=cache_k_t, cache_v_t=cache_v_t,
                                     cache_kidx_t=cache_kidx_t, page_table=page_table, rel_bias=rel_bias)
        ys, (kn, vn, kin, sr, si) = _mixer_layer(
            ys, attend_s,
            state_ssm_re[l].reshape(1, db, n_state), state_ssm_im[l].reshape(1, db, n_state),
            cache_mem_k[l].reshape(db, mem_len, CROSS_W), cache_mem_v[l].reshape(db, mem_len, CROSS_W),
            norm_g[l], w_in_t, l, w_glu[l], b_glu[l], w_branch_attn[l], w_branch_ssm[l],
            w_branch_cross[l], w_out[l], ssm, final_norm_g,
            tm=db * t, n_scan=1, seg_len=t, chain=False, hi=True, tq=t, final=l == depth - 1)
        outs[7].append(kn); outs[8].append(vn); outs[9].append(kin)
        outs[10].append(sr.reshape(db, SSM_GROUPS, SSM_STATE)); outs[11].append(si.reshape(db, SSM_GROUPS, SSM_STATE))

    return (yp, ys) + tuple(jnp.stack(o, axis=0) for o in outs)
```

```python
import functools
import math

import jax
import jax.numpy as jnp
import numpy as np
from jax import lax
from jax.experimental import pallas as pl
from jax.experimental.pallas import tpu as pltpu

F32 = jnp.float32
BF16 = jnp.bfloat16
I32 = jnp.int32
HIGHEST = lax.Precision.HIGHEST
NT_DIMS = (((1,), (1,)), ((), ()))

VMEM_LIMIT_BYTES = 56 * 1024 * 1024
LANES = 128
SUBLANES = 8

D_MODEL = 2048
HEAD_DIM = 128
ATT_HEADS = 6
ATT_W = ATT_HEADS * HEAD_DIM
IDX_HEADS = 16
IDX_DIM = 64
IDX_W = IDX_HEADS * IDX_DIM
TOPK_MAX = 256
QBLOCK = 128
SSM_GROUP = 16
SSM_W = 768
SSM_GROUPS = SSM_W // SSM_GROUP
SSM_STATE = 64
CROSS_HEADS = 4
CROSS_HD = 128
CROSS_W = CROSS_HEADS * CROSS_HD
REL_BUCKETS = 32
REL_MAX_DIST = 128
N_BRANCHES = 3
EPS = 1e-6
SPLIT_WIDTHS = (ATT_W, ATT_W, ATT_W, ATT_W, IDX_W, IDX_HEADS, IDX_DIM,
                SSM_W, SSM_W, CROSS_W, CROSS_W, N_BRANCHES * D_MODEL)
SPLIT_POINTS = tuple(int(c) for c in np.cumsum(SPLIT_WIDTHS)[:-1])

COL_GM = 0
COL_Q = 6144
COL_K = COL_Q + ATT_W
COL_V = COL_K + ATT_W
COL_GA = COL_V + ATT_W
COL_U = COL_GA + ATT_W
COL_GS = COL_U + SSM_W
COL_QC = COL_GS + SSM_W
COL_GC = COL_QC + CROSS_W
COL_KIWI = COL_GC + CROSS_W
COL_WI = COL_KIWI
COL_KI = COL_KIWI + IDX_HEADS
COL_QI = 12288
PROJ_W = COL_QI + IDX_W
PROJ_TN = 512
PROJ_ROW_ALIGN = 16
PROJ_LO_TILES = COL_KIWI // PROJ_TN


def _proj_tile_rows():
    ref_start = dict(zip(("q", "k", "v", "ga", "qi", "wi", "ki", "u", "gs", "qc", "gc", "gm"),
                         (0,) + SPLIT_POINTS))
    runs = ((COL_GM, "gm", N_BRANCHES * D_MODEL), (COL_Q, "q", 4 * ATT_W), (COL_U, "u", 2 * SSM_W),
            (COL_QC, "qc", 2 * CROSS_W), (COL_KIWI, "wi", PROJ_TN), (COL_QI, "qi", IDX_W))
    rows = np.zeros(PROJ_W // PROJ_TN, np.int32)
    for col, name, width in runs:
        for t in range(width // PROJ_TN):
            rows[col // PROJ_TN + t] = ref_start[name] + t * PROJ_TN
    assert not (rows % PROJ_ROW_ALIGN).any()
    return rows

SLAB_GROUPS = LANES // SSM_GROUP
SLAB_STATES = SLAB_GROUPS * SSM_STATE
N_SLABS = SSM_GROUPS // SLAB_GROUPS
SCAN_ROWS = SUBLANES

KEY_EXTENT_CLASSES = 16
COUNT_FOLD_ROWS = 64

PAGES_PER_STEP = 8
IDX_PAGES_PER_STEP = 32
PAGE_BUFFERS = 2

INT_MIN = -2 ** 31
LOWEST_F32_KEY = -0x7F800000


def _params(*sem):
    return pltpu.CompilerParams(dimension_semantics=sem, vmem_limit_bytes=VMEM_LIMIT_BYTES)


def _sigmoid(x):
    return 1.0 / (1.0 + jnp.exp(-x))


def _silu(x):
    return x * _sigmoid(x)


def _gelu_tanh(x):
    return 0.5 * x * (1.0 + jnp.tanh(math.sqrt(2.0 / math.pi) * (x + 0.044715 * (x * x * x))))


def _dot(a, b, hi):
    if hi:
        return jnp.dot(a.astype(F32), b.astype(F32), precision=HIGHEST, preferred_element_type=F32)
    return jnp.dot(a.astype(BF16), b.astype(BF16), preferred_element_type=F32)


def _split_bf16(x):
    hi = x.astype(BF16).astype(F32)
    return hi, x - hi


def _ordered_bits_to_float(key):
    return lax.bitcast_convert_type(key ^ ((key >> 31) & I32(0x7FFFFFFF)), F32)


def _kth_largest(score, k, axis=1):
    def count_ge(cand_bits):
        ones = (score >= _ordered_bits_to_float(cand_bits)).astype(I32)
        if axis == 0:
            while ones.shape[0] > COUNT_FOLD_ROWS and ones.shape[0] % (2 * SUBLANES) == 0:
                half = ones.shape[0] // 2
                ones = ones[:half] + ones[half:]
            return jnp.sum(ones.astype(F32), axis=0, keepdims=True).astype(I32)
        return jnp.sum(ones, axis=axis, keepdims=True)

    zero = jnp.zeros((score.shape[0], 1) if axis == 1 else (1, score.shape[1]), I32)
    t = jnp.where(count_ge(zero) >= k, zero, jnp.full_like(zero, INT_MIN))

    def body(it, t):
        cand = t | (I32(1) << (I32(30) - it))
        return jnp.where(count_ge(cand) >= k, cand, t)

    t = lax.fori_loop(0, 31, body, t)
    return _ordered_bits_to_float(jnp.maximum(t, I32(LOWEST_F32_KEY)))


def _t5_bucket(dist):
    n = jnp.maximum(dist, 0)
    max_exact = REL_BUCKETS // 2
    nf = jnp.maximum(n, 1).astype(F32)
    scale = (REL_BUCKETS - max_exact) / math.log(REL_MAX_DIST / max_exact)
    large = max_exact + jnp.floor(jnp.log(nf * (1.0 / max_exact)) * scale).astype(I32)
    large = jnp.minimum(large, REL_BUCKETS - 1)
    return jnp.where(n < max_exact, n, large)


def _rms_split_kernel(x_ref, g_ref, hh_ref, hl_ref):
    x = x_ref[...]
    y = x * lax.rsqrt(jnp.mean(x * x, axis=-1, keepdims=True) + EPS) * g_ref[...]
    hh = y.astype(BF16)
    hh_ref[...] = hh
    hl_ref[...] = (y - hh.astype(F32)).astype(BF16)


def _rms_split(x2d, g, tm):
    m, d = x2d.shape
    return pl.pallas_call(
        _rms_split_kernel,
        grid=(m // tm,),
        in_specs=[pl.BlockSpec((tm, d), lambda i: (i, 0)), pl.BlockSpec((1, d), lambda i: (0, 0))],
        out_specs=[pl.BlockSpec((tm, d), lambda i: (i, 0))] * 2,
        out_shape=[jax.ShapeDtypeStruct((m, d), BF16)] * 2,
        compiler_params=_params("parallel"),
        name="rms_split",
    )(x2d, g.reshape(1, d))


def _inproj_kernel(rows_ref, hh_ref, hl_ref, wt_ref, o_ref, *, lo_tiles):
    j = pl.program_id(1)
    kiwi_tile = COL_KIWI // PROJ_TN

    def three_pass(w):
        w_hi = w.astype(BF16)
        w_lo = (w - w_hi.astype(F32)).astype(BF16)
        hh = hh_ref[...]
        acc = lax.dot_general(hh, w_hi, NT_DIMS, preferred_element_type=F32)
        acc = acc + lax.dot_general(hl_ref[...], w_hi, NT_DIMS, preferred_element_type=F32)
        return acc + lax.dot_general(hh, w_lo, NT_DIMS, preferred_element_type=F32)

    @pl.when(j < lo_tiles)
    def _():
        o_ref[...] = lax.dot_general(hh_ref[...], wt_ref[...].astype(BF16), NT_DIMS,
                                     preferred_element_type=F32)

    @pl.when((j >= lo_tiles) & (j != kiwi_tile))
    def _():
        o_ref[...] = three_pass(wt_ref[...])

    @pl.when(j == kiwi_tile)
    def _():
        o_ref[:, 0:LANES] = three_pass(wt_ref[0:LANES, :])
        o_ref[:, LANES:PROJ_TN] = jnp.zeros((o_ref.shape[0], PROJ_TN - LANES), F32)


def _inproj(hh, hl, w_in_t, layer, tm, lo_tiles):
    m, d = hh.shape
    n_tiles = PROJ_W // PROJ_TN
    resident = lambda: pl.BlockSpec((tm, d), lambda i, j, rows: (i, 0), pipeline_mode=pl.Buffered(1))
    return pl.pallas_call(
        functools.partial(_inproj_kernel, lo_tiles=lo_tiles),
        grid_spec=pltpu.PrefetchScalarGridSpec(
            num_scalar_prefetch=1,
            grid=(m // tm, n_tiles),
            in_specs=[resident(), resident(),
                      pl.BlockSpec((None, pl.Element(PROJ_TN), pl.Element(d)),
                                   lambda i, j, rows: (layer, rows[j] * PROJ_ROW_ALIGN, 0))],
            out_specs=pl.BlockSpec((tm, PROJ_TN), lambda i, j, rows: (i, j))),
        out_shape=jax.ShapeDtypeStruct((m, PROJ_W), F32),
        compiler_params=_params("parallel", "arbitrary"),
        name="inproj",
    )(jnp.asarray(_proj_tile_rows() // PROJ_ROW_ALIGN), hh, hl, w_in_t)


def _memkv_kernel(a_ref, w_ref, o_ref):
    o_ref[...] = _dot(a_ref[...], w_ref[...], False)


def _memkv(mem2d, w):
    m, d = mem2d.shape
    n = w.shape[1]
    tn = 512
    return pl.pallas_call(
        _memkv_kernel,
        grid=(n // tn,),
        in_specs=[pl.BlockSpec((m, d), lambda j: (0, 0)), pl.BlockSpec((d, tn), lambda j: (0, j))],
        out_specs=pl.BlockSpec((m, tn), lambda j: (0, j)),
        out_shape=jax.ShapeDtypeStruct((m, n), F32),
        compiler_params=_params("parallel"),
        name="memkv",
    )(mem2d, w)


def _bias_tile_kernel(rb_ref, o_ref):
    c = lax.broadcasted_iota(I32, (2 * QBLOCK, QBLOCK), 0)
    r = lax.broadcasted_iota(I32, (2 * QBLOCK, QBLOCK), 1)
    bucket = _t5_bucket(r - c + QBLOCK)
    for h in range(ATT_HEADS):
        tile = jnp.zeros((2 * QBLOCK, QBLOCK), F32)
        for b in range(REL_BUCKETS):
            tile = jnp.where(bucket == b, rb_ref[b, h], tile)
        o_ref[h] = tile


def _bias_tiles(rel_bias):
    return pl.pallas_call(
        _bias_tile_kernel,
        in_specs=[pl.BlockSpec(memory_space=pltpu.SMEM)],
        out_specs=pl.BlockSpec(memory_space=pltpu.VMEM),
        out_shape=jax.ShapeDtypeStruct((ATT_HEADS, 2 * QBLOCK, QBLOCK), F32),
        name="bias_tiles",
    )(rel_bias)


def _dsa_prompt_kernel(rb_ref, q_ref, ga_ref, qi_ref, kwq_ref, k_ref, v_ref, kwk_ref, bt_ref, o_ref,
                       ki4_ref, kb_ref, vt_ref, *, seq, topk):
    i = pl.program_id(1)

    @pl.when(i == 0)
    def _():
        kh, kl = _split_bf16(kwk_ref[:, IDX_HEADS:IDX_HEADS + IDX_DIM])
        ki4_ref[...] = jnp.concatenate([kh, kl, kh, kl], axis=1).astype(BF16)
        kb_ref[...] = k_ref[...].astype(BF16)
        for kb in range(seq // QBLOCK):
            for h in range(ATT_HEADS):
                rows = slice(kb * QBLOCK, (kb + 1) * QBLOCK)
                cols = slice(h * HEAD_DIM, (h + 1) * HEAD_DIM)
                vt_ref[cols, rows] = v_ref[rows, cols].T.astype(BF16)

    n_blk = seq // QBLOCK
    per = n_blk // KEY_EXTENT_CLASSES
    for cls in range(KEY_EXTENT_CLASSES):
        pl.when(i // per == cls)(functools.partial(
            _dsa_prompt_block, i, (cls + 1) * per * QBLOCK, per, topk,
            rb_ref, q_ref, ga_ref, qi_ref, kwq_ref, bt_ref, o_ref, ki4_ref, kb_ref, vt_ref))


def _dsa_prompt_block(i, nk, per, topk, rb_ref, q_ref, ga_ref, qi_ref, kwq_ref, bt_ref, o_ref,
                      ki4_ref, kb_ref, vt_ref):
    qi = qi_ref[...]
    wi_t = kwq_ref[...].T * (IDX_HEADS ** -0.5)
    score = jnp.zeros((nk, QBLOCK), F32)
    for h0 in range(0, IDX_HEADS, 2):
        q4 = []
        for h in (h0, h0 + 1):
            qh, ql = _split_bf16(qi[:, h * IDX_DIM:(h + 1) * IDX_DIM])
            q4.append(jnp.concatenate([qh, qh, ql, ql], axis=1).astype(BF16))
        s = lax.dot_general(ki4_ref[0:nk, :], jnp.concatenate(q4, axis=0), NT_DIMS,
                            preferred_element_type=F32)
        score = score + jnp.maximum(s[:, 0:QBLOCK], 0.0) * wi_t[h0:h0 + 1, :]
        score = score + jnp.maximum(s[:, QBLOCK:2 * QBLOCK], 0.0) * wi_t[h0 + 1:h0 + 2, :]

    key_pos = lax.broadcasted_iota(I32, (nk, QBLOCK), 0)
    q_pos = i * QBLOCK + lax.broadcasted_iota(I32, (nk, QBLOCK), 1)
    score = jnp.where(key_pos <= q_pos, score, -jnp.inf)
    mask_add = jnp.where(score >= _kth_largest(score, topk, axis=0), 0.0, -jnp.inf)

    near = min(nk, (per + 1) * QBLOCK)
    key_blk = key_pos[nk - near:, :] // QBLOCK
    is_diag = key_blk == i
    is_prev = key_blk == i - 1
    reps = near // QBLOCK
    q = q_ref[...] * (HEAD_DIM ** -0.5)
    ga = ga_ref[...]
    for h in range(ATT_HEADS):
        cols = slice(h * HEAD_DIM, (h + 1) * HEAD_DIM)
        far_bias = rb_ref[REL_BUCKETS - 1, h]
        t_prev = jnp.concatenate([bt_ref[h, 0:QBLOCK, :]] * reps, axis=0)
        t_diag = jnp.concatenate([bt_ref[h, QBLOCK:2 * QBLOCK, :]] * reps, axis=0)
        bias = jnp.where(is_diag, t_diag, jnp.where(is_prev, t_prev, far_bias))
        if nk > near:
            bias = jnp.concatenate([jnp.full((nk - near, QBLOCK), far_bias, F32), bias], axis=0)
        logits = lax.dot_general(kb_ref[0:nk, cols], q[:, cols].astype(BF16), NT_DIMS,
                                 preferred_element_type=F32)
        logits = logits + bias + mask_add
        m = jnp.max(logits, axis=0, keepdims=True)
        p = jnp.exp(logits - m)
        l = jnp.sum(p, axis=0, keepdims=True)
        o_t = jnp.dot(vt_ref[cols, 0:nk], p.astype(BF16), preferred_element_type=F32)
        o_ref[:, cols] = (o_t / l).T * _silu(ga[:, cols])


def _dsa_prompt(p3, bias_tiles, rel_bias):
    b, s, _ = p3.shape
    topk = min(TOPK_MAX, s // 4)
    row = lambda col, w: pl.BlockSpec((None, QBLOCK, w), lambda bi, i: (bi, i, col // w))
    full = lambda col, w: pl.BlockSpec((None, s, w), lambda bi, i: (bi, 0, col // w))
    return pl.pallas_call(
        functools.partial(_dsa_prompt_kernel, seq=s, topk=topk),
        grid=(b, s // QBLOCK),
        in_specs=[pl.BlockSpec(memory_space=pltpu.SMEM),
                  row(COL_Q, ATT_W), row(COL_GA, ATT_W), row(COL_QI, IDX_W), row(COL_KIWI, LANES),
                  full(COL_K, ATT_W), full(COL_V, ATT_W), full(COL_KIWI, LANES),
                  pl.BlockSpec((ATT_HEADS, 2 * QBLOCK, QBLOCK), lambda bi, i: (0, 0, 0))],
        out_specs=pl.BlockSpec((None, QBLOCK, ATT_W), lambda bi, i: (bi, i, 0)),
        out_shape=jax.ShapeDtypeStruct((b, s, ATT_W), F32),
        scratch_shapes=[pltpu.VMEM((s, 4 * IDX_DIM), BF16),
                        pltpu.VMEM((s, ATT_W), BF16),
                        pltpu.VMEM((ATT_W, s), BF16)],
        compiler_params=_params("parallel", "arbitrary"),
        name="dsa_prompt",
    )(rel_bias, p3, p3, p3, p3, p3, p3, p3, bias_tiles)


def _sidx_kernel(pt_ref, q_ref, w_ref, *rest, nch):
    k_refs = rest[:IDX_PAGES_PER_STEP]
    kn_ref, s_ref, q4_ref = rest[IDX_PAGES_PER_STEP:]
    c = pl.program_id(1)

    @pl.when(c == 0)
    def _():
        qh, ql = _split_bf16(q_ref[...])
        q4_ref[...] = jnp.concatenate([qh, qh, ql, ql], axis=1).astype(BF16)

    def scores(ki_t):
        n = ki_t.shape[1]
        kh, kl = _split_bf16(ki_t)
        ki4_t = jnp.concatenate([kh, kl, kh, kl], axis=0).astype(BF16)
        s = jnp.dot(q4_ref[...], ki4_t, preferred_element_type=F32)
        s = jnp.maximum(s, 0.0) * (w_ref[...] * (IDX_HEADS ** -0.5))
        return jnp.sum(s.reshape(IDX_HEADS, SUBLANES, n), axis=0)

    @pl.when(c < nch)
    def _():
        s_ref[...] = scores(jnp.concatenate([r[...] for r in k_refs], axis=1))

    @pl.when(c == nch)
    def _():
        n = kn_ref.shape[1]
        sn = scores(kn_ref[...])
        t_key = lax.broadcasted_iota(I32, (SUBLANES, n), 1)
        t_q = lax.broadcasted_iota(I32, (SUBLANES, n), 0)
        s_ref[...] = jnp.full(s_ref.shape, -jnp.inf, F32)
        s_ref[:, 0:n] = jnp.where(t_key <= t_q, sn, -jnp.inf)


def _sattn_kernel(pt_ref, rb_ref, sf_ref, sc_ref, q_ref, ga_ref, *rest, nch, past, topk, page):
    k_refs = rest[:PAGES_PER_STEP]
    v_refs = rest[PAGES_PER_STEP:2 * PAGES_PER_STEP]
    kn_ref, vn_ref, o_ref, thr_ref, m_ref, l_ref, acc_ref = rest[2 * PAGES_PER_STEP:]
    c = pl.program_id(1)
    chunk = PAGES_PER_STEP * page

    @pl.when(c == 0)
    def _():
        thr_ref[...] = _kth_largest(sf_ref[...], topk)
        m_ref[...] = jnp.full(m_ref.shape, -jnp.inf, F32)
        l_ref[...] = jnp.zeros(l_ref.shape, F32)
        acc_ref[...] = jnp.zeros(acc_ref.shape, F32)

    def attend(k_of, v_of, sc, key_pos0):
        n = sc.shape[1]
        mask_add = jnp.where(sc >= thr_ref[...], 0.0, -jnp.inf)
        q_pos = past + lax.broadcasted_iota(I32, (SUBLANES, n), 0)
        key_pos = key_pos0 + lax.broadcasted_iota(I32, (SUBLANES, n), 1)
        bucket = _t5_bucket(q_pos - key_pos)
        q = q_ref[...]
        rows = []
        for h in range(ATT_HEADS):
            cols = slice(h * HEAD_DIM, (h + 1) * HEAD_DIM)
            bias = jnp.zeros((SUBLANES, n), F32)
            for b in range(REL_BUCKETS):
                bias = jnp.where(bucket == b, rb_ref[b, h], bias)
            lg = lax.dot_general(q[:, cols].astype(BF16), k_of(h).astype(BF16), NT_DIMS,
                                 preferred_element_type=F32)
            rows.append(lg * (HEAD_DIM ** -0.5) + bias + mask_add)
        logits = jnp.concatenate(rows, axis=0)
        m_old = m_ref[...]
        m_new = jnp.maximum(m_old, jnp.max(logits, axis=1, keepdims=True))
        m_safe = jnp.where(m_new == -jnp.inf, 0.0, m_new)
        alpha = jnp.exp(m_old - m_safe)
        p = jnp.exp(logits - m_safe[:, 0:1])
        l_ref[...] = alpha * l_ref[...] + jnp.sum(p, axis=1, keepdims=True)
        m_ref[...] = m_new
        for h in range(ATT_HEADS):
            cols = slice(h * HEAD_DIM, (h + 1) * HEAD_DIM)
            rs = slice(h * SUBLANES, (h + 1) * SUBLANES)
            acc_ref[:, cols] = alpha[rs] * acc_ref[:, cols] + jnp.dot(
                p[rs].astype(BF16), v_of(h).astype(BF16), preferred_element_type=F32)

    @pl.when(c < nch)
    def _():
        k_of = lambda h: jnp.concatenate([r[h] for r in k_refs], axis=0)
        v_of = lambda h: jnp.concatenate([r[h] for r in v_refs], axis=0)
        attend(k_of, v_of, sc_ref[...], c * chunk)

    @pl.when(c == nch)
    def _():
        n = kn_ref.shape[0]
        head = lambda ref: (lambda h: ref[:, h * HEAD_DIM:(h + 1) * HEAD_DIM])
        attend(head(kn_ref), head(vn_ref), sc_ref[:, 0:n], past)
        ga = ga_ref[...]
        l = l_ref[...]
        for h in range(ATT_HEADS):
            cols = slice(h * HEAD_DIM, (h + 1) * HEAD_DIM)
            o_ref[:, cols] = acc_ref[:, cols] / l[h * SUBLANES:(h + 1) * SUBLANES] * _silu(ga[:, cols])


def _dsa_sample(layer, ps3, cache_k_t, cache_v_t, cache_kidx_t, page_table, rel_bias):
    db, t, _ = ps3.shape
    n_pages = page_table.shape[1]
    page = cache_kidx_t.shape[3]
    past = n_pages * page
    topk = min(TOPK_MAX, (past + t) // 4)
    nch = n_pages // PAGES_PER_STEP
    chunk = PAGES_PER_STEP * page
    nch_idx = n_pages // IDX_PAGES_PER_STEP
    chunk_idx = IDX_PAGES_PER_STEP * page
    ntot = (nch_idx + 1) * chunk_idx
    assert chunk_idx % chunk == 0 and t == SUBLANES

    qi = ps3[:, :, COL_QI:COL_QI + IDX_W].reshape(db, t, IDX_HEADS, IDX_DIM)
    qi_hq = qi.transpose(0, 2, 1, 3).reshape(db, IDX_HEADS * t, IDX_DIM)
    wi_hq = ps3[:, :, COL_WI:COL_WI + IDX_HEADS].transpose(0, 2, 1).reshape(db, IDX_HEADS * t, 1)
    pad_rows = lambda a: jnp.pad(a, ((0, 0), (0, page - t), (0, 0)))
    ki_new_t = pad_rows(ps3[:, :, COL_KI:COL_KI + IDX_DIM]).transpose(0, 2, 1)
    k_new = pad_rows(ps3[:, :, COL_K:COL_K + ATT_W])
    v_new = pad_rows(ps3[:, :, COL_V:COL_V + ATT_W])
    q = ps3[:, :, COL_Q:COL_Q + ATT_W]
    ga = ps3[:, :, COL_GA:COL_GA + ATT_W]

    def page_spec(r, per_step, dims, **kw):
        def imap(b, c, pt):
            return (layer, pt[b, jnp.minimum(c * per_step + r, n_pages - 1)]) + (0,) * len(dims)
        return pl.BlockSpec((None, None) + dims, imap, **kw)

    per_b = lambda rows, w: pl.BlockSpec((None, rows, w), lambda b, c, pt: (b, 0, 0))

    scores = pl.pallas_call(
        functools.partial(_sidx_kernel, nch=nch_idx),
        grid_spec=pltpu.PrefetchScalarGridSpec(
            num_scalar_prefetch=1,
            grid=(db, nch_idx + 1),
            in_specs=[per_b(IDX_HEADS * t, IDX_DIM), per_b(IDX_HEADS * t, 1)]
                     + [page_spec(r, IDX_PAGES_PER_STEP, (IDX_DIM, page)) for r in range(IDX_PAGES_PER_STEP)]
                     + [per_b(IDX_DIM, page)],
            out_specs=pl.BlockSpec((None, t, chunk_idx), lambda b, c, pt: (b, 0, c)),
            scratch_shapes=[pltpu.VMEM((IDX_HEADS * t, 4 * IDX_DIM), BF16)]),
        out_shape=jax.ShapeDtypeStruct((db, t, ntot), F32),
        compiler_params=_params("parallel", "arbitrary"),
        name="dsa_sample_scores",
    )(page_table, qi_hq, wi_hq, *([cache_kidx_t] * IDX_PAGES_PER_STEP), ki_new_t)

    return pl.pallas_call(
        functools.partial(_sattn_kernel, nch=nch, past=past, topk=topk, page=page),
        grid_spec=pltpu.PrefetchScalarGridSpec(
            num_scalar_prefetch=1,
            grid=(db, nch + 1),
            in_specs=[pl.BlockSpec(memory_space=pltpu.SMEM),
                      per_b(t, ntot),
                      pl.BlockSpec((None, t, chunk), lambda b, c, pt: (b, 0, c)),
                      per_b(t, ATT_W), per_b(t, ATT_W)]
                     + [page_spec(r, PAGES_PER_STEP, (ATT_HEADS, page, HEAD_DIM),
                                  pipeline_mode=pl.Buffered(PAGE_BUFFERS)) for r in range(PAGES_PER_STEP)] * 2
                     + [per_b(page, ATT_W), per_b(page, ATT_W)],
            out_specs=per_b(t, ATT_W),
            scratch_shapes=[pltpu.VMEM((t, 1), F32),
                            pltpu.VMEM((ATT_HEADS * t, LANES), F32),
                            pltpu.VMEM((ATT_HEADS * t, LANES), F32),
                            pltpu.VMEM((t, ATT_W), F32)]),
        out_shape=jax.ShapeDtypeStruct((db, t, ATT_W), F32),
        compiler_params=_params("parallel", "arbitrary"),
        name="dsa_sample_attn",
    )(page_table, rel_bias, scores, scores, q, ga,
      *([cache_k_t] * PAGES_PER_STEP), *([cache_v_t] * PAGES_PER_STEP), k_new, v_new)


def _ssm_param_kernel(are_ref, aim_ref, ldt_ref, bre_ref, bim_ref, lbr_ref, lbi_ref, bbr_ref, bbi_ref):
    lam_re = jnp.minimum(are_ref[...], -1e-4)
    lam_im = aim_ref[...]
    step = jnp.exp(ldt_ref[...])
    mag = jnp.exp(lam_re * step)
    ang = lam_im * step
    lb_re = mag * jnp.cos(ang)
    lb_im = mag * jnp.sin(ang)
    nr = lb_re - 1.0
    ni = lb_im
    den = lam_re * lam_re + lam_im * lam_im
    coef_re = (nr * lam_re + ni * lam_im) / den
    coef_im = (ni * lam_re - nr * lam_im) / den
    lbr_ref[...] = lb_re
    lbi_ref[...] = lb_im
    for c in range(SSM_GROUP):
        br = bre_ref[c]
        bi = bim_ref[c]
        bbr_ref[c] = coef_re * br - coef_im * bi
        bbi_ref[c] = coef_re * bi + coef_im * br


def _ssm_params(a_re, a_im, log_dt, b_re, b_im):
    g, p, c = b_re.shape
    vm = pl.BlockSpec(memory_space=pltpu.VMEM)
    return pl.pallas_call(
        _ssm_param_kernel,
        in_specs=[vm] * 5,
        out_specs=[vm] * 4,
        out_shape=[jax.ShapeDtypeStruct((g, p), F32)] * 2 + [jax.ShapeDtypeStruct((c, g, p), F32)] * 2,
        name="ssm_params",
    )(a_re, a_im, log_dt.reshape(g, 1), b_re.transpose(2, 0, 1), b_im.transpose(2, 0, 1))


def _block_diag_slabs(m):
    g, a, b = m.shape
    eye = jnp.eye(SLAB_GROUPS, dtype=m.dtype)
    m4 = m.reshape(N_SLABS, SLAB_GROUPS, a, b)
    return jnp.einsum('sgab,gh->sgahb', m4, eye).reshape(N_SLABS, SLAB_GROUPS * a, SLAB_GROUPS * b)


def _scan_pitch(seg_len):
    return seg_len + 4 if seg_len % 8 == 0 else seg_len


def _s5_kernel(u_ref, x0r_ref, x0i_ref, lbr_ref, lbi_ref, wbu_ref, wyr_ref, wyi_ref, d_ref,
               z_ref, xr_ref, xi_ref, up_ref, zs_ref, *, seg_len, chain, hi):
    nc = SLAB_STATES // LANES
    pitch = _scan_pitch(seg_len)
    up_ref[...] = jnp.zeros(up_ref.shape, F32)
    for s in range(SCAN_ROWS):
        up_ref[pl.ds(s * pitch, seg_len), :] = u_ref[pl.ds(s * seg_len, seg_len), :]
    u = up_ref[...]
    bu = _dot(u, wbu_ref[...], hi)
    for j in range(2 * nc):
        zs_ref[j] = bu[:, j * LANES:(j + 1) * LANES]
    shape = (SCAN_ROWS, SLAB_STATES)
    lbr = jnp.broadcast_to(lbr_ref[...], shape)
    lbi = jnp.broadcast_to(lbi_ref[...], shape)

    def load(t, base):
        rows = pl.ds(t, SCAN_ROWS, stride=pitch)
        return jnp.concatenate([zs_ref[base + j, rows, :] for j in range(nc)], axis=1)

    def store(t, base, val):
        rows = pl.ds(t, SCAN_ROWS, stride=pitch)
        for j in range(nc):
            zs_ref[base + j, rows, :] = val[:, j * LANES:(j + 1) * LANES]

    def advance(t, carry):
        xr, xi = carry
        nr = lbr * xr - lbi * xi + load(t, 0)
        ni = lbr * xi + lbi * xr + load(t, nc)
        return nr, ni

    def advance_store(t, carry):
        nr, ni = advance(t, carry)
        store(t, 0, nr)
        store(t, nc, ni)
        return nr, ni

    unroll = min(seg_len, 8)
    x0r = x0r_ref[...]
    x0i = x0i_ref[...]
    if chain:
        zero = jnp.zeros(shape, F32)
        er, ei = lax.fori_loop(0, seg_len, advance, (zero, zero), unroll=unroll)
        pr, pi = lbr, lbi
        for _ in range(int(math.log2(seg_len))):
            pr, pi = pr * pr - pi * pi, 2.0 * pr * pi
        row = lax.broadcasted_iota(I32, shape, 0)
        sr = jnp.where(row == 0, x0r, 0.0)
        si = jnp.where(row == 0, x0i, 0.0)
        for s in range(SCAN_ROWS - 1):
            nr = er + pr * sr - pi * si
            ni = ei + pr * si + pi * sr
            sr = jnp.where(row == s + 1, pltpu.roll(nr, 1, axis=0), sr)
            si = jnp.where(row == s + 1, pltpu.roll(ni, 1, axis=0), si)
        x0r, x0i = sr, si
    xr, xi = lax.fori_loop(0, seg_len, advance_store, (x0r, x0i), unroll=unroll)
    xr_ref[...] = xr
    xi_ref[...] = xi
    y = d_ref[...] * u
    for j in range(nc):
        blk = slice(j * LANES, (j + 1) * LANES)
        y = y + _dot(zs_ref[j], wyr_ref[blk, :], hi) - _dot(zs_ref[nc + j], wyi_ref[blk, :], hi)
    up_ref[...] = _gelu_tanh(y)
    for s in range(SCAN_ROWS):
        z_ref[pl.ds(s * seg_len, seg_len), :] = up_ref[pl.ds(s * pitch, seg_len), :]


def _s5(p3, x0r, x0i, lbr, lbi, wbu, wyr, wyi, d, *, seg_len, chain, hi):
    nb, r, _ = p3.shape
    assert r == SCAN_ROWS * seg_len and (not chain or seg_len & (seg_len - 1) == 0)
    rp = SCAN_ROWS * _scan_pitch(seg_len)
    slab = lambda rows, w: pl.BlockSpec((None, rows, w), lambda b, s: (s, 0, 0))
    st = pl.BlockSpec((None, SCAN_ROWS, SLAB_STATES), lambda b, s: (b, 0, s))
    return pl.pallas_call(
        functools.partial(_s5_kernel, seg_len=seg_len, chain=chain, hi=hi),
        grid=(nb, N_SLABS),
        in_specs=[pl.BlockSpec((None, r, LANES), lambda b, s: (b, 0, COL_U // LANES + s)),
                  st, st, slab(1, SLAB_STATES), slab(1, SLAB_STATES),
                  slab(LANES, 2 * SLAB_STATES), slab(SLAB_STATES, LANES), slab(SLAB_STATES, LANES),
                  slab(1, LANES)],
        out_specs=[pl.BlockSpec((None, r, LANES), lambda b, s: (b, 0, s)), st, st],
        out_shape=[jax.ShapeDtypeStruct((nb, r, SSM_W), F32),
                   jax.ShapeDtypeStruct((nb, SCAN_ROWS, SSM_GROUPS * SSM_STATE), F32),
                   jax.ShapeDtypeStruct((nb, SCAN_ROWS, SSM_GROUPS * SSM_STATE), F32)],
        scratch_shapes=[pltpu.VMEM((rp, LANES), F32),
                        pltpu.VMEM((2 * SLAB_STATES // LANES, rp, LANES), F32)],
        compiler_params=_params("parallel", "parallel"),
        name="s5_scan",
    )(p3, x0r, x0i, lbr, lbi, wbu, wyr, wyi, d)


def _cross_kernel(q_ref, g_ref, mk_ref, mv_ref, o_ref):
    q = q_ref[...]
    g = g_ref[...]
    for h in range(CROSS_HEADS):
        cols = slice(h * CROSS_HD, (h + 1) * CROSS_HD)
        logits = lax.dot_general(q[:, cols].astype(BF16), mk_ref[:, cols].astype(BF16), NT_DIMS,
                                 preferred_element_type=F32) * (CROSS_HD ** -0.5)
        m = jnp.max(logits, axis=1, keepdims=True)
        p = jnp.exp(logits - m)
        l = jnp.sum(p, axis=1, keepdims=True)
        o = jnp.dot(p.astype(BF16), mv_ref[:, cols].astype(BF16), preferred_element_type=F32)
        o_ref[:, cols] = o / l * _silu(g[:, cols])


def _cross(p3, mk, mv, tq):
    nb, s, _ = p3.shape
    mem = mk.shape[1]
    qspec = lambda col: pl.BlockSpec((None, tq, CROSS_W), lambda b, i: (b, i, col // CROSS_W))
    mspec = pl.BlockSpec((None, mem, CROSS_W), lambda b, i: (b, 0, 0))
    return pl.pallas_call(
        _cross_kernel,
        grid=(nb, s // tq),
        in_specs=[qspec(COL_QC), qspec(COL_GC), mspec, mspec],
        out_specs=pl.BlockSpec((None, tq, CROSS_W), lambda b, i: (b, i, 0)),
        out_shape=jax.ShapeDtypeStruct((nb, s, CROSS_W), F32),
        compiler_params=_params("parallel", "parallel"),
        name="cross_attn",
    )(p3, p3, mk, mv)


def _merge_out_kernel(x_ref, a_ref, z_ref, c_ref, gs_ref, gm0_ref, gm1_ref, gm2_ref,
                      wglu_ref, bglu_ref, wba_ref, wbs_ref, wbc_ref, wout_ref, gfin_ref, o_ref, *, hi, final):
    gl = _dot(z_ref[...], wglu_ref[...], hi) + bglu_ref[...]
    s_out = gl[:, 0:SSM_W] * _sigmoid(gl[:, SSM_W:2 * SSM_W]) * _silu(gs_ref[...])
    merged = _sigmoid(gm0_ref[...]) * _dot(a_ref[...], wba_ref[...], hi)
    merged = merged + _sigmoid(gm1_ref[...]) * _dot(s_out, wbs_ref[...], hi)
    merged = merged + _sigmoid(gm2_ref[...]) * _dot(c_ref[...], wbc_ref[...], hi)
    y = x_ref[...] + _dot(merged, wout_ref[...], hi)
    if final:
        y = y * lax.rsqrt(jnp.mean(y * y, axis=-1, keepdims=True) + EPS) * gfin_ref[...]
    o_ref[...] = y


def _merge_out(x2d, a2d, z2d, c2d, p2d, w_glu, b_glu, w_ba, w_bs, w_bc, w_out, g_final, tm, hi, final):
    m, d = x2d.shape
    rows = lambda w, col=0: pl.BlockSpec((tm, w), lambda i: (i, col // w))
    const = lambda arr: pl.BlockSpec(arr.shape, lambda i: (0, 0), pipeline_mode=pl.Buffered(1))
    b2 = b_glu.reshape(1, -1)
    g2 = g_final.reshape(1, d)
    return pl.pallas_call(
        functools.partial(_merge_out_kernel, hi=hi, final=final),
        grid=(m // tm,),
        in_specs=[rows(d), rows(ATT_W), rows(SSM_W), rows(CROSS_W), rows(SSM_W, COL_GS),
                  rows(D_MODEL, COL_GM), rows(D_MODEL, COL_GM + D_MODEL), rows(D_MODEL, COL_GM + 2 * D_MODEL),
                  const(w_glu), const(b2), const(w_ba), const(w_bs), const(w_bc), const(w_out), const(g2)],
        out_specs=rows(d),
        out_shape=jax.ShapeDtypeStruct((m, d), F32),
        compiler_params=_params("parallel"),
        name="merge_out",
    )(x2d, a2d, z2d, c2d, p2d, p2d, p2d, p2d, w_glu, b2, w_ba, w_bs, w_bc, w_out, g2)


def _mixer_layer(x3, attend, x0r, x0i, mk, mv, norm_g, w_in_t, layer, w_glu, b_glu, w_ba, w_bs, w_bc, w_out,
                 ssm, g_final, *, tm, n_scan, seg_len, chain, hi, tq, final):
    nb, s, d = x3.shape
    m = nb * s
    x2d = x3.reshape(m, d)
    tr = min(tm, 256)
    hh, hl = _rms_split(x2d, norm_g, tr)
    p2d = _inproj(hh, hl, w_in_t, layer, tm, 0 if hi else PROJ_LO_TILES)
    p3 = p2d.reshape(nb, s, PROJ_W)
    a_out = attend(p3)
    z, xr, xi = _s5(p2d.reshape(n_scan, m // n_scan, PROJ_W), x0r, x0i, *ssm,
                    seg_len=seg_len, chain=chain, hi=hi)
    c_out = _cross(p3, mk, mv, tq)
    y = _merge_out(x2d, a_out.reshape(m, ATT_W), z.reshape(m, SSM_W), c_out.reshape(m, CROSS_W), p2d,
                   w_glu, b_glu, w_ba, w_bs, w_bc, w_out, g_final, tr, hi, final).reshape(nb, s, d)
    ki = p3[:, :, COL_KI:COL_KI + IDX_DIM]
    return y, (p3, ki, xr, xi)


def _stack_heads_kernel(*refs):
    *in_refs, o_ref = refs
    for l, r in enumerate(in_refs):
        for h in range(ATT_HEADS):
            o_ref[l, h] = r[:, h * HEAD_DIM:(h + 1) * HEAD_DIM]


def _stack_heads(p3_layers, col, ts):
    depth = len(p3_layers)
    b, s, _ = p3_layers[0].shape
    out = pl.pallas_call(
        _stack_heads_kernel,
        grid=(b, s // ts),
        in_specs=[pl.BlockSpec((None, ts, ATT_W), lambda bi, i: (bi, i, col // ATT_W))] * depth,
        out_specs=pl.BlockSpec((depth, None, ATT_HEADS, ts, HEAD_DIM), lambda bi, i: (0, bi, 0, i, 0)),
        out_shape=jax.ShapeDtypeStruct((depth, b, ATT_HEADS, s, HEAD_DIM), F32),
        compiler_params=_params("parallel", "parallel"),
        name="stack_heads",
    )(*p3_layers)
    return jnp.transpose(out, (0, 1, 3, 2, 4))


def kernel(x_prompt, x_sample, cache_k, cache_v, cache_kidx, cache_mem_k, cache_mem_v, state_ssm_re, state_ssm_im, page_table, mem_prompt, norm_g, w_in, w_branch_attn, w_branch_ssm, w_branch_cross, w_out, w_mem_kv, ssm_a_re, ssm_a_im, ssm_b_re, ssm_b_im, ssm_c_re, ssm_c_im, ssm_d, ssm_log_dt, w_glu, b_glu, rel_bias, final_norm_g):
    depth = w_in.shape[0]
    b, s, d = x_prompt.shape
    db, t, _ = x_sample.shape
    mem_len = mem_prompt.shape[1]
    n_state = SSM_GROUPS * SSM_STATE
    assert s == SCAN_ROWS * (s // SCAN_ROWS) and db == SCAN_ROWS

    bias_tiles = _bias_tiles(rel_bias)
    w_in_t = jnp.transpose(w_in, (0, 2, 1))
    cache_k_t = jnp.transpose(cache_k, (0, 1, 3, 2, 4))
    cache_v_t = jnp.transpose(cache_v, (0, 1, 3, 2, 4))
    cache_kidx_t = jnp.transpose(cache_kidx, (0, 1, 3, 2))
    zeros_state = jnp.zeros((b, SCAN_ROWS, n_state), F32)
    yp, ys = x_prompt, x_sample
    outs = [[] for _ in range(12)]
    p3_prompt = []
    for l in range(depth):
        lb_re, lb_im, bb_re, bb_im = _ssm_params(ssm_a_re[l], ssm_a_im[l], ssm_log_dt[l],
                                                 ssm_b_re[l], ssm_b_im[l])
        wbu = jnp.concatenate([_block_diag_slabs(bb_re.transpose(1, 0, 2)),
                               _block_diag_slabs(bb_im.transpose(1, 0, 2))], axis=2)
        wyr = _block_diag_slabs(ssm_c_re[l].transpose(0, 2, 1))
        wyi = _block_diag_slabs(ssm_c_im[l].transpose(0, 2, 1))
        ssm = (lb_re.reshape(N_SLABS, 1, SLAB_STATES), lb_im.reshape(N_SLABS, 1, SLAB_STATES),
               wbu, wyr, wyi, ssm_d[l].reshape(N_SLABS, 1, LANES))

        mkv = _memkv(mem_prompt.reshape(b * mem_len, d), w_mem_kv[l])
        mk_p = mkv[:, :CROSS_W].reshape(b, mem_len, CROSS_W)
        mv_p = mkv[:, CROSS_W:].reshape(b, mem_len, CROSS_W)
        attend_p = functools.partial(_dsa_prompt, bias_tiles=bias_tiles, rel_bias=rel_bias)
        yp, (p3_p, kip, xr, xi) = _mixer_layer(
            yp, attend_p, zeros_state, zeros_state, mk_p, mv_p, norm_g[l], w_in_t, l,
            w_glu[l].astype(BF16), b_glu[l], w_branch_attn[l].astype(BF16), w_branch_ssm[l].astype(BF16),
            w_branch_cross[l].astype(BF16), w_out[l].astype(BF16), ssm, final_norm_g,
            tm=2048, n_scan=b, seg_len=s // SCAN_ROWS, chain=True, hi=False, tq=512, final=l == depth - 1)
        p3_prompt.append(p3_p); outs[2].append(kip)
        outs[3].append(mk_p.reshape(b, mem_len, CROSS_HEADS, CROSS_HD))
        outs[4].append(mv_p.reshape(b, mem_len, CROSS_HEADS, CROSS_HD))
        outs[5].append(xr[:, SCAN_ROWS - 1].reshape(b, SSM_GROUPS, SSM_STATE))
        outs[6].append(xi[:, SCAN_ROWS - 1].reshape(b, SSM_GROUPS, SSM_STATE))

        attend_s = functools.partial(_dsa_sample, l, cache_k_t=cache_k_t, cache_v_t=cache_v_t,
                                     cache_kidx_t=cache_kidx_t, page_table=page_table, rel_bias=rel_bias)
        ys, (p3_s, kin, sr, si) = _mixer_layer(
            ys, attend_s,
            state_ssm_re[l].reshape(1, db, n_state), state_ssm_im[l].reshape(1, db, n_state),
            cache_mem_k[l].reshape(db, mem_len, CROSS_W), cache_mem_v[l].reshape(db, mem_len, CROSS_W),
            norm_g[l], w_in_t, l, w_glu[l], b_glu[l], w_branch_attn[l], w_branch_ssm[l],
            w_branch_cross[l], w_out[l], ssm, final_norm_g,
            tm=db * t, n_scan=1, seg_len=t, chain=False, hi=True, tq=t, final=l == depth - 1)
        outs[7].append(p3_s[:, :, COL_K:COL_K + ATT_W].reshape(db, t, ATT_HEADS, HEAD_DIM))
        outs[8].append(p3_s[:, :, COL_V:COL_V + ATT_W].reshape(db, t, ATT_HEADS, HEAD_DIM))
        outs[9].append(kin)
        outs[10].append(sr.reshape(db, SSM_GROUPS, SSM_STATE)); outs[11].append(si.reshape(db, SSM_GROUPS, SSM_STATE))

    stacked = [jnp.stack(o, axis=0) if o else None for o in outs]
    stacked[0] = _stack_heads(p3_prompt, COL_K, 512)
    stacked[1] = _stack_heads(p3_prompt, COL_V, 512)
    return (yp, ys) + tuple(stacked)
```

```python
import functools
import math

import jax
import jax.numpy as jnp
import numpy as np
from jax import lax
from jax.experimental import pallas as pl
from jax.experimental.pallas import tpu as pltpu

F32 = jnp.float32
BF16 = jnp.bfloat16
I32 = jnp.int32
HIGHEST = lax.Precision.HIGHEST
NT_DIMS = (((1,), (1,)), ((), ()))

VMEM_LIMIT_BYTES = 56 * 1024 * 1024
LANES = 128
SUBLANES = 8

D_MODEL = 2048
HEAD_DIM = 128
ATT_HEADS = 6
ATT_W = ATT_HEADS * HEAD_DIM
IDX_HEADS = 16
IDX_DIM = 64
IDX_W = IDX_HEADS * IDX_DIM
TOPK_MAX = 256
QBLOCK = 128
SSM_GROUP = 16
SSM_W = 768
SSM_GROUPS = SSM_W // SSM_GROUP
SSM_STATE = 64
CROSS_HEADS = 4
CROSS_HD = 128
CROSS_W = CROSS_HEADS * CROSS_HD
REL_BUCKETS = 32
REL_MAX_DIST = 128
N_BRANCHES = 3
EPS = 1e-6
SPLIT_WIDTHS = (ATT_W, ATT_W, ATT_W, ATT_W, IDX_W, IDX_HEADS, IDX_DIM,
                SSM_W, SSM_W, CROSS_W, CROSS_W, N_BRANCHES * D_MODEL)
SPLIT_POINTS = tuple(int(c) for c in np.cumsum(SPLIT_WIDTHS)[:-1])

COL_GM = 0
COL_Q = 6144
COL_K = COL_Q + ATT_W
COL_V = COL_K + ATT_W
COL_GA = COL_V + ATT_W
COL_U = COL_GA + ATT_W
COL_GS = COL_U + SSM_W
COL_QC = COL_GS + SSM_W
COL_GC = COL_QC + CROSS_W
COL_KIWI = COL_GC + CROSS_W
COL_WI = COL_KIWI
COL_KI = COL_KIWI + IDX_HEADS
COL_QI = 12288
PROJ_W = COL_QI + IDX_W
PROJ_TN = 512
PROJ_ROW_ALIGN = 16
PROJ_LO_TILES = COL_KIWI // PROJ_TN


def _proj_tile_rows():
    ref_start = dict(zip(("q", "k", "v", "ga", "qi", "wi", "ki", "u", "gs", "qc", "gc", "gm"),
                         (0,) + SPLIT_POINTS))
    runs = ((COL_GM, "gm", N_BRANCHES * D_MODEL), (COL_Q, "q", 4 * ATT_W), (COL_U, "u", 2 * SSM_W),
            (COL_QC, "qc", 2 * CROSS_W), (COL_KIWI, "wi", PROJ_TN), (COL_QI, "qi", IDX_W))
    rows = np.zeros(PROJ_W // PROJ_TN, np.int32)
    for col, name, width in runs:
        for t in range(width // PROJ_TN):
            rows[col // PROJ_TN + t] = ref_start[name] + t * PROJ_TN
    assert not (rows % PROJ_ROW_ALIGN).any()
    return rows

SLAB_GROUPS = LANES // SSM_GROUP
SLAB_STATES = SLAB_GROUPS * SSM_STATE
N_SLABS = SSM_GROUPS // SLAB_GROUPS
SCAN_ROWS = SUBLANES

KEY_EXTENT_CLASSES = 8
COUNT_FOLD_ROWS = 64

PAGES_PER_STEP = 8
IDX_PAGES_PER_STEP = 32
PAGE_BUFFERS = 2

INT_MIN = -2 ** 31
LOWEST_F32_KEY = -0x7F800000


def _params(*sem):
    return pltpu.CompilerParams(dimension_semantics=sem, vmem_limit_bytes=VMEM_LIMIT_BYTES)


def _sigmoid(x):
    return 1.0 / (1.0 + jnp.exp(-x))


def _silu(x):
    return x * _sigmoid(x)


def _gelu_tanh(x):
    return 0.5 * x * (1.0 + jnp.tanh(math.sqrt(2.0 / math.pi) * (x + 0.044715 * (x * x * x))))


def _dot(a, b, hi):
    if hi:
        return jnp.dot(a.astype(F32), b.astype(F32), precision=HIGHEST, preferred_element_type=F32)
    return jnp.dot(a.astype(BF16), b.astype(BF16), preferred_element_type=F32)


def _split_bf16(x):
    hi = x.astype(BF16).astype(F32)
    return hi, x - hi


def _ordered_bits_to_float(key):
    return lax.bitcast_convert_type(key ^ ((key >> 31) & I32(0x7FFFFFFF)), F32)


def _kth_largest(score, k, axis=1):
    def count_ge(cand_bits):
        ones = (score >= _ordered_bits_to_float(cand_bits)).astype(I32)
        if axis == 0:
            while ones.shape[0] > COUNT_FOLD_ROWS and ones.shape[0] % (2 * SUBLANES) == 0:
                half = ones.shape[0] // 2
                ones = ones[:half] + ones[half:]
            return jnp.sum(ones.astype(F32), axis=0, keepdims=True).astype(I32)
        return jnp.sum(ones, axis=axis, keepdims=True)

    zero = jnp.zeros((score.shape[0], 1) if axis == 1 else (1, score.shape[1]), I32)
    t = jnp.where(count_ge(zero) >= k, zero, jnp.full_like(zero, INT_MIN))

    def body(it, t):
        cand = t | (I32(1) << (I32(30) - it))
        return jnp.where(count_ge(cand) >= k, cand, t)

    t = lax.fori_loop(0, 31, body, t)
    return _ordered_bits_to_float(jnp.maximum(t, I32(LOWEST_F32_KEY)))


def _t5_bucket(dist):
    n = jnp.maximum(dist, 0)
    max_exact = REL_BUCKETS // 2
    nf = jnp.maximum(n, 1).astype(F32)
    scale = (REL_BUCKETS - max_exact) / math.log(REL_MAX_DIST / max_exact)
    large = max_exact + jnp.floor(jnp.log(nf * (1.0 / max_exact)) * scale).astype(I32)
    large = jnp.minimum(large, REL_BUCKETS - 1)
    return jnp.where(n < max_exact, n, large)


def _rms_split_kernel(x_ref, g_ref, hh_ref, hl_ref):
    x = x_ref[...]
    y = x * lax.rsqrt(jnp.mean(x * x, axis=-1, keepdims=True) + EPS) * g_ref[...]
    hh = y.astype(BF16)
    hh_ref[...] = hh
    hl_ref[...] = (y - hh.astype(F32)).astype(BF16)


def _rms_split(x2d, g, tm):
    m, d = x2d.shape
    return pl.pallas_call(
        _rms_split_kernel,
        grid=(m // tm,),
        in_specs=[pl.BlockSpec((tm, d), lambda i: (i, 0)), pl.BlockSpec((1, d), lambda i: (0, 0))],
        out_specs=[pl.BlockSpec((tm, d), lambda i: (i, 0))] * 2,
        out_shape=[jax.ShapeDtypeStruct((m, d), BF16)] * 2,
        compiler_params=_params("parallel"),
        name="rms_split",
    )(x2d, g.reshape(1, d))


def _inproj_kernel(rows_ref, hh_ref, hl_ref, wt_ref, o_ref, *, lo_tiles):
    j = pl.program_id(1)
    kiwi_tile = COL_KIWI // PROJ_TN

    def three_pass(w):
        w_hi = w.astype(BF16)
        w_lo = (w - w_hi.astype(F32)).astype(BF16)
        hh = hh_ref[...]
        acc = lax.dot_general(hh, w_hi, NT_DIMS, preferred_element_type=F32)
        acc = acc + lax.dot_general(hl_ref[...], w_hi, NT_DIMS, preferred_element_type=F32)
        return acc + lax.dot_general(hh, w_lo, NT_DIMS, preferred_element_type=F32)

    @pl.when(j < lo_tiles)
    def _():
        o_ref[...] = lax.dot_general(hh_ref[...], wt_ref[...].astype(BF16), NT_DIMS,
                                     preferred_element_type=F32)

    @pl.when((j >= lo_tiles) & (j != kiwi_tile))
    def _():
        o_ref[...] = three_pass(wt_ref[...])

    @pl.when(j == kiwi_tile)
    def _():
        o_ref[:, 0:LANES] = three_pass(wt_ref[0:LANES, :])
        o_ref[:, LANES:PROJ_TN] = jnp.zeros((o_ref.shape[0], PROJ_TN - LANES), F32)


def _inproj(hh, hl, w_in_t, layer, tm, lo_tiles):
    m, d = hh.shape
    n_tiles = PROJ_W // PROJ_TN
    resident = lambda: pl.BlockSpec((tm, d), lambda i, j, rows: (i, 0), pipeline_mode=pl.Buffered(1))
    return pl.pallas_call(
        functools.partial(_inproj_kernel, lo_tiles=lo_tiles),
        grid_spec=pltpu.PrefetchScalarGridSpec(
            num_scalar_prefetch=1,
            grid=(m // tm, n_tiles),
            in_specs=[resident(), resident(),
                      pl.BlockSpec((None, pl.Element(PROJ_TN), pl.Element(d)),
                                   lambda i, j, rows: (layer, rows[j] * PROJ_ROW_ALIGN, 0))],
            out_specs=pl.BlockSpec((tm, PROJ_TN), lambda i, j, rows: (i, j))),
        out_shape=jax.ShapeDtypeStruct((m, PROJ_W), F32),
        compiler_params=_params("parallel", "arbitrary"),
        name="inproj",
    )(jnp.asarray(_proj_tile_rows() // PROJ_ROW_ALIGN), hh, hl, w_in_t)


def _memkv_kernel(a_ref, w_ref, o_ref):
    o_ref[...] = _dot(a_ref[...], w_ref[...], False)


def _memkv(mem2d, w):
    m, d = mem2d.shape
    n = w.shape[1]
    tn = 512
    return pl.pallas_call(
        _memkv_kernel,
        grid=(n // tn,),
        in_specs=[pl.BlockSpec((m, d), lambda j: (0, 0)), pl.BlockSpec((d, tn), lambda j: (0, j))],
        out_specs=pl.BlockSpec((m, tn), lambda j: (0, j)),
        out_shape=jax.ShapeDtypeStruct((m, n), F32),
        compiler_params=_params("parallel"),
        name="memkv",
    )(mem2d, w)


def _bias_tile_kernel(rb_ref, o_ref):
    c = lax.broadcasted_iota(I32, (2 * QBLOCK, QBLOCK), 0)
    r = lax.broadcasted_iota(I32, (2 * QBLOCK, QBLOCK), 1)
    bucket = _t5_bucket(r - c + QBLOCK)
    for h in range(ATT_HEADS):
        tile = jnp.zeros((2 * QBLOCK, QBLOCK), F32)
        for b in range(REL_BUCKETS):
            tile = jnp.where(bucket == b, rb_ref[b, h], tile)
        o_ref[h] = tile


def _bias_tiles(rel_bias):
    return pl.pallas_call(
        _bias_tile_kernel,
        in_specs=[pl.BlockSpec(memory_space=pltpu.SMEM)],
        out_specs=pl.BlockSpec(memory_space=pltpu.VMEM),
        out_shape=jax.ShapeDtypeStruct((ATT_HEADS, 2 * QBLOCK, QBLOCK), F32),
        name="bias_tiles",
    )(rel_bias)


def _dsa_prompt_kernel(rb_ref, q_ref, ga_ref, qi_ref, kwq_ref, k_ref, v_ref, kwk_ref, bt_ref, o_ref,
                       ki4_ref, kb_ref, vt_ref, *, seq, topk):
    i = pl.program_id(1)

    @pl.when(i == 0)
    def _():
        kh, kl = _split_bf16(kwk_ref[:, IDX_HEADS:IDX_HEADS + IDX_DIM])
        ki4_ref[...] = jnp.concatenate([kh, kl, kh, kl], axis=1).astype(BF16)
        kb_ref[...] = k_ref[...].astype(BF16)
        for kb in range(seq // QBLOCK):
            for h in range(ATT_HEADS):
                rows = slice(kb * QBLOCK, (kb + 1) * QBLOCK)
                cols = slice(h * HEAD_DIM, (h + 1) * HEAD_DIM)
                vt_ref[cols, rows] = v_ref[rows, cols].T.astype(BF16)

    n_blk = seq // QBLOCK
    per = n_blk // KEY_EXTENT_CLASSES
    for cls in range(KEY_EXTENT_CLASSES):
        pl.when(i // per == cls)(functools.partial(
            _dsa_prompt_block, i, (cls + 1) * per * QBLOCK, per, topk,
            rb_ref, q_ref, ga_ref, qi_ref, kwq_ref, bt_ref, o_ref, ki4_ref, kb_ref, vt_ref))


def _dsa_prompt_block(i, nk, per, topk, rb_ref, q_ref, ga_ref, qi_ref, kwq_ref, bt_ref, o_ref,
                      ki4_ref, kb_ref, vt_ref):
    qi = qi_ref[...]
    wi_t = kwq_ref[...].T * (IDX_HEADS ** -0.5)
    score = jnp.zeros((nk, QBLOCK), F32)
    for h0 in range(0, IDX_HEADS, 2):
        q4 = []
        for h in (h0, h0 + 1):
            qh, ql = _split_bf16(qi[:, h * IDX_DIM:(h + 1) * IDX_DIM])
            q4.append(jnp.concatenate([qh, qh, ql, ql], axis=1).astype(BF16))
        s = lax.dot_general(ki4_ref[0:nk, :], jnp.concatenate(q4, axis=0), NT_DIMS,
                            preferred_element_type=F32)
        score = score + jnp.maximum(s[:, 0:QBLOCK], 0.0) * wi_t[h0:h0 + 1, :]
        score = score + jnp.maximum(s[:, QBLOCK:2 * QBLOCK], 0.0) * wi_t[h0 + 1:h0 + 2, :]

    key_pos = lax.broadcasted_iota(I32, (nk, QBLOCK), 0)
    q_pos = i * QBLOCK + lax.broadcasted_iota(I32, (nk, QBLOCK), 1)
    score = jnp.where(key_pos <= q_pos, score, -jnp.inf)
    mask_add = jnp.where(score >= _kth_largest(score, topk, axis=0), 0.0, -jnp.inf)

    near = min(nk, (per + 1) * QBLOCK)
    key_blk = key_pos[nk - near:, :] // QBLOCK
    is_diag = key_blk == i
    is_prev = key_blk == i - 1
    reps = near // QBLOCK
    q = q_ref[...] * (HEAD_DIM ** -0.5)
    ga = ga_ref[...]
    for h in range(ATT_HEADS):
        cols = slice(h * HEAD_DIM, (h + 1) * HEAD_DIM)
        far_bias = rb_ref[REL_BUCKETS - 1, h]
        t_prev = jnp.concatenate([bt_ref[h, 0:QBLOCK, :]] * reps, axis=0)
        t_diag = jnp.concatenate([bt_ref[h, QBLOCK:2 * QBLOCK, :]] * reps, axis=0)
        bias = jnp.where(is_diag, t_diag, jnp.where(is_prev, t_prev, far_bias))
        if nk > near:
            bias = jnp.concatenate([jnp.full((nk - near, QBLOCK), far_bias, F32), bias], axis=0)
        logits = lax.dot_general(kb_ref[0:nk, cols], q[:, cols].astype(BF16), NT_DIMS,
                                 preferred_element_type=F32)
        logits = logits + bias + mask_add
        m = jnp.max(logits, axis=0, keepdims=True)
        p = jnp.exp(logits - m)
        l = jnp.sum(p, axis=0, keepdims=True)
        o_t = jnp.dot(vt_ref[cols, 0:nk], p.astype(BF16), preferred_element_type=F32)
        o_ref[:, cols] = (o_t / l).T * _silu(ga[:, cols])


def _dsa_prompt(p3, bias_tiles, rel_bias):
    b, s, _ = p3.shape
    topk = min(TOPK_MAX, s // 4)
    row = lambda col, w: pl.BlockSpec((None, QBLOCK, w), lambda bi, i: (bi, i, col // w))
    full = lambda col, w: pl.BlockSpec((None, s, w), lambda bi, i: (bi, 0, col // w))
    return pl.pallas_call(
        functools.partial(_dsa_prompt_kernel, seq=s, topk=topk),
        grid=(b, s // QBLOCK),
        in_specs=[pl.BlockSpec(memory_space=pltpu.SMEM),
                  row(COL_Q, ATT_W), row(COL_GA, ATT_W), row(COL_QI, IDX_W), row(COL_KIWI, LANES),
                  full(COL_K, ATT_W), full(COL_V, ATT_W), full(COL_KIWI, LANES),
                  pl.BlockSpec((ATT_HEADS, 2 * QBLOCK, QBLOCK), lambda bi, i: (0, 0, 0))],
        out_specs=pl.BlockSpec((None, QBLOCK, ATT_W), lambda bi, i: (bi, i, 0)),
        out_shape=jax.ShapeDtypeStruct((b, s, ATT_W), F32),
        scratch_shapes=[pltpu.VMEM((s, 4 * IDX_DIM), BF16),
                        pltpu.VMEM((s, ATT_W), BF16),
                        pltpu.VMEM((ATT_W, s), BF16)],
        compiler_params=_params("parallel", "arbitrary"),
        name="dsa_prompt",
    )(rel_bias, p3, p3, p3, p3, p3, p3, p3, bias_tiles)


def _sidx_kernel(pt_ref, q_ref, w_ref, *rest, nch):
    k_refs = rest[:IDX_PAGES_PER_STEP]
    kn_ref, s_ref, q4_ref = rest[IDX_PAGES_PER_STEP:]
    c = pl.program_id(1)

    @pl.when(c == 0)
    def _():
        qh, ql = _split_bf16(q_ref[...])
        q4_ref[...] = jnp.concatenate([qh, qh, ql, ql], axis=1).astype(BF16)

    def scores(ki_t):
        n = ki_t.shape[1]
        kh, kl = _split_bf16(ki_t)
        ki4_t = jnp.concatenate([kh, kl, kh, kl], axis=0).astype(BF16)
        s = jnp.dot(q4_ref[...], ki4_t, preferred_element_type=F32)
        s = jnp.maximum(s, 0.0) * (w_ref[...] * (IDX_HEADS ** -0.5))
        return jnp.sum(s.reshape(IDX_HEADS, SUBLANES, n), axis=0)

    @pl.when(c < nch)
    def _():
        s_ref[...] = scores(jnp.concatenate([r[...] for r in k_refs], axis=1))

    @pl.when(c == nch)
    def _():
        n = kn_ref.shape[1]
        sn = scores(kn_ref[...])
        t_key = lax.broadcasted_iota(I32, (SUBLANES, n), 1)
        t_q = lax.broadcasted_iota(I32, (SUBLANES, n), 0)
        s_ref[...] = jnp.full(s_ref.shape, -jnp.inf, F32)
        s_ref[:, 0:n] = jnp.where(t_key <= t_q, sn, -jnp.inf)


def _sattn_kernel(pt_ref, rb_ref, sf_ref, sc_ref, q_ref, ga_ref, *rest, nch, past, topk, page):
    k_refs = rest[:PAGES_PER_STEP]
    v_refs = rest[PAGES_PER_STEP:2 * PAGES_PER_STEP]
    kn_ref, vn_ref, o_ref, thr_ref, m_ref, l_ref, acc_ref = rest[2 * PAGES_PER_STEP:]
    c = pl.program_id(1)
    chunk = PAGES_PER_STEP * page

    @pl.when(c == 0)
    def _():
        thr_ref[...] = _kth_largest(sf_ref[...], topk)
        m_ref[...] = jnp.full(m_ref.shape, -jnp.inf, F32)
        l_ref[...] = jnp.zeros(l_ref.shape, F32)
        acc_ref[...] = jnp.zeros(acc_ref.shape, F32)

    def attend(k_of, v_of, sc, key_pos0):
        n = sc.shape[1]
        mask_add = jnp.where(sc >= thr_ref[...], 0.0, -jnp.inf)
        q_pos = past + lax.broadcasted_iota(I32, (SUBLANES, n), 0)
        key_pos = key_pos0 + lax.broadcasted_iota(I32, (SUBLANES, n), 1)
        bucket = _t5_bucket(q_pos - key_pos)
        q = q_ref[...]
        rows = []
        for h in range(ATT_HEADS):
            cols = slice(h * HEAD_DIM, (h + 1) * HEAD_DIM)
            bias = jnp.zeros((SUBLANES, n), F32)
            for b in range(REL_BUCKETS):
                bias = jnp.where(bucket == b, rb_ref[b, h], bias)
            lg = lax.dot_general(q[:, cols].astype(BF16), k_of(h).astype(BF16), NT_DIMS,
                                 preferred_element_type=F32)
            rows.append(lg * (HEAD_DIM ** -0.5) + bias + mask_add)
        logits = jnp.concatenate(rows, axis=0)
        m_old = m_ref[...]
        m_new = jnp.maximum(m_old, jnp.max(logits, axis=1, keepdims=True))
        m_safe = jnp.where(m_new == -jnp.inf, 0.0, m_new)
        alpha = jnp.exp(m_old - m_safe)
        p = jnp.exp(logits - m_safe[:, 0:1])
        l_ref[...] = alpha * l_ref[...] + jnp.sum(p, axis=1, keepdims=True)
        m_ref[...] = m_new
        for h in range(ATT_HEADS):
            cols = slice(h * HEAD_DIM, (h + 1) * HEAD_DIM)
            rs = slice(h * SUBLANES, (h + 1) * SUBLANES)
            acc_ref[:, cols] = alpha[rs] * acc_ref[:, cols] + jnp.dot(
                p[rs].astype(BF16), v_of(h).astype(BF16), preferred_element_type=F32)

    @pl.when(c < nch)
    def _():
        k_of = lambda h: jnp.concatenate([r[h] for r in k_refs], axis=0)
        v_of = lambda h: jnp.concatenate([r[h] for r in v_refs], axis=0)
        attend(k_of, v_of, sc_ref[...], c * chunk)

    @pl.when(c == nch)
    def _():
        n = kn_ref.shape[0]
        head = lambda ref: (lambda h: ref[:, h * HEAD_DIM:(h + 1) * HEAD_DIM])
        attend(head(kn_ref), head(vn_ref), sc_ref[:, 0:n], past)
        ga = ga_ref[...]
        l = l_ref[...]
        for h in range(ATT_HEADS):
            cols = slice(h * HEAD_DIM, (h + 1) * HEAD_DIM)
            o_ref[:, cols] = acc_ref[:, cols] / l[h * SUBLANES:(h + 1) * SUBLANES] * _silu(ga[:, cols])


def _dsa_sample(layer, ps3, cache_k_t, cache_v_t, cache_kidx_t, page_table, rel_bias):
    db, t, _ = ps3.shape
    n_pages = page_table.shape[1]
    page = cache_kidx_t.shape[3]
    past = n_pages * page
    topk = min(TOPK_MAX, (past + t) // 4)
    nch = n_pages // PAGES_PER_STEP
    chunk = PAGES_PER_STEP * page
    nch_idx = n_pages // IDX_PAGES_PER_STEP
    chunk_idx = IDX_PAGES_PER_STEP * page
    ntot = (nch_idx + 1) * chunk_idx
    assert chunk_idx % chunk == 0 and t == SUBLANES

    qi = ps3[:, :, COL_QI:COL_QI + IDX_W].reshape(db, t, IDX_HEADS, IDX_DIM)
    qi_hq = qi.transpose(0, 2, 1, 3).reshape(db, IDX_HEADS * t, IDX_DIM)
    wi_hq = ps3[:, :, COL_WI:COL_WI + IDX_HEADS].transpose(0, 2, 1).reshape(db, IDX_HEADS * t, 1)
    pad_rows = lambda a: jnp.pad(a, ((0, 0), (0, page - t), (0, 0)))
    ki_new_t = pad_rows(ps3[:, :, COL_KI:COL_KI + IDX_DIM]).transpose(0, 2, 1)
    k_new = pad_rows(ps3[:, :, COL_K:COL_K + ATT_W])
    v_new = pad_rows(ps3[:, :, COL_V:COL_V + ATT_W])
    q = ps3[:, :, COL_Q:COL_Q + ATT_W]
    ga = ps3[:, :, COL_GA:COL_GA + ATT_W]

    def page_spec(r, per_step, dims, **kw):
        def imap(b, c, pt):
            return (layer, pt[b, jnp.minimum(c * per_step + r, n_pages - 1)]) + (0,) * len(dims)
        return pl.BlockSpec((None, None) + dims, imap, **kw)

    per_b = lambda rows, w: pl.BlockSpec((None, rows, w), lambda b, c, pt: (b, 0, 0))

    scores = pl.pallas_call(
        functools.partial(_sidx_kernel, nch=nch_idx),
        grid_spec=pltpu.PrefetchScalarGridSpec(
            num_scalar_prefetch=1,
            grid=(db, nch_idx + 1),
            in_specs=[per_b(IDX_HEADS * t, IDX_DIM), per_b(IDX_HEADS * t, 1)]
                     + [page_spec(r, IDX_PAGES_PER_STEP, (IDX_DIM, page)) for r in range(IDX_PAGES_PER_STEP)]
                     + [per_b(IDX_DIM, page)],
            out_specs=pl.BlockSpec((None, t, chunk_idx), lambda b, c, pt: (b, 0, c)),
            scratch_shapes=[pltpu.VMEM((IDX_HEADS * t, 4 * IDX_DIM), BF16)]),
        out_shape=jax.ShapeDtypeStruct((db, t, ntot), F32),
        compiler_params=_params("parallel", "arbitrary"),
        name="dsa_sample_scores",
    )(page_table, qi_hq, wi_hq, *([cache_kidx_t] * IDX_PAGES_PER_STEP), ki_new_t)

    return pl.pallas_call(
        functools.partial(_sattn_kernel, nch=nch, past=past, topk=topk, page=page),
        grid_spec=pltpu.PrefetchScalarGridSpec(
            num_scalar_prefetch=1,
            grid=(db, nch + 1),
            in_specs=[pl.BlockSpec(memory_space=pltpu.SMEM),
                      per_b(t, ntot),
                      pl.BlockSpec((None, t, chunk), lambda b, c, pt: (b, 0, c)),
                      per_b(t, ATT_W), per_b(t, ATT_W)]
                     + [page_spec(r, PAGES_PER_STEP, (ATT_HEADS, page, HEAD_DIM),
                                  pipeline_mode=pl.Buffered(PAGE_BUFFERS)) for r in range(PAGES_PER_STEP)] * 2
                     + [per_b(page, ATT_W), per_b(page, ATT_W)],
            out_specs=per_b(t, ATT_W),
            scratch_shapes=[pltpu.VMEM((t, 1), F32),
                            pltpu.VMEM((ATT_HEADS * t, LANES), F32),
                            pltpu.VMEM((ATT_HEADS * t, LANES), F32),
                            pltpu.VMEM((t, ATT_W), F32)]),
        out_shape=jax.ShapeDtypeStruct((db, t, ATT_W), F32),
        compiler_params=_params("parallel", "arbitrary"),
        name="dsa_sample_attn",
    )(page_table, rel_bias, scores, scores, q, ga,
      *([cache_k_t] * PAGES_PER_STEP), *([cache_v_t] * PAGES_PER_STEP), k_new, v_new)


def _ssm_param_kernel(are_ref, aim_ref, ldt_ref, bre_ref, bim_ref, lbr_ref, lbi_ref, bbr_ref, bbi_ref):
    lam_re = jnp.minimum(are_ref[...], -1e-4)
    lam_im = aim_ref[...]
    step = jnp.exp(ldt_ref[...])
    mag = jnp.exp(lam_re * step)
    ang = lam_im * step
    lb_re = mag * jnp.cos(ang)
    lb_im = mag * jnp.sin(ang)
    nr = lb_re - 1.0
    ni = lb_im
    den = lam_re * lam_re + lam_im * lam_im
    coef_re = (nr * lam_re + ni * lam_im) / den
    coef_im = (ni * lam_re - nr * lam_im) / den
    lbr_ref[...] = lb_re
    lbi_ref[...] = lb_im
    for c in range(SSM_GROUP):
        br = bre_ref[c]
        bi = bim_ref[c]
        bbr_ref[c] = coef_re * br - coef_im * bi
        bbi_ref[c] = coef_re * bi + coef_im * br


def _ssm_params(a_re, a_im, log_dt, b_re, b_im):
    g, p, c = b_re.shape
    vm = pl.BlockSpec(memory_space=pltpu.VMEM)
    return pl.pallas_call(
        _ssm_param_kernel,
        in_specs=[vm] * 5,
        out_specs=[vm] * 4,
        out_shape=[jax.ShapeDtypeStruct((g, p), F32)] * 2 + [jax.ShapeDtypeStruct((c, g, p), F32)] * 2,
        name="ssm_params",
    )(a_re, a_im, log_dt.reshape(g, 1), b_re.transpose(2, 0, 1), b_im.transpose(2, 0, 1))


def _block_diag_slabs(m):
    g, a, b = m.shape
    eye = jnp.eye(SLAB_GROUPS, dtype=m.dtype)
    m4 = m.reshape(N_SLABS, SLAB_GROUPS, a, b)
    return jnp.einsum('sgab,gh->sgahb', m4, eye).reshape(N_SLABS, SLAB_GROUPS * a, SLAB_GROUPS * b)


def _scan_pitch(seg_len):
    return seg_len + 4 if seg_len % 8 == 0 else seg_len


def _s5_kernel(u_ref, x0r_ref, x0i_ref, lbr_ref, lbi_ref, wbu_ref, wyr_ref, wyi_ref, d_ref,
               z_ref, xr_ref, xi_ref, up_ref, zs_ref, *, seg_len, chain, hi):
    nc = SLAB_STATES // LANES
    pitch = _scan_pitch(seg_len)
    up_ref[...] = jnp.zeros(up_ref.shape, F32)
    for s in range(SCAN_ROWS):
        up_ref[pl.ds(s * pitch, seg_len), :] = u_ref[pl.ds(s * seg_len, seg_len), :]
    u = up_ref[...]
    bu = _dot(u, wbu_ref[...], hi)
    for j in range(2 * nc):
        zs_ref[j] = bu[:, j * LANES:(j + 1) * LANES]
    shape = (SCAN_ROWS, SLAB_STATES)
    lbr = jnp.broadcast_to(lbr_ref[...], shape)
    lbi = jnp.broadcast_to(lbi_ref[...], shape)

    def load(t, base):
        rows = pl.ds(t, SCAN_ROWS, stride=pitch)
        return jnp.concatenate([zs_ref[base + j, rows, :] for j in range(nc)], axis=1)

    def store(t, base, val):
        rows = pl.ds(t, SCAN_ROWS, stride=pitch)
        for j in range(nc):
            zs_ref[base + j, rows, :] = val[:, j * LANES:(j + 1) * LANES]

    def advance(t, carry):
        xr, xi = carry
        nr = lbr * xr - lbi * xi + load(t, 0)
        ni = lbr * xi + lbi * xr + load(t, nc)
        return nr, ni

    def advance_store(t, carry):
        nr, ni = advance(t, carry)
        store(t, 0, nr)
        store(t, nc, ni)
        return nr, ni

    unroll = min(seg_len, 8)
    x0r = x0r_ref[...]
    x0i = x0i_ref[...]
    if chain:
        zero = jnp.zeros(shape, F32)
        er, ei = lax.fori_loop(0, seg_len, advance, (zero, zero), unroll=unroll)
        pr, pi = lbr, lbi
        for _ in range(int(math.log2(seg_len))):
            pr, pi = pr * pr - pi * pi, 2.0 * pr * pi
        row = lax.broadcasted_iota(I32, shape, 0)
        sr = jnp.where(row == 0, x0r, 0.0)
        si = jnp.where(row == 0, x0i, 0.0)
        for s in range(SCAN_ROWS - 1):
            nr = er + pr * sr - pi * si
            ni = ei + pr * si + pi * sr
            sr = jnp.where(row == s + 1, pltpu.roll(nr, 1, axis=0), sr)
            si = jnp.where(row == s + 1, pltpu.roll(ni, 1, axis=0), si)
        x0r, x0i = sr, si
    xr, xi = lax.fori_loop(0, seg_len, advance_store, (x0r, x0i), unroll=unroll)
    xr_ref[...] = xr
    xi_ref[...] = xi
    y = d_ref[...] * u
    for j in range(nc):
        blk = slice(j * LANES, (j + 1) * LANES)
        y = y + _dot(zs_ref[j], wyr_ref[blk, :], hi) - _dot(zs_ref[nc + j], wyi_ref[blk, :], hi)
    up_ref[...] = _gelu_tanh(y)
    for s in range(SCAN_ROWS):
        z_ref[pl.ds(s * seg_len, seg_len), :] = up_ref[pl.ds(s * pitch, seg_len), :]


def _s5(p3, x0r, x0i, lbr, lbi, wbu, wyr, wyi, d, *, seg_len, chain, hi):
    nb, r, _ = p3.shape
    assert r == SCAN_ROWS * seg_len and (not chain or seg_len & (seg_len - 1) == 0)
    rp = SCAN_ROWS * _scan_pitch(seg_len)
    slab = lambda rows, w: pl.BlockSpec((None, rows, w), lambda b, s: (s, 0, 0))
    st = pl.BlockSpec((None, SCAN_ROWS, SLAB_STATES), lambda b, s: (b, 0, s))
    return pl.pallas_call(
        functools.partial(_s5_kernel, seg_len=seg_len, chain=chain, hi=hi),
        grid=(nb, N_SLABS),
        in_specs=[pl.BlockSpec((None, r, LANES), lambda b, s: (b, 0, COL_U // LANES + s)),
                  st, st, slab(1, SLAB_STATES), slab(1, SLAB_STATES),
                  slab(LANES, 2 * SLAB_STATES), slab(SLAB_STATES, LANES), slab(SLAB_STATES, LANES),
                  slab(1, LANES)],
        out_specs=[pl.BlockSpec((None, r, LANES), lambda b, s: (b, 0, s)), st, st],
        out_shape=[jax.ShapeDtypeStruct((nb, r, SSM_W), F32),
                   jax.ShapeDtypeStruct((nb, SCAN_ROWS, SSM_GROUPS * SSM_STATE), F32),
                   jax.ShapeDtypeStruct((nb, SCAN_ROWS, SSM_GROUPS * SSM_STATE), F32)],
        scratch_shapes=[pltpu.VMEM((rp, LANES), F32),
                        pltpu.VMEM((2 * SLAB_STATES // LANES, rp, LANES), F32)],
        compiler_params=_params("parallel", "parallel"),
        name="s5_scan",
    )(p3, x0r, x0i, lbr, lbi, wbu, wyr, wyi, d)


def _cross_kernel(q_ref, g_ref, mk_ref, mv_ref, o_ref):
    q = q_ref[...]
    g = g_ref[...]
    for h in range(CROSS_HEADS):
        cols = slice(h * CROSS_HD, (h + 1) * CROSS_HD)
        logits = lax.dot_general(q[:, cols].astype(BF16), mk_ref[:, cols].astype(BF16), NT_DIMS,
                                 preferred_element_type=F32) * (CROSS_HD ** -0.5)
        m = jnp.max(logits, axis=1, keepdims=True)
        p = jnp.exp(logits - m)
        l = jnp.sum(p, axis=1, keepdims=True)
        o = jnp.dot(p.astype(BF16), mv_ref[:, cols].astype(BF16), preferred_element_type=F32)
        o_ref[:, cols] = o / l * _silu(g[:, cols])


def _cross(p3, mk, mv, tq):
    nb, s, _ = p3.shape
    mem = mk.shape[1]
    qspec = lambda col: pl.BlockSpec((None, tq, CROSS_W), lambda b, i: (b, i, col // CROSS_W))
    mspec = pl.BlockSpec((None, mem, CROSS_W), lambda b, i: (b, 0, 0))
    return pl.pallas_call(
        _cross_kernel,
        grid=(nb, s // tq),
        in_specs=[qspec(COL_QC), qspec(COL_GC), mspec, mspec],
        out_specs=pl.BlockSpec((None, tq, CROSS_W), lambda b, i: (b, i, 0)),
        out_shape=jax.ShapeDtypeStruct((nb, s, CROSS_W), F32),
        compiler_params=_params("parallel", "parallel"),
        name="cross_attn",
    )(p3, p3, mk, mv)


def _merge_out_kernel(x_ref, a_ref, z_ref, c_ref, gs_ref, gm0_ref, gm1_ref, gm2_ref,
                      wglu_ref, bglu_ref, wba_ref, wbs_ref, wbc_ref, wout_ref, gfin_ref, o_ref, *, hi, final):
    gl = _dot(z_ref[...], wglu_ref[...], hi) + bglu_ref[...]
    s_out = gl[:, 0:SSM_W] * _sigmoid(gl[:, SSM_W:2 * SSM_W]) * _silu(gs_ref[...])
    merged = _sigmoid(gm0_ref[...]) * _dot(a_ref[...], wba_ref[...], hi)
    merged = merged + _sigmoid(gm1_ref[...]) * _dot(s_out, wbs_ref[...], hi)
    merged = merged + _sigmoid(gm2_ref[...]) * _dot(c_ref[...], wbc_ref[...], hi)
    y = x_ref[...] + _dot(merged, wout_ref[...], hi)
    if final:
        y = y * lax.rsqrt(jnp.mean(y * y, axis=-1, keepdims=True) + EPS) * gfin_ref[...]
    o_ref[...] = y


def _merge_out(x2d, a2d, z2d, c2d, p2d, w_glu, b_glu, w_ba, w_bs, w_bc, w_out, g_final, tm, hi, final):
    m, d = x2d.shape
    rows = lambda w, col=0: pl.BlockSpec((tm, w), lambda i: (i, col // w))
    const = lambda arr: pl.BlockSpec(arr.shape, lambda i: (0, 0), pipeline_mode=pl.Buffered(1))
    b2 = b_glu.reshape(1, -1)
    g2 = g_final.reshape(1, d)
    return pl.pallas_call(
        functools.partial(_merge_out_kernel, hi=hi, final=final),
        grid=(m // tm,),
        in_specs=[rows(d), rows(ATT_W), rows(SSM_W), rows(CROSS_W), rows(SSM_W, COL_GS),
                  rows(D_MODEL, COL_GM), rows(D_MODEL, COL_GM + D_MODEL), rows(D_MODEL, COL_GM + 2 * D_MODEL),
                  const(w_glu), const(b2), const(w_ba), const(w_bs), const(w_bc), const(w_out), const(g2)],
        out_specs=rows(d),
        out_shape=jax.ShapeDtypeStruct((m, d), F32),
        compiler_params=_params("parallel"),
        name="merge_out",
    )(x2d, a2d, z2d, c2d, p2d, p2d, p2d, p2d, w_glu, b2, w_ba, w_bs, w_bc, w_out, g2)


def _mixer_layer(x3, attend, x0r, x0i, mk, mv, norm_g, w_in_t, layer, w_glu, b_glu, w_ba, w_bs, w_bc, w_out,
                 ssm, g_final, *, tm, n_scan, seg_len, chain, hi, tq, final):
    nb, s, d = x3.shape
    m = nb * s
    x2d = x3.reshape(m, d)
    tr = min(tm, 256)
    hh, hl = _rms_split(x2d, norm_g, tr)
    p2d = _inproj(hh, hl, w_in_t, layer, tm, 0 if hi else PROJ_LO_TILES)
    p3 = p2d.reshape(nb, s, PROJ_W)
    a_out = attend(p3)
    z, xr, xi = _s5(p2d.reshape(n_scan, m // n_scan, PROJ_W), x0r, x0i, *ssm,
                    seg_len=seg_len, chain=chain, hi=hi)
    c_out = _cross(p3, mk, mv, tq)
    y = _merge_out(x2d, a_out.reshape(m, ATT_W), z.reshape(m, SSM_W), c_out.reshape(m, CROSS_W), p2d,
                   w_glu, b_glu, w_ba, w_bs, w_bc, w_out, g_final, tr, hi, final).reshape(nb, s, d)
    ki = p3[:, :, COL_KI:COL_KI + IDX_DIM]
    return y, (p3, ki, xr, xi)


def _stack_heads_kernel(*refs):
    *in_refs, o_ref = refs
    for l, r in enumerate(in_refs):
        for h in range(ATT_HEADS):
            o_ref[l, h] = r[:, h * HEAD_DIM:(h + 1) * HEAD_DIM]


def _stack_heads(p3_layers, col, ts):
    depth = len(p3_layers)
    b, s, _ = p3_layers[0].shape
    out = pl.pallas_call(
        _stack_heads_kernel,
        grid=(b, s // ts),
        in_specs=[pl.BlockSpec((None, ts, ATT_W), lambda bi, i: (bi, i, col // ATT_W))] * depth,
        out_specs=pl.BlockSpec((depth, None, ATT_HEADS, ts, HEAD_DIM), lambda bi, i: (0, bi, 0, i, 0)),
        out_shape=jax.ShapeDtypeStruct((depth, b, ATT_HEADS, s, HEAD_DIM), F32),
        compiler_params=_params("parallel", "parallel"),
        name="stack_heads",
    )(*p3_layers)
    return jnp.transpose(out, (0, 1, 3, 2, 4))


def kernel(x_prompt, x_sample, cache_k, cache_v, cache_kidx, cache_mem_k, cache_mem_v, state_ssm_re, state_ssm_im, page_table, mem_prompt, norm_g, w_in, w_branch_attn, w_branch_ssm, w_branch_cross, w_out, w_mem_kv, ssm_a_re, ssm_a_im, ssm_b_re, ssm_b_im, ssm_c_re, ssm_c_im, ssm_d, ssm_log_dt, w_glu, b_glu, rel_bias, final_norm_g):
    depth = w_in.shape[0]
    b, s, d = x_prompt.shape
    db, t, _ = x_sample.shape
    mem_len = mem_prompt.shape[1]
    n_state = SSM_GROUPS * SSM_STATE
    assert s == SCAN_ROWS * (s // SCAN_ROWS) and db == SCAN_ROWS

    bias_tiles = _bias_tiles(rel_bias)
    w_in_t = jnp.transpose(w_in, (0, 2, 1))
    cache_k_t = jnp.transpose(cache_k, (0, 1, 3, 2, 4))
    cache_v_t = jnp.transpose(cache_v, (0, 1, 3, 2, 4))
    cache_kidx_t = jnp.transpose(cache_kidx, (0, 1, 3, 2))
    zeros_state = jnp.zeros((b, SCAN_ROWS, n_state), F32)
    yp, ys = x_prompt, x_sample
    outs = [[] for _ in range(12)]
    p3_prompt = []
    for l in range(depth):
        lb_re, lb_im, bb_re, bb_im = _ssm_params(ssm_a_re[l], ssm_a_im[l], ssm_log_dt[l],
                                                 ssm_b_re[l], ssm_b_im[l])
        wbu = jnp.concatenate([_block_diag_slabs(bb_re.transpose(1, 0, 2)),
                               _block_diag_slabs(bb_im.transpose(1, 0, 2))], axis=2)
        wyr = _block_diag_slabs(ssm_c_re[l].transpose(0, 2, 1))
        wyi = _block_diag_slabs(ssm_c_im[l].transpose(0, 2, 1))
        ssm = (lb_re.reshape(N_SLABS, 1, SLAB_STATES), lb_im.reshape(N_SLABS, 1, SLAB_STATES),
               wbu, wyr, wyi, ssm_d[l].reshape(N_SLABS, 1, LANES))

        mkv = _memkv(mem_prompt.reshape(b * mem_len, d), w_mem_kv[l])
        mk_p = mkv[:, :CROSS_W].reshape(b, mem_len, CROSS_W)
        mv_p = mkv[:, CROSS_W:].reshape(b, mem_len, CROSS_W)
        attend_p = functools.partial(_dsa_prompt, bias_tiles=bias_tiles, rel_bias=rel_bias)
        yp, (p3_p, kip, xr, xi) = _mixer_layer(
            yp, attend_p, zeros_state, zeros_state, mk_p, mv_p, norm_g[l], w_in_t, l,
            w_glu[l].astype(BF16), b_glu[l], w_branch_attn[l].astype(BF16), w_branch_ssm[l].astype(BF16),
            w_branch_cross[l].astype(BF16), w_out[l].astype(BF16), ssm, final_norm_g,
            tm=2048, n_scan=b, seg_len=s // SCAN_ROWS, chain=True, hi=False, tq=512, final=l == depth - 1)
        p3_prompt.append(p3_p); outs[2].append(kip)
        outs[3].append(mk_p.reshape(b, mem_len, CROSS_HEADS, CROSS_HD))
        outs[4].append(mv_p.reshape(b, mem_len, CROSS_HEADS, CROSS_HD))
        outs[5].append(xr[:, SCAN_ROWS - 1].reshape(b, SSM_GROUPS, SSM_STATE))
        outs[6].append(xi[:, SCAN_ROWS - 1].reshape(b, SSM_GROUPS, SSM_STATE))

        attend_s = functools.partial(_dsa_sample, l, cache_k_t=cache_k_t, cache_v_t=cache_v_t,
                                     cache_kidx_t=cache_kidx_t, page_table=page_table, rel_bias=rel_bias)
        ys, (p3_s, kin, sr, si) = _mixer_layer(
            ys, attend_s,
            state_ssm_re[l].reshape(1, db, n_state), state_ssm_im[l].reshape(1, db, n_state),
            cache_mem_k[l].reshape(db, mem_len, CROSS_W), cache_mem_v[l].reshape(db, mem_len, CROSS_W),
            norm_g[l], w_in_t, l, w_glu[l], b_glu[l], w_branch_attn[l], w_branch_ssm[l],
            w_branch_cross[l], w_out[l], ssm, final_norm_g,
            tm=db * t, n_scan=1, seg_len=t, chain=False, hi=True, tq=t, final=l == depth - 1)
        outs[7].append(p3_s[:, :, COL_K:COL_K + ATT_W].reshape(db, t, ATT_HEADS, HEAD_DIM))
        outs[8].append(p3_s[:, :, COL_V:COL_V + ATT_W].reshape(db, t, ATT_HEADS, HEAD_DIM))
        outs[9].append(kin)
        outs[10].append(sr.reshape(db, SSM_GROUPS, SSM_STATE)); outs[11].append(si.reshape(db, SSM_GROUPS, SSM_STATE))

    stacked = [jnp.stack(o, axis=0) if o else None for o in outs]
    stacked[0] = _stack_heads(p3_prompt, COL_K, 512)
    stacked[1] = _stack_heads(p3_prompt, COL_V, 512)
    return (yp, ys) + tuple(stacked)
```

```python
import functools
import math

import jax
import jax.numpy as jnp
import numpy as np
from jax import lax
from jax.experimental import pallas as pl
from jax.experimental.pallas import tpu as pltpu

F32 = jnp.float32
BF16 = jnp.bfloat16
I32 = jnp.int32
HIGHEST = lax.Precision.HIGHEST
NT_DIMS = (((1,), (1,)), ((), ()))

VMEM_LIMIT_BYTES = 56 * 1024 * 1024
LANES = 128
SUBLANES = 8

D_MODEL = 2048
HEAD_DIM = 128
ATT_HEADS = 6
ATT_W = ATT_HEADS * HEAD_DIM
IDX_HEADS = 16
IDX_DIM = 64
IDX_W = IDX_HEADS * IDX_DIM
TOPK_MAX = 256
QBLOCK = 128
SSM_GROUP = 16
SSM_W = 768
SSM_GROUPS = SSM_W // SSM_GROUP
SSM_STATE = 64
CROSS_HEADS = 4
CROSS_HD = 128
CROSS_W = CROSS_HEADS * CROSS_HD
REL_BUCKETS = 32
REL_MAX_DIST = 128
N_BRANCHES = 3
EPS = 1e-6
SPLIT_WIDTHS = (ATT_W, ATT_W, ATT_W, ATT_W, IDX_W, IDX_HEADS, IDX_DIM,
                SSM_W, SSM_W, CROSS_W, CROSS_W, N_BRANCHES * D_MODEL)
SPLIT_POINTS = tuple(int(c) for c in np.cumsum(SPLIT_WIDTHS)[:-1])

COL_GM = 0
COL_Q = 6144
COL_K = COL_Q + ATT_W
COL_V = COL_K + ATT_W
COL_GA = COL_V + ATT_W
COL_U = COL_GA + ATT_W
COL_GS = COL_U + SSM_W
COL_QC = COL_GS + SSM_W
COL_GC = COL_QC + CROSS_W
COL_KIWI = COL_GC + CROSS_W
COL_WI = COL_KIWI
COL_KI = COL_KIWI + IDX_HEADS
COL_QI = 12288
PROJ_W = COL_QI + IDX_W
PROJ_TN = 512
PROJ_ROW_ALIGN = 16
PROJ_LO_TILES = COL_KIWI // PROJ_TN


def _proj_tile_rows():
    ref_start = dict(zip(("q", "k", "v", "ga", "qi", "wi", "ki", "u", "gs", "qc", "gc", "gm"),
                         (0,) + SPLIT_POINTS))
    runs = ((COL_GM, "gm", N_BRANCHES * D_MODEL), (COL_Q, "q", 4 * ATT_W), (COL_U, "u", 2 * SSM_W),
            (COL_QC, "qc", 2 * CROSS_W), (COL_KIWI, "wi", PROJ_TN), (COL_QI, "qi", IDX_W))
    rows = np.zeros(PROJ_W // PROJ_TN, np.int32)
    for col, name, width in runs:
        for t in range(width // PROJ_TN):
            rows[col // PROJ_TN + t] = ref_start[name] + t * PROJ_TN
    assert not (rows % PROJ_ROW_ALIGN).any()
    return rows

SLAB_GROUPS = LANES // SSM_GROUP
SLAB_STATES = SLAB_GROUPS * SSM_STATE
N_SLABS = SSM_GROUPS // SLAB_GROUPS
SCAN_ROWS = SUBLANES

KEY_EXTENT_CLASSES = 8
COUNT_FOLD_ROWS = 64

PAGES_PER_STEP = 16
IDX_PAGES_PER_STEP = 64
PAGE_BUFFERS = 2

INT_MIN = -2 ** 31
LOWEST_F32_KEY = -0x7F800000


def _params(*sem):
    return pltpu.CompilerParams(dimension_semantics=sem, vmem_limit_bytes=VMEM_LIMIT_BYTES)


def _sigmoid(x):
    return 1.0 / (1.0 + jnp.exp(-x))


def _silu(x):
    return x * _sigmoid(x)


def _gelu_tanh(x):
    return 0.5 * x * (1.0 + jnp.tanh(math.sqrt(2.0 / math.pi) * (x + 0.044715 * (x * x * x))))


def _dot(a, b, hi):
    if hi:
        return jnp.dot(a.astype(F32), b.astype(F32), precision=HIGHEST, preferred_element_type=F32)
    return jnp.dot(a.astype(BF16), b.astype(BF16), preferred_element_type=F32)


def _split_bf16(x):
    hi = x.astype(BF16).astype(F32)
    return hi, x - hi


def _ordered_bits_to_float(key):
    return lax.bitcast_convert_type(key ^ ((key >> 31) & I32(0x7FFFFFFF)), F32)


def _kth_largest(score, k, axis=1):
    def count_ge(cand_bits):
        ones = (score >= _ordered_bits_to_float(cand_bits)).astype(I32)
        if axis == 0:
            while ones.shape[0] > COUNT_FOLD_ROWS and ones.shape[0] % (2 * SUBLANES) == 0:
                half = ones.shape[0] // 2
                ones = ones[:half] + ones[half:]
            return jnp.sum(ones.astype(F32), axis=0, keepdims=True).astype(I32)
        return jnp.sum(ones, axis=axis, keepdims=True)

    zero = jnp.zeros((score.shape[0], 1) if axis == 1 else (1, score.shape[1]), I32)
    t = jnp.where(count_ge(zero) >= k, zero, jnp.full_like(zero, INT_MIN))

    def body(it, t):
        cand = t | (I32(1) << (I32(30) - it))
        return jnp.where(count_ge(cand) >= k, cand, t)

    t = lax.fori_loop(0, 31, body, t)
    return _ordered_bits_to_float(jnp.maximum(t, I32(LOWEST_F32_KEY)))


def _t5_bucket(dist):
    n = jnp.maximum(dist, 0)
    max_exact = REL_BUCKETS // 2
    nf = jnp.maximum(n, 1).astype(F32)
    scale = (REL_BUCKETS - max_exact) / math.log(REL_MAX_DIST / max_exact)
    large = max_exact + jnp.floor(jnp.log(nf * (1.0 / max_exact)) * scale).astype(I32)
    large = jnp.minimum(large, REL_BUCKETS - 1)
    return jnp.where(n < max_exact, n, large)


def _rms_split_kernel(x_ref, g_ref, hh_ref, hl_ref):
    x = x_ref[...]
    y = x * lax.rsqrt(jnp.mean(x * x, axis=-1, keepdims=True) + EPS) * g_ref[...]
    hh = y.astype(BF16)
    hh_ref[...] = hh
    hl_ref[...] = (y - hh.astype(F32)).astype(BF16)


def _rms_split(x2d, g, tm):
    m, d = x2d.shape
    return pl.pallas_call(
        _rms_split_kernel,
        grid=(m // tm,),
        in_specs=[pl.BlockSpec((tm, d), lambda i: (i, 0)), pl.BlockSpec((1, d), lambda i: (0, 0))],
        out_specs=[pl.BlockSpec((tm, d), lambda i: (i, 0))] * 2,
        out_shape=[jax.ShapeDtypeStruct((m, d), BF16)] * 2,
        compiler_params=_params("parallel"),
        name="rms_split",
    )(x2d, g.reshape(1, d))


def _inproj_kernel(rows_ref, hh_ref, hl_ref, wt_ref, o_ref, *, lo_tiles):
    j = pl.program_id(1)
    kiwi_tile = COL_KIWI // PROJ_TN

    def three_pass(w):
        w_hi = w.astype(BF16)
        w_lo = (w - w_hi.astype(F32)).astype(BF16)
        hh = hh_ref[...]
        acc = lax.dot_general(hh, w_hi, NT_DIMS, preferred_element_type=F32)
        acc = acc + lax.dot_general(hl_ref[...], w_hi, NT_DIMS, preferred_element_type=F32)
        return acc + lax.dot_general(hh, w_lo, NT_DIMS, preferred_element_type=F32)

    @pl.when(j < lo_tiles)
    def _():
        o_ref[...] = lax.dot_general(hh_ref[...], wt_ref[...].astype(BF16), NT_DIMS,
                                     preferred_element_type=F32)

    @pl.when((j >= lo_tiles) & (j != kiwi_tile))
    def _():
        o_ref[...] = three_pass(wt_ref[...])

    @pl.when(j == kiwi_tile)
    def _():
        o_ref[:, 0:LANES] = three_pass(wt_ref[0:LANES, :])
        o_ref[:, LANES:PROJ_TN] = jnp.zeros((o_ref.shape[0], PROJ_TN - LANES), F32)


def _inproj(hh, hl, w_in_t, layer, tm, lo_tiles):
    m, d = hh.shape
    n_tiles = PROJ_W // PROJ_TN
    resident = lambda: pl.BlockSpec((tm, d), lambda i, j, rows: (i, 0), pipeline_mode=pl.Buffered(1))
    return pl.pallas_call(
        functools.partial(_inproj_kernel, lo_tiles=lo_tiles),
        grid_spec=pltpu.PrefetchScalarGridSpec(
            num_scalar_prefetch=1,
            grid=(m // tm, n_tiles),
            in_specs=[resident(), resident(),
                      pl.BlockSpec((None, pl.Element(PROJ_TN), pl.Element(d)),
                                   lambda i, j, rows: (layer, rows[j] * PROJ_ROW_ALIGN, 0))],
            out_specs=pl.BlockSpec((tm, PROJ_TN), lambda i, j, rows: (i, j))),
        out_shape=jax.ShapeDtypeStruct((m, PROJ_W), F32),
        compiler_params=_params("parallel", "arbitrary"),
        name="inproj",
    )(jnp.asarray(_proj_tile_rows() // PROJ_ROW_ALIGN), hh, hl, w_in_t)


def _memkv_kernel(a_ref, w_ref, o_ref):
    o_ref[...] = _dot(a_ref[...], w_ref[...], False)


def _memkv(mem2d, w):
    m, d = mem2d.shape
    n = w.shape[1]
    tn = 512
    return pl.pallas_call(
        _memkv_kernel,
        grid=(n // tn,),
        in_specs=[pl.BlockSpec((m, d), lambda j: (0, 0)), pl.BlockSpec((d, tn), lambda j: (0, j))],
        out_specs=pl.BlockSpec((m, tn), lambda j: (0, j)),
        out_shape=jax.ShapeDtypeStruct((m, n), F32),
        compiler_params=_params("parallel"),
        name="memkv",
    )(mem2d, w)


def _bias_tile_kernel(rb_ref, o_ref):
    c = lax.broadcasted_iota(I32, (2 * QBLOCK, QBLOCK), 0)
    r = lax.broadcasted_iota(I32, (2 * QBLOCK, QBLOCK), 1)
    bucket = _t5_bucket(r - c + QBLOCK)
    for h in range(ATT_HEADS):
        tile = jnp.zeros((2 * QBLOCK, QBLOCK), F32)
        for b in range(REL_BUCKETS):
            tile = jnp.where(bucket == b, rb_ref[b, h], tile)
        o_ref[h] = tile


def _bias_tiles(rel_bias):
    return pl.pallas_call(
        _bias_tile_kernel,
        in_specs=[pl.BlockSpec(memory_space=pltpu.SMEM)],
        out_specs=pl.BlockSpec(memory_space=pltpu.VMEM),
        out_shape=jax.ShapeDtypeStruct((ATT_HEADS, 2 * QBLOCK, QBLOCK), F32),
        name="bias_tiles",
    )(rel_bias)


def _dsa_prompt_kernel(rb_ref, q_ref, ga_ref, qi_ref, kwq_ref, k_ref, v_ref, kwk_ref, bt_ref, o_ref,
                       ki4_ref, kb_ref, vt_ref, *, seq, topk):
    i = pl.program_id(1)

    @pl.when(i == 0)
    def _():
        kh, kl = _split_bf16(kwk_ref[:, IDX_HEADS:IDX_HEADS + IDX_DIM])
        ki4_ref[...] = jnp.concatenate([kh, kl, kh, kl], axis=1).astype(BF16)
        kb_ref[...] = k_ref[...].astype(BF16)
        for kb in range(seq // QBLOCK):
            for h in range(ATT_HEADS):
                rows = slice(kb * QBLOCK, (kb + 1) * QBLOCK)
                cols = slice(h * HEAD_DIM, (h + 1) * HEAD_DIM)
                vt_ref[cols, rows] = v_ref[rows, cols].T.astype(BF16)

    n_blk = seq // QBLOCK
    per = n_blk // KEY_EXTENT_CLASSES
    for cls in range(KEY_EXTENT_CLASSES):
        pl.when(i // per == cls)(functools.partial(
            _dsa_prompt_block, i, (cls + 1) * per * QBLOCK, per, topk,
            rb_ref, q_ref, ga_ref, qi_ref, kwq_ref, bt_ref, o_ref, ki4_ref, kb_ref, vt_ref))


def _dsa_prompt_block(i, nk, per, topk, rb_ref, q_ref, ga_ref, qi_ref, kwq_ref, bt_ref, o_ref,
                      ki4_ref, kb_ref, vt_ref):
    qi = qi_ref[...]
    wi_t = kwq_ref[...].T * (IDX_HEADS ** -0.5)
    score = jnp.zeros((nk, QBLOCK), F32)
    for h0 in range(0, IDX_HEADS, 2):
        q4 = []
        for h in (h0, h0 + 1):
            qh, ql = _split_bf16(qi[:, h * IDX_DIM:(h + 1) * IDX_DIM])
            q4.append(jnp.concatenate([qh, qh, ql, ql], axis=1).astype(BF16))
        s = lax.dot_general(ki4_ref[0:nk, :], jnp.concatenate(q4, axis=0), NT_DIMS,
                            preferred_element_type=F32)
        score = score + jnp.maximum(s[:, 0:QBLOCK], 0.0) * wi_t[h0:h0 + 1, :]
        score = score + jnp.maximum(s[:, QBLOCK:2 * QBLOCK], 0.0) * wi_t[h0 + 1:h0 + 2, :]

    key_pos = lax.broadcasted_iota(I32, (nk, QBLOCK), 0)
    q_pos = i * QBLOCK + lax.broadcasted_iota(I32, (nk, QBLOCK), 1)
    score = jnp.where(key_pos <= q_pos, score, -jnp.inf)
    mask_add = jnp.where(score >= _kth_largest(score, topk, axis=0), 0.0, -jnp.inf)

    near = min(nk, (per + 1) * QBLOCK)
    key_blk = key_pos[nk - near:, :] // QBLOCK
    is_diag = key_blk == i
    is_prev = key_blk == i - 1
    reps = near // QBLOCK
    q = q_ref[...] * (HEAD_DIM ** -0.5)
    ga = ga_ref[...]
    for h in range(ATT_HEADS):
        cols = slice(h * HEAD_DIM, (h + 1) * HEAD_DIM)
        far_bias = rb_ref[REL_BUCKETS - 1, h]
        t_prev = jnp.concatenate([bt_ref[h, 0:QBLOCK, :]] * reps, axis=0)
        t_diag = jnp.concatenate([bt_ref[h, QBLOCK:2 * QBLOCK, :]] * reps, axis=0)
        bias = jnp.where(is_diag, t_diag, jnp.where(is_prev, t_prev, far_bias))
        if nk > near:
            bias = jnp.concatenate([jnp.full((nk - near, QBLOCK), far_bias, F32), bias], axis=0)
        logits = lax.dot_general(kb_ref[0:nk, cols], q[:, cols].astype(BF16), NT_DIMS,
                                 preferred_element_type=F32)
        logits = logits + bias + mask_add
        m = jnp.max(logits, axis=0, keepdims=True)
        p = jnp.exp(logits - m)
        l = jnp.sum(p, axis=0, keepdims=True)
        o_t = jnp.dot(vt_ref[cols, 0:nk], p.astype(BF16), preferred_element_type=F32)
        o_ref[:, cols] = (o_t / l).T * _silu(ga[:, cols])


def _dsa_prompt(p3, bias_tiles, rel_bias):
    b, s, _ = p3.shape
    topk = min(TOPK_MAX, s // 4)
    row = lambda col, w: pl.BlockSpec((None, QBLOCK, w), lambda bi, i: (bi, i, col // w))
    full = lambda col, w: pl.BlockSpec((None, s, w), lambda bi, i: (bi, 0, col // w))
    return pl.pallas_call(
        functools.partial(_dsa_prompt_kernel, seq=s, topk=topk),
        grid=(b, s // QBLOCK),
        in_specs=[pl.BlockSpec(memory_space=pltpu.SMEM),
                  row(COL_Q, ATT_W), row(COL_GA, ATT_W), row(COL_QI, IDX_W), row(COL_KIWI, LANES),
                  full(COL_K, ATT_W), full(COL_V, ATT_W), full(COL_KIWI, LANES),
                  pl.BlockSpec((ATT_HEADS, 2 * QBLOCK, QBLOCK), lambda bi, i: (0, 0, 0))],
        out_specs=pl.BlockSpec((None, QBLOCK, ATT_W), lambda bi, i: (bi, i, 0)),
        out_shape=jax.ShapeDtypeStruct((b, s, ATT_W), F32),
        scratch_shapes=[pltpu.VMEM((s, 4 * IDX_DIM), BF16),
                        pltpu.VMEM((s, ATT_W), BF16),
                        pltpu.VMEM((ATT_W, s), BF16)],
        compiler_params=_params("parallel", "arbitrary"),
        name="dsa_prompt",
    )(rel_bias, p3, p3, p3, p3, p3, p3, p3, bias_tiles)


def _sidx_kernel(pt_ref, q_ref, w_ref, *rest, nch):
    k_refs = rest[:IDX_PAGES_PER_STEP]
    kn_ref, s_ref, q4_ref = rest[IDX_PAGES_PER_STEP:]
    c = pl.program_id(1)

    @pl.when(c == 0)
    def _():
        qh, ql = _split_bf16(q_ref[...])
        q4_ref[...] = jnp.concatenate([qh, qh, ql, ql], axis=1).astype(BF16)

    def scores(ki_t):
        n = ki_t.shape[1]
        kh, kl = _split_bf16(ki_t)
        ki4_t = jnp.concatenate([kh, kl, kh, kl], axis=0).astype(BF16)
        s = jnp.dot(q4_ref[...], ki4_t, preferred_element_type=F32)
        s = jnp.maximum(s, 0.0) * (w_ref[...] * (IDX_HEADS ** -0.5))
        return jnp.sum(s.reshape(IDX_HEADS, SUBLANES, n), axis=0)

    @pl.when(c < nch)
    def _():
        s_ref[...] = scores(jnp.concatenate([r[...] for r in k_refs], axis=1))

    @pl.when(c == nch)
    def _():
        n = kn_ref.shape[1]
        sn = scores(kn_ref[...])
        t_key = lax.broadcasted_iota(I32, (SUBLANES, n), 1)
        t_q = lax.broadcasted_iota(I32, (SUBLANES, n), 0)
        s_ref[...] = jnp.full(s_ref.shape, -jnp.inf, F32)
        s_ref[:, 0:n] = jnp.where(t_key <= t_q, sn, -jnp.inf)


def _sattn_kernel(pt_ref, rb_ref, sf_ref, sc_ref, q_ref, ga_ref, *rest, nch, past, topk, page):
    k_refs = rest[:PAGES_PER_STEP]
    v_refs = rest[PAGES_PER_STEP:2 * PAGES_PER_STEP]
    kn_ref, vn_ref, o_ref, thr_ref, m_ref, l_ref, acc_ref = rest[2 * PAGES_PER_STEP:]
    c = pl.program_id(1)
    chunk = PAGES_PER_STEP * page

    @pl.when(c == 0)
    def _():
        thr_ref[...] = _kth_largest(sf_ref[...], topk)
        m_ref[...] = jnp.full(m_ref.shape, -jnp.inf, F32)
        l_ref[...] = jnp.zeros(l_ref.shape, F32)
        acc_ref[...] = jnp.zeros(acc_ref.shape, F32)

    def attend(k_of, v_of, sc, key_pos0):
        n = sc.shape[1]
        mask_add = jnp.where(sc >= thr_ref[...], 0.0, -jnp.inf)
        q_pos = past + lax.broadcasted_iota(I32, (SUBLANES, n), 0)
        key_pos = key_pos0 + lax.broadcasted_iota(I32, (SUBLANES, n), 1)
        bucket = _t5_bucket(q_pos - key_pos)
        q = q_ref[...]
        rows = []
        for h in range(ATT_HEADS):
            cols = slice(h * HEAD_DIM, (h + 1) * HEAD_DIM)
            bias = jnp.zeros((SUBLANES, n), F32)
            for b in range(REL_BUCKETS):
                bias = jnp.where(bucket == b, rb_ref[b, h], bias)
            lg = lax.dot_general(q[:, cols].astype(BF16), k_of(h).astype(BF16), NT_DIMS,
                                 preferred_element_type=F32)
            rows.append(lg * (HEAD_DIM ** -0.5) + bias + mask_add)
        logits = jnp.concatenate(rows, axis=0)
        m_old = m_ref[...]
        m_new = jnp.maximum(m_old, jnp.max(logits, axis=1, keepdims=True))
        m_safe = jnp.where(m_new == -jnp.inf, 0.0, m_new)
        alpha = jnp.exp(m_old - m_safe)
        p = jnp.exp(logits - m_safe[:, 0:1])
        l_ref[...] = alpha * l_ref[...] + jnp.sum(p, axis=1, keepdims=True)
        m_ref[...] = m_new
        for h in range(ATT_HEADS):
            cols = slice(h * HEAD_DIM, (h + 1) * HEAD_DIM)
            rs = slice(h * SUBLANES, (h + 1) * SUBLANES)
            acc_ref[:, cols] = alpha[rs] * acc_ref[:, cols] + jnp.dot(
                p[rs].astype(BF16), v_of(h).astype(BF16), preferred_element_type=F32)

    @pl.when(c < nch)
    def _():
        k_of = lambda h: jnp.concatenate([r[h] for r in k_refs], axis=0)
        v_of = lambda h: jnp.concatenate([r[h] for r in v_refs], axis=0)
        attend(k_of, v_of, sc_ref[...], c * chunk)

    @pl.when(c == nch)
    def _():
        n = kn_ref.shape[0]
        head = lambda ref: (lambda h: ref[:, h * HEAD_DIM:(h + 1) * HEAD_DIM])
        attend(head(kn_ref), head(vn_ref), sc_ref[:, 0:n], past)
        ga = ga_ref[...]
        l = l_ref[...]
        for h in range(ATT_HEADS):
            cols = slice(h * HEAD_DIM, (h + 1) * HEAD_DIM)
            o_ref[:, cols] = acc_ref[:, cols] / l[h * SUBLANES:(h + 1) * SUBLANES] * _silu(ga[:, cols])


def _dsa_sample(layer, ps3, cache_k_t, cache_v_t, cache_kidx_t, page_table, rel_bias):
    db, t, _ = ps3.shape
    n_pages = page_table.shape[1]
    page = cache_kidx_t.shape[3]
    past = n_pages * page
    topk = min(TOPK_MAX, (past + t) // 4)
    nch = n_pages // PAGES_PER_STEP
    chunk = PAGES_PER_STEP * page
    nch_idx = n_pages // IDX_PAGES_PER_STEP
    chunk_idx = IDX_PAGES_PER_STEP * page
    ntot = (nch_idx + 1) * chunk_idx
    assert chunk_idx % chunk == 0 and t == SUBLANES

    qi = ps3[:, :, COL_QI:COL_QI + IDX_W].reshape(db, t, IDX_HEADS, IDX_DIM)
    qi_hq = qi.transpose(0, 2, 1, 3).reshape(db, IDX_HEADS * t, IDX_DIM)
    wi_hq = ps3[:, :, COL_WI:COL_WI + IDX_HEADS].transpose(0, 2, 1).reshape(db, IDX_HEADS * t, 1)
    pad_rows = lambda a: jnp.pad(a, ((0, 0), (0, page - t), (0, 0)))
    ki_new_t = pad_rows(ps3[:, :, COL_KI:COL_KI + IDX_DIM]).transpose(0, 2, 1)
    k_new = pad_rows(ps3[:, :, COL_K:COL_K + ATT_W])
    v_new = pad_rows(ps3[:, :, COL_V:COL_V + ATT_W])
    q = ps3[:, :, COL_Q:COL_Q + ATT_W]
    ga = ps3[:, :, COL_GA:COL_GA + ATT_W]

    def page_spec(r, per_step, dims, **kw):
        def imap(b, c, pt):
            return (layer, pt[b, jnp.minimum(c * per_step + r, n_pages - 1)]) + (0,) * len(dims)
        return pl.BlockSpec((None, None) + dims, imap, **kw)

    per_b = lambda rows, w: pl.BlockSpec((None, rows, w), lambda b, c, pt: (b, 0, 0))

    scores = pl.pallas_call(
        functools.partial(_sidx_kernel, nch=nch_idx),
        grid_spec=pltpu.PrefetchScalarGridSpec(
            num_scalar_prefetch=1,
            grid=(db, nch_idx + 1),
            in_specs=[per_b(IDX_HEADS * t, IDX_DIM), per_b(IDX_HEADS * t, 1)]
                     + [page_spec(r, IDX_PAGES_PER_STEP, (IDX_DIM, page)) for r in range(IDX_PAGES_PER_STEP)]
                     + [per_b(IDX_DIM, page)],
            out_specs=pl.BlockSpec((None, t, chunk_idx), lambda b, c, pt: (b, 0, c)),
            scratch_shapes=[pltpu.VMEM((IDX_HEADS * t, 4 * IDX_DIM), BF16)]),
        out_shape=jax.ShapeDtypeStruct((db, t, ntot), F32),
        compiler_params=_params("parallel", "arbitrary"),
        name="dsa_sample_scores",
    )(page_table, qi_hq, wi_hq, *([cache_kidx_t] * IDX_PAGES_PER_STEP), ki_new_t)

    return pl.pallas_call(
        functools.partial(_sattn_kernel, nch=nch, past=past, topk=topk, page=page),
        grid_spec=pltpu.PrefetchScalarGridSpec(
            num_scalar_prefetch=1,
            grid=(db, nch + 1),
            in_specs=[pl.BlockSpec(memory_space=pltpu.SMEM),
                      per_b(t, ntot),
                      pl.BlockSpec((None, t, chunk), lambda b, c, pt: (b, 0, c)),
                      per_b(t, ATT_W), per_b(t, ATT_W)]
                     + [page_spec(r, PAGES_PER_STEP, (ATT_HEADS, page, HEAD_DIM),
                                  pipeline_mode=pl.Buffered(PAGE_BUFFERS)) for r in range(PAGES_PER_STEP)] * 2
                     + [per_b(page, ATT_W), per_b(page, ATT_W)],
            out_specs=per_b(t, ATT_W),
            scratch_shapes=[pltpu.VMEM((t, 1), F32),
                            pltpu.VMEM((ATT_HEADS * t, LANES), F32),
                            pltpu.VMEM((ATT_HEADS * t, LANES), F32),
                            pltpu.VMEM((t, ATT_W), F32)]),
        out_shape=jax.ShapeDtypeStruct((db, t, ATT_W), F32),
        compiler_params=_params("parallel", "arbitrary"),
        name="dsa_sample_attn",
    )(page_table, rel_bias, scores, scores, q, ga,
      *([cache_k_t] * PAGES_PER_STEP), *([cache_v_t] * PAGES_PER_STEP), k_new, v_new)


def _ssm_param_kernel(are_ref, aim_ref, ldt_ref, bre_ref, bim_ref, lbr_ref, lbi_ref, bbr_ref, bbi_ref):
    lam_re = jnp.minimum(are_ref[...], -1e-4)
    lam_im = aim_ref[...]
    step = jnp.exp(ldt_ref[...])
    mag = jnp.exp(lam_re * step)
    ang = lam_im * step
    lb_re = mag * jnp.cos(ang)
    lb_im = mag * jnp.sin(ang)
    nr = lb_re - 1.0
    ni = lb_im
    den = lam_re * lam_re + lam_im * lam_im
    coef_re = (nr * lam_re + ni * lam_im) / den
    coef_im = (ni * lam_re - nr * lam_im) / den
    lbr_ref[...] = lb_re
    lbi_ref[...] = lb_im
    for c in range(SSM_GROUP):
        br = bre_ref[c]
        bi = bim_ref[c]
        bbr_ref[c] = coef_re * br - coef_im * bi
        bbi_ref[c] = coef_re * bi + coef_im * br


def _ssm_params(a_re, a_im, log_dt, b_re, b_im):
    g, p, c = b_re.shape
    vm = pl.BlockSpec(memory_space=pltpu.VMEM)
    return pl.pallas_call(
        _ssm_param_kernel,
        in_specs=[vm] * 5,
        out_specs=[vm] * 4,
        out_shape=[jax.ShapeDtypeStruct((g, p), F32)] * 2 + [jax.ShapeDtypeStruct((c, g, p), F32)] * 2,
        name="ssm_params",
    )(a_re, a_im, log_dt.reshape(g, 1), b_re.transpose(2, 0, 1), b_im.transpose(2, 0, 1))


def _block_diag_slabs(m):
    g, a, b = m.shape
    eye = jnp.eye(SLAB_GROUPS, dtype=m.dtype)
    m4 = m.reshape(N_SLABS, SLAB_GROUPS, a, b)
    return jnp.einsum('sgab,gh->sgahb', m4, eye).reshape(N_SLABS, SLAB_GROUPS * a, SLAB_GROUPS * b)


def _scan_pitch(seg_len):
    return seg_len + 4 if seg_len % 8 == 0 else seg_len


def _s5_kernel(u_ref, x0r_ref, x0i_ref, lbr_ref, lbi_ref, wbu_ref, wyr_ref, wyi_ref, d_ref,
               z_ref, xr_ref, xi_ref, up_ref, zs_ref, *, seg_len, chain, hi):
    nc = SLAB_STATES // LANES
    pitch = _scan_pitch(seg_len)
    up_ref[...] = jnp.zeros(up_ref.shape, F32)
    for s in range(SCAN_ROWS):
        up_ref[pl.ds(s * pitch, seg_len), :] = u_ref[pl.ds(s * seg_len, seg_len), :]
    u = up_ref[...]
    bu = _dot(u, wbu_ref[...], hi)
    for j in range(2 * nc):
        zs_ref[j] = bu[:, j * LANES:(j + 1) * LANES]
    shape = (SCAN_ROWS, SLAB_STATES)
    lbr = jnp.broadcast_to(lbr_ref[...], shape)
    lbi = jnp.broadcast_to(lbi_ref[...], shape)

    def load(t, base):
        rows = pl.ds(t, SCAN_ROWS, stride=pitch)
        return jnp.concatenate([zs_ref[base + j, rows, :] for j in range(nc)], axis=1)

    def store(t, base, val):
        rows = pl.ds(t, SCAN_ROWS, stride=pitch)
        for j in range(nc):
            zs_ref[base + j, rows, :] = val[:, j * LANES:(j + 1) * LANES]

    def advance(t, carry):
        xr, xi = carry
        nr = lbr * xr - lbi * xi + load(t, 0)
        ni = lbr * xi + lbi * xr + load(t, nc)
        return nr, ni

    def advance_store(t, carry):
        nr, ni = advance(t, carry)
        store(t, 0, nr)
        store(t, nc, ni)
        return nr, ni

    unroll = min(seg_len, 8)
    x0r = x0r_ref[...]
    x0i = x0i_ref[...]
    if chain:
        zero = jnp.zeros(shape, F32)
        er, ei = lax.fori_loop(0, seg_len, advance, (zero, zero), unroll=unroll)
        pr, pi = lbr, lbi
        for _ in range(int(math.log2(seg_len))):
            pr, pi = pr * pr - pi * pi, 2.0 * pr * pi
        row = lax.broadcasted_iota(I32, shape, 0)
        sr = jnp.where(row == 0, x0r, 0.0)
        si = jnp.where(row == 0, x0i, 0.0)
        for s in range(SCAN_ROWS - 1):
            nr = er + pr * sr - pi * si
            ni = ei + pr * si + pi * sr
            sr = jnp.where(row == s + 1, pltpu.roll(nr, 1, axis=0), sr)
            si = jnp.where(row == s + 1, pltpu.roll(ni, 1, axis=0), si)
        x0r, x0i = sr, si
    xr, xi = lax.fori_loop(0, seg_len, advance_store, (x0r, x0i), unroll=unroll)
    xr_ref[...] = xr
    xi_ref[...] = xi
    y = d_ref[...] * u
    for j in range(nc):
        blk = slice(j * LANES, (j + 1) * LANES)
        y = y + _dot(zs_ref[j], wyr_ref[blk, :], hi) - _dot(zs_ref[nc + j], wyi_ref[blk, :], hi)
    up_ref[...] = _gelu_tanh(y)
    for s in range(SCAN_ROWS):
        z_ref[pl.ds(s * seg_len, seg_len), :] = up_ref[pl.ds(s * pitch, seg_len), :]


def _s5(p3, x0r, x0i, lbr, lbi, wbu, wyr, wyi, d, *, seg_len, chain, hi):
    nb, r, _ = p3.shape
    assert r == SCAN_ROWS * seg_len and (not chain or seg_len & (seg_len - 1) == 0)
    rp = SCAN_ROWS * _scan_pitch(seg_len)
    slab = lambda rows, w: pl.BlockSpec((None, rows, w), lambda b, s: (s, 0, 0))
    st = pl.BlockSpec((None, SCAN_ROWS, SLAB_STATES), lambda b, s: (b, 0, s))
    return pl.pallas_call(
        functools.partial(_s5_kernel, seg_len=seg_len, chain=chain, hi=hi),
        grid=(nb, N_SLABS),
        in_specs=[pl.BlockSpec((None, r, LANES), lambda b, s: (b, 0, COL_U // LANES + s)),
                  st, st, slab(1, SLAB_STATES), slab(1, SLAB_STATES),
                  slab(LANES, 2 * SLAB_STATES), slab(SLAB_STATES, LANES), slab(SLAB_STATES, LANES),
                  slab(1, LANES)],
        out_specs=[pl.BlockSpec((None, r, LANES), lambda b, s: (b, 0, s)), st, st],
        out_shape=[jax.ShapeDtypeStruct((nb, r, SSM_W), F32),
                   jax.ShapeDtypeStruct((nb, SCAN_ROWS, SSM_GROUPS * SSM_STATE), F32),
                   jax.ShapeDtypeStruct((nb, SCAN_ROWS, SSM_GROUPS * SSM_STATE), F32)],
        scratch_shapes=[pltpu.VMEM((rp, LANES), F32),
                        pltpu.VMEM((2 * SLAB_STATES // LANES, rp, LANES), F32)],
        compiler_params=_params("parallel", "parallel"),
        name="s5_scan",
    )(p3, x0r, x0i, lbr, lbi, wbu, wyr, wyi, d)


def _cross_kernel(q_ref, g_ref, mk_ref, mv_ref, o_ref):
    q = q_ref[...]
    g = g_ref[...]
    for h in range(CROSS_HEADS):
        cols = slice(h * CROSS_HD, (h + 1) * CROSS_HD)
        logits = lax.dot_general(q[:, cols].astype(BF16), mk_ref[:, cols].astype(BF16), NT_DIMS,
                                 preferred_element_type=F32) * (CROSS_HD ** -0.5)
        m = jnp.max(logits, axis=1, keepdims=True)
        p = jnp.exp(logits - m)
        l = jnp.sum(p, axis=1, keepdims=True)
        o = jnp.dot(p.astype(BF16), mv_ref[:, cols].astype(BF16), preferred_element_type=F32)
        o_ref[:, cols] = o / l * _silu(g[:, cols])


def _cross(p3, mk, mv, tq):
    nb, s, _ = p3.shape
    mem = mk.shape[1]
    qspec = lambda col: pl.BlockSpec((None, tq, CROSS_W), lambda b, i: (b, i, col // CROSS_W))
    mspec = pl.BlockSpec((None, mem, CROSS_W), lambda b, i: (b, 0, 0))
    return pl.pallas_call(
        _cross_kernel,
        grid=(nb, s // tq),
        in_specs=[qspec(COL_QC), qspec(COL_GC), mspec, mspec],
        out_specs=pl.BlockSpec((None, tq, CROSS_W), lambda b, i: (b, i, 0)),
        out_shape=jax.ShapeDtypeStruct((nb, s, CROSS_W), F32),
        compiler_params=_params("parallel", "parallel"),
        name="cross_attn",
    )(p3, p3, mk, mv)


def _merge_out_kernel(x_ref, a_ref, z_ref, c_ref, gs_ref, gm0_ref, gm1_ref, gm2_ref,
                      wglu_ref, bglu_ref, wba_ref, wbs_ref, wbc_ref, wout_ref, gfin_ref, o_ref, *, hi, final):
    gl = _dot(z_ref[...], wglu_ref[...], hi) + bglu_ref[...]
    s_out = gl[:, 0:SSM_W] * _sigmoid(gl[:, SSM_W:2 * SSM_W]) * _silu(gs_ref[...])
    merged = _sigmoid(gm0_ref[...]) * _dot(a_ref[...], wba_ref[...], hi)
    merged = merged + _sigmoid(gm1_ref[...]) * _dot(s_out, wbs_ref[...], hi)
    merged = merged + _sigmoid(gm2_ref[...]) * _dot(c_ref[...], wbc_ref[...], hi)
    y = x_ref[...] + _dot(merged, wout_ref[...], hi)
    if final:
        y = y * lax.rsqrt(jnp.mean(y * y, axis=-1, keepdims=True) + EPS) * gfin_ref[...]
    o_ref[...] = y


def _merge_out(x2d, a2d, z2d, c2d, p2d, w_glu, b_glu, w_ba, w_bs, w_bc, w_out, g_final, tm, hi, final):
    m, d = x2d.shape
    rows = lambda w, col=0: pl.BlockSpec((tm, w), lambda i: (i, col // w))
    const = lambda arr: pl.BlockSpec(arr.shape, lambda i: (0, 0), pipeline_mode=pl.Buffered(1))
    b2 = b_glu.reshape(1, -1)
    g2 = g_final.reshape(1, d)
    return pl.pallas_call(
        functools.partial(_merge_out_kernel, hi=hi, final=final),
        grid=(m // tm,),
        in_specs=[rows(d), rows(ATT_W), rows(SSM_W), rows(CROSS_W), rows(SSM_W, COL_GS),
                  rows(D_MODEL, COL_GM), rows(D_MODEL, COL_GM + D_MODEL), rows(D_MODEL, COL_GM + 2 * D_MODEL),
                  const(w_glu), const(b2), const(w_ba), const(w_bs), const(w_bc), const(w_out), const(g2)],
        out_specs=rows(d),
        out_shape=jax.ShapeDtypeStruct((m, d), F32),
        compiler_params=_params("parallel"),
        name="merge_out",
    )(x2d, a2d, z2d, c2d, p2d, p2d, p2d, p2d, w_glu, b2, w_ba, w_bs, w_bc, w_out, g2)


def _mixer_layer(x3, attend, x0r, x0i, mk, mv, norm_g, w_in_t, layer, w_glu, b_glu, w_ba, w_bs, w_bc, w_out,
                 ssm, g_final, *, tm, n_scan, seg_len, chain, hi, tq, final):
    nb, s, d = x3.shape
    m = nb * s
    x2d = x3.reshape(m, d)
    tr = min(tm, 256)
    hh, hl = _rms_split(x2d, norm_g, tr)
    p2d = _inproj(hh, hl, w_in_t, layer, tm, 0 if hi else PROJ_LO_TILES)
    p3 = p2d.reshape(nb, s, PROJ_W)
    a_out = attend(p3)
    z, xr, xi = _s5(p2d.reshape(n_scan, m // n_scan, PROJ_W), x0r, x0i, *ssm,
                    seg_len=seg_len, chain=chain, hi=hi)
    c_out = _cross(p3, mk, mv, tq)
    y = _merge_out(x2d, a_out.reshape(m, ATT_W), z.reshape(m, SSM_W), c_out.reshape(m, CROSS_W), p2d,
                   w_glu, b_glu, w_ba, w_bs, w_bc, w_out, g_final, tr, hi, final).reshape(nb, s, d)
    ki = p3[:, :, COL_KI:COL_KI + IDX_DIM]
    return y, (p3, ki, xr, xi)


def _stack_heads_kernel(*refs):
    *in_refs, o_ref = refs
    for l, r in enumerate(in_refs):
        for h in range(ATT_HEADS):
            o_ref[l, h] = r[:, h * HEAD_DIM:(h + 1) * HEAD_DIM]


def _stack_heads(p3_layers, col, ts):
    depth = len(p3_layers)
    b, s, _ = p3_layers[0].shape
    out = pl.pallas_call(
        _stack_heads_kernel,
        grid=(b, s // ts),
        in_specs=[pl.BlockSpec((None, ts, ATT_W), lambda bi, i: (bi, i, col // ATT_W))] * depth,
        out_specs=pl.BlockSpec((depth, None, ATT_HEADS, ts, HEAD_DIM), lambda bi, i: (0, bi, 0, i, 0)),
        out_shape=jax.ShapeDtypeStruct((depth, b, ATT_HEADS, s, HEAD_DIM), F32),
        compiler_params=_params("parallel", "parallel"),
        name="stack_heads",
    )(*p3_layers)
    return jnp.transpose(out, (0, 1, 3, 2, 4))


def kernel(x_prompt, x_sample, cache_k, cache_v, cache_kidx, cache_mem_k, cache_mem_v, state_ssm_re, state_ssm_im, page_table, mem_prompt, norm_g, w_in, w_branch_attn, w_branch_ssm, w_branch_cross, w_out, w_mem_kv, ssm_a_re, ssm_a_im, ssm_b_re, ssm_b_im, ssm_c_re, ssm_c_im, ssm_d, ssm_log_dt, w_glu, b_glu, rel_bias, final_norm_g):
    depth = w_in.shape[0]
    b, s, d = x_prompt.shape
    db, t, _ = x_sample.shape
    mem_len = mem_prompt.shape[1]
    n_state = SSM_GROUPS * SSM_STATE
    assert s == SCAN_ROWS * (s // SCAN_ROWS) and db == SCAN_ROWS

    bias_tiles = _bias_tiles(rel_bias)
    w_in_t = jnp.transpose(w_in, (0, 2, 1))
    cache_k_t = jnp.transpose(cache_k, (0, 1, 3, 2, 4))
    cache_v_t = jnp.transpose(cache_v, (0, 1, 3, 2, 4))
    cache_kidx_t = jnp.transpose(cache_kidx, (0, 1, 3, 2))
    zeros_state = jnp.zeros((b, SCAN_ROWS, n_state), F32)
    yp, ys = x_prompt, x_sample
    outs = [[] for _ in range(12)]
    p3_prompt = []
    for l in range(depth):
        lb_re, lb_im, bb_re, bb_im = _ssm_params(ssm_a_re[l], ssm_a_im[l], ssm_log_dt[l],
                                                 ssm_b_re[l], ssm_b_im[l])
        wbu = jnp.concatenate([_block_diag_slabs(bb_re.transpose(1, 0, 2)),
                               _block_diag_slabs(bb_im.transpose(1, 0, 2))], axis=2)
        wyr = _block_diag_slabs(ssm_c_re[l].transpose(0, 2, 1))
        wyi = _block_diag_slabs(ssm_c_im[l].transpose(0, 2, 1))
        ssm = (lb_re.reshape(N_SLABS, 1, SLAB_STATES), lb_im.reshape(N_SLABS, 1, SLAB_STATES),
               wbu, wyr, wyi, ssm_d[l].reshape(N_SLABS, 1, LANES))

        mkv = _memkv(mem_prompt.reshape(b * mem_len, d), w_mem_kv[l])
        mk_p = mkv[:, :CROSS_W].reshape(b, mem_len, CROSS_W)
        mv_p = mkv[:, CROSS_W:].reshape(b, mem_len, CROSS_W)
        attend_p = functools.partial(_dsa_prompt, bias_tiles=bias_tiles, rel_bias=rel_bias)
        yp, (p3_p, kip, xr, xi) = _mixer_layer(
            yp, attend_p, zeros_state, zeros_state, mk_p, mv_p, norm_g[l], w_in_t, l,
            w_glu[l].astype(BF16), b_glu[l], w_branch_attn[l].astype(BF16), w_branch_ssm[l].astype(BF16),
            w_branch_cross[l].astype(BF16), w_out[l].astype(BF16), ssm, final_norm_g,
            tm=2048, n_scan=b, seg_len=s // SCAN_ROWS, chain=True, hi=False, tq=512, final=l == depth - 1)
        p3_prompt.append(p3_p); outs[2].append(kip)
        outs[3].append(mk_p.reshape(b, mem_len, CROSS_HEADS, CROSS_HD))
        outs[4].append(mv_p.reshape(b, mem_len, CROSS_HEADS, CROSS_HD))
        outs[5].append(xr[:, SCAN_ROWS - 1].reshape(b, SSM_GROUPS, SSM_STATE))
        outs[6].append(xi[:, SCAN_ROWS - 1].reshape(b, SSM_GROUPS, SSM_STATE))

        attend_s = functools.partial(_dsa_sample, l, cache_k_t=cache_k_t, cache_v_t=cache_v_t,
                                     cache_kidx_t=cache_kidx_t, page_table=page_table, rel_bias=rel_bias)
        ys, (p3_s, kin, sr, si) = _mixer_layer(
            ys, attend_s,
            state_ssm_re[l].reshape(1, db, n_state), state_ssm_im[l].reshape(1, db, n_state),
            cache_mem_k[l].reshape(db, mem_len, CROSS_W), cache_mem_v[l].reshape(db, mem_len, CROSS_W),
            norm_g[l], w_in_t, l, w_glu[l], b_glu[l], w_branch_attn[l], w_branch_ssm[l],
            w_branch_cross[l], w_out[l], ssm, final_norm_g,
            tm=db * t, n_scan=1, seg_len=t, chain=False, hi=True, tq=t, final=l == depth - 1)
        outs[7].append(p3_s[:, :, COL_K:COL_K + ATT_W].reshape(db, t, ATT_HEADS, HEAD_DIM))
        outs[8].append(p3_s[:, :, COL_V:COL_V + ATT_W].reshape(db, t, ATT_HEADS, HEAD_DIM))
        outs[9].append(kin)
        outs[10].append(sr.reshape(db, SSM_GROUPS, SSM_STATE)); outs[11].append(si.reshape(db, SSM_GROUPS, SSM_STATE))

    stacked = [jnp.stack(o, axis=0) if o else None for o in outs]
    stacked[0] = _stack_heads(p3_prompt, COL_K, 512)
    stacked[1] = _stack_heads(p3_prompt, COL_V, 512)
    return (yp, ys) + tuple(stacked)
```

```python
import functools
import math

import jax
import jax.numpy as jnp
import numpy as np
from jax import lax
from jax.experimental import pallas as pl
from jax.experimental.pallas import tpu as pltpu

F32 = jnp.float32
BF16 = jnp.bfloat16
I32 = jnp.int32
HIGHEST = lax.Precision.HIGHEST
NT_DIMS = (((1,), (1,)), ((), ()))

VMEM_LIMIT_BYTES = 56 * 1024 * 1024
LANES = 128
SUBLANES = 8

D_MODEL = 2048
HEAD_DIM = 128
ATT_HEADS = 6
ATT_W = ATT_HEADS * HEAD_DIM
IDX_HEADS = 16
IDX_DIM = 64
IDX_W = IDX_HEADS * IDX_DIM
TOPK_MAX = 256
QBLOCK = 128
SSM_GROUP = 16
SSM_W = 768
SSM_GROUPS = SSM_W // SSM_GROUP
SSM_STATE = 64
CROSS_HEADS = 4
CROSS_HD = 128
CROSS_W = CROSS_HEADS * CROSS_HD
REL_BUCKETS = 32
REL_MAX_DIST = 128
N_BRANCHES = 3
EPS = 1e-6
SPLIT_WIDTHS = (ATT_W, ATT_W, ATT_W, ATT_W, IDX_W, IDX_HEADS, IDX_DIM,
                SSM_W, SSM_W, CROSS_W, CROSS_W, N_BRANCHES * D_MODEL)
SPLIT_POINTS = tuple(int(c) for c in np.cumsum(SPLIT_WIDTHS)[:-1])

COL_GM = 0
COL_Q = 6144
COL_K = COL_Q + ATT_W
COL_V = COL_K + ATT_W
COL_GA = COL_V + ATT_W
COL_U = COL_GA + ATT_W
COL_GS = COL_U + SSM_W
COL_QC = COL_GS + SSM_W
COL_GC = COL_QC + CROSS_W
COL_KIWI = COL_GC + CROSS_W
COL_WI = COL_KIWI
COL_KI = COL_KIWI + IDX_HEADS
COL_QI = 12288
PROJ_W = COL_QI + IDX_W
PROJ_TN = 512
PROJ_ROW_ALIGN = 16
PROJ_LO_TILES = COL_KIWI // PROJ_TN


def _proj_tile_rows():
    ref_start = dict(zip(("q", "k", "v", "ga", "qi", "wi", "ki", "u", "gs", "qc", "gc", "gm"),
                         (0,) + SPLIT_POINTS))
    runs = ((COL_GM, "gm", N_BRANCHES * D_MODEL), (COL_Q, "q", 4 * ATT_W), (COL_U, "u", 2 * SSM_W),
            (COL_QC, "qc", 2 * CROSS_W), (COL_KIWI, "wi", PROJ_TN), (COL_QI, "qi", IDX_W))
    rows = np.zeros(PROJ_W // PROJ_TN, np.int32)
    for col, name, width in runs:
        for t in range(width // PROJ_TN):
            rows[col // PROJ_TN + t] = ref_start[name] + t * PROJ_TN
    assert not (rows % PROJ_ROW_ALIGN).any()
    return rows

SLAB_GROUPS = LANES // SSM_GROUP
SLAB_STATES = SLAB_GROUPS * SSM_STATE
N_SLABS = SSM_GROUPS // SLAB_GROUPS
SCAN_ROWS = SUBLANES

KEY_EXTENT_CLASSES = 4
COUNT_FOLD_ROWS = 64

PAGES_PER_STEP = 8
IDX_PAGES_PER_STEP = 32
PAGE_BUFFERS = 2

INT_MIN = -2 ** 31
LOWEST_F32_KEY = -0x7F800000


def _params(*sem):
    return pltpu.CompilerParams(dimension_semantics=sem, vmem_limit_bytes=VMEM_LIMIT_BYTES)


def _sigmoid(x):
    return 1.0 / (1.0 + jnp.exp(-x))


def _silu(x):
    return x * _sigmoid(x)


def _gelu_tanh(x):
    return 0.5 * x * (1.0 + jnp.tanh(math.sqrt(2.0 / math.pi) * (x + 0.044715 * (x * x * x))))


def _dot(a, b, hi):
    if hi:
        return jnp.dot(a.astype(F32), b.astype(F32), precision=HIGHEST, preferred_element_type=F32)
    return jnp.dot(a.astype(BF16), b.astype(BF16), preferred_element_type=F32)


def _split_bf16(x):
    hi = x.astype(BF16).astype(F32)
    return hi, x - hi


def _ordered_bits_to_float(key):
    return lax.bitcast_convert_type(key ^ ((key >> 31) & I32(0x7FFFFFFF)), F32)


def _kth_largest(score, k, axis=1):
    def count_ge(cand_bits):
        ones = (score >= _ordered_bits_to_float(cand_bits)).astype(I32)
        if axis == 0:
            while ones.shape[0] > COUNT_FOLD_ROWS and ones.shape[0] % (2 * SUBLANES) == 0:
                half = ones.shape[0] // 2
                ones = ones[:half] + ones[half:]
            return jnp.sum(ones.astype(F32), axis=0, keepdims=True).astype(I32)
        return jnp.sum(ones, axis=axis, keepdims=True)

    zero = jnp.zeros((score.shape[0], 1) if axis == 1 else (1, score.shape[1]), I32)
    t = jnp.where(count_ge(zero) >= k, zero, jnp.full_like(zero, INT_MIN))

    def body(it, t):
        cand = t | (I32(1) << (I32(30) - it))
        return jnp.where(count_ge(cand) >= k, cand, t)

    t = lax.fori_loop(0, 31, body, t)
    return _ordered_bits_to_float(jnp.maximum(t, I32(LOWEST_F32_KEY)))


def _t5_bucket(dist):
    n = jnp.maximum(dist, 0)
    max_exact = REL_BUCKETS // 2
    nf = jnp.maximum(n, 1).astype(F32)
    scale = (REL_BUCKETS - max_exact) / math.log(REL_MAX_DIST / max_exact)
    large = max_exact + jnp.floor(jnp.log(nf * (1.0 / max_exact)) * scale).astype(I32)
    large = jnp.minimum(large, REL_BUCKETS - 1)
    return jnp.where(n < max_exact, n, large)


def _rms_split_kernel(x_ref, g_ref, hh_ref, hl_ref):
    x = x_ref[...]
    y = x * lax.rsqrt(jnp.mean(x * x, axis=-1, keepdims=True) + EPS) * g_ref[...]
    hh = y.astype(BF16)
    hh_ref[...] = hh
    hl_ref[...] = (y - hh.astype(F32)).astype(BF16)


def _rms_split(x2d, g, tm):
    m, d = x2d.shape
    return pl.pallas_call(
        _rms_split_kernel,
        grid=(m // tm,),
        in_specs=[pl.BlockSpec((tm, d), lambda i: (i, 0)), pl.BlockSpec((1, d), lambda i: (0, 0))],
        out_specs=[pl.BlockSpec((tm, d), lambda i: (i, 0))] * 2,
        out_shape=[jax.ShapeDtypeStruct((m, d), BF16)] * 2,
        compiler_params=_params("parallel"),
        name="rms_split",
    )(x2d, g.reshape(1, d))


def _inproj_kernel(rows_ref, hh_ref, hl_ref, wt_ref, o_ref, *, lo_tiles):
    j = pl.program_id(1)
    kiwi_tile = COL_KIWI // PROJ_TN

    def three_pass(w):
        w_hi = w.astype(BF16)
        w_lo = (w - w_hi.astype(F32)).astype(BF16)
        hh = hh_ref[...]
        acc = lax.dot_general(hh, w_hi, NT_DIMS, preferred_element_type=F32)
        acc = acc + lax.dot_general(hl_ref[...], w_hi, NT_DIMS, preferred_element_type=F32)
        return acc + lax.dot_general(hh, w_lo, NT_DIMS, preferred_element_type=F32)

    @pl.when(j < lo_tiles)
    def _():
        o_ref[...] = lax.dot_general(hh_ref[...], wt_ref[...].astype(BF16), NT_DIMS,
                                     preferred_element_type=F32)

    @pl.when((j >= lo_tiles) & (j != kiwi_tile))
    def _():
        o_ref[...] = three_pass(wt_ref[...])

    @pl.when(j == kiwi_tile)
    def _():
        o_ref[:, 0:LANES] = three_pass(wt_ref[0:LANES, :])
        o_ref[:, LANES:PROJ_TN] = jnp.zeros((o_ref.shape[0], PROJ_TN - LANES), F32)


def _inproj(hh, hl, w_in_t, layer, tm, lo_tiles):
    m, d = hh.shape
    n_tiles = PROJ_W // PROJ_TN
    resident = lambda: pl.BlockSpec((tm, d), lambda i, j, rows: (i, 0), pipeline_mode=pl.Buffered(1))
    return pl.pallas_call(
        functools.partial(_inproj_kernel, lo_tiles=lo_tiles),
        grid_spec=pltpu.PrefetchScalarGridSpec(
            num_scalar_prefetch=1,
            grid=(m // tm, n_tiles),
            in_specs=[resident(), resident(),
                      pl.BlockSpec((None, pl.Element(PROJ_TN), pl.Element(d)),
                                   lambda i, j, rows: (layer, rows[j] * PROJ_ROW_ALIGN, 0))],
            out_specs=pl.BlockSpec((tm, PROJ_TN), lambda i, j, rows: (i, j))),
        out_shape=jax.ShapeDtypeStruct((m, PROJ_W), F32),
        compiler_params=_params("parallel", "arbitrary"),
        name="inproj",
    )(jnp.asarray(_proj_tile_rows() // PROJ_ROW_ALIGN), hh, hl, w_in_t)


def _memkv_kernel(a_ref, w_ref, o_ref):
    o_ref[...] = _dot(a_ref[...], w_ref[...], False)


def _memkv(mem2d, w):
    m, d = mem2d.shape
    n = w.shape[1]
    tn = 512
    return pl.pallas_call(
        _memkv_kernel,
        grid=(n // tn,),
        in_specs=[pl.BlockSpec((m, d), lambda j: (0, 0)), pl.BlockSpec((d, tn), lambda j: (0, j))],
        out_specs=pl.BlockSpec((m, tn), lambda j: (0, j)),
        out_shape=jax.ShapeDtypeStruct((m, n), F32),
        compiler_params=_params("parallel"),
        name="memkv",
    )(mem2d, w)


def _bias_tile_kernel(rb_ref, o_ref):
    c = lax.broadcasted_iota(I32, (2 * QBLOCK, QBLOCK), 0)
    r = lax.broadcasted_iota(I32, (2 * QBLOCK, QBLOCK), 1)
    bucket = _t5_bucket(r - c + QBLOCK)
    for h in range(ATT_HEADS):
        tile = jnp.zeros((2 * QBLOCK, QBLOCK), F32)
        for b in range(REL_BUCKETS):
            tile = jnp.where(bucket == b, rb_ref[b, h], tile)
        o_ref[h] = tile


def _bias_tiles(rel_bias):
    return pl.pallas_call(
        _bias_tile_kernel,
        in_specs=[pl.BlockSpec(memory_space=pltpu.SMEM)],
        out_specs=pl.BlockSpec(memory_space=pltpu.VMEM),
        out_shape=jax.ShapeDtypeStruct((ATT_HEADS, 2 * QBLOCK, QBLOCK), F32),
        name="bias_tiles",
    )(rel_bias)


def _dsa_prompt_kernel(rb_ref, q_ref, ga_ref, qi_ref, kwq_ref, k_ref, v_ref, kwk_ref, bt_ref, o_ref,
                       ki4_ref, kb_ref, vt_ref, *, seq, topk):
    i = pl.program_id(1)

    @pl.when(i == 0)
    def _():
        kh, kl = _split_bf16(kwk_ref[:, IDX_HEADS:IDX_HEADS + IDX_DIM])
        ki4_ref[...] = jnp.concatenate([kh, kl, kh, kl], axis=1).astype(BF16)
        kb_ref[...] = k_ref[...].astype(BF16)
        for kb in range(seq // QBLOCK):
            for h in range(ATT_HEADS):
                rows = slice(kb * QBLOCK, (kb + 1) * QBLOCK)
                cols = slice(h * HEAD_DIM, (h + 1) * HEAD_DIM)
                vt_ref[cols, rows] = v_ref[rows, cols].T.astype(BF16)

    n_blk = seq // QBLOCK
    per = n_blk // KEY_EXTENT_CLASSES
    for cls in range(KEY_EXTENT_CLASSES):
        pl.when(i // per == cls)(functools.partial(
            _dsa_prompt_block, i, (cls + 1) * per * QBLOCK, per, topk,
            rb_ref, q_ref, ga_ref, qi_ref, kwq_ref, bt_ref, o_ref, ki4_ref, kb_ref, vt_ref))


def _dsa_prompt_block(i, nk, per, topk, rb_ref, q_ref, ga_ref, qi_ref, kwq_ref, bt_ref, o_ref,
                      ki4_ref, kb_ref, vt_ref):
    qi = qi_ref[...]
    wi_t = kwq_ref[...].T * (IDX_HEADS ** -0.5)
    score = jnp.zeros((nk, QBLOCK), F32)
    for h0 in range(0, IDX_HEADS, 2):
        q4 = []
        for h in (h0, h0 + 1):
            qh, ql = _split_bf16(qi[:, h * IDX_DIM:(h + 1) * IDX_DIM])
            q4.append(jnp.concatenate([qh, qh, ql, ql], axis=1).astype(BF16))
        s = lax.dot_general(ki4_ref[0:nk, :], jnp.concatenate(q4, axis=0), NT_DIMS,
                            preferred_element_type=F32)
        score = score + jnp.maximum(s[:, 0:QBLOCK], 0.0) * wi_t[h0:h0 + 1, :]
        score = score + jnp.maximum(s[:, QBLOCK:2 * QBLOCK], 0.0) * wi_t[h0 + 1:h0 + 2, :]

    key_pos = lax.broadcasted_iota(I32, (nk, QBLOCK), 0)
    q_pos = i * QBLOCK + lax.broadcasted_iota(I32, (nk, QBLOCK), 1)
    score = jnp.where(key_pos <= q_pos, score, -jnp.inf)
    mask_add = jnp.where(score >= _kth_largest(score, topk, axis=0), 0.0, -jnp.inf)

    near = min(nk, (per + 1) * QBLOCK)
    key_blk = key_pos[nk - near:, :] // QBLOCK
    is_diag = key_blk == i
    is_prev = key_blk == i - 1
    reps = near // QBLOCK
    q = q_ref[...] * (HEAD_DIM ** -0.5)
    ga = ga_ref[...]
    for h in range(ATT_HEADS):
        cols = slice(h * HEAD_DIM, (h + 1) * HEAD_DIM)
        far_bias = rb_ref[REL_BUCKETS - 1, h]
        t_prev = jnp.concatenate([bt_ref[h, 0:QBLOCK, :]] * reps, axis=0)
        t_diag = jnp.concatenate([bt_ref[h, QBLOCK:2 * QBLOCK, :]] * reps, axis=0)
        bias = jnp.where(is_diag, t_diag, jnp.where(is_prev, t_prev, far_bias))
        if nk > near:
            bias = jnp.concatenate([jnp.full((nk - near, QBLOCK), far_bias, F32), bias], axis=0)
        logits = lax.dot_general(kb_ref[0:nk, cols], q[:, cols].astype(BF16), NT_DIMS,
                                 preferred_element_type=F32)
        logits = logits + bias + mask_add
        m = jnp.max(logits, axis=0, keepdims=True)
        p = jnp.exp(logits - m)
        l = jnp.sum(p, axis=0, keepdims=True)
        o_t = jnp.dot(vt_ref[cols, 0:nk], p.astype(BF16), preferred_element_type=F32)
        o_ref[:, cols] = (o_t / l).T * _silu(ga[:, cols])


def _dsa_prompt(p3, bias_tiles, rel_bias):
    b, s, _ = p3.shape
    topk = min(TOPK_MAX, s // 4)
    row = lambda col, w: pl.BlockSpec((None, QBLOCK, w), lambda bi, i: (bi, i, col // w))
    full = lambda col, w: pl.BlockSpec((None, s, w), lambda bi, i: (bi, 0, col // w))
    return pl.pallas_call(
        functools.partial(_dsa_prompt_kernel, seq=s, topk=topk),
        grid=(b, s // QBLOCK),
        in_specs=[pl.BlockSpec(memory_space=pltpu.SMEM),
                  row(COL_Q, ATT_W), row(COL_GA, ATT_W), row(COL_QI, IDX_W), row(COL_KIWI, LANES),
                  full(COL_K, ATT_W), full(COL_V, ATT_W), full(COL_KIWI, LANES),
                  pl.BlockSpec((ATT_HEADS, 2 * QBLOCK, QBLOCK), lambda bi, i: (0, 0, 0))],
        out_specs=pl.BlockSpec((None, QBLOCK, ATT_W), lambda bi, i: (bi, i, 0)),
        out_shape=jax.ShapeDtypeStruct((b, s, ATT_W), F32),
        scratch_shapes=[pltpu.VMEM((s, 4 * IDX_DIM), BF16),
                        pltpu.VMEM((s, ATT_W), BF16),
                        pltpu.VMEM((ATT_W, s), BF16)],
        compiler_params=_params("parallel", "arbitrary"),
        name="dsa_prompt",
    )(rel_bias, p3, p3, p3, p3, p3, p3, p3, bias_tiles)


def _sidx_kernel(pt_ref, q_ref, w_ref, *rest, nch):
    k_refs = rest[:IDX_PAGES_PER_STEP]
    kn_ref, s_ref, q4_ref = rest[IDX_PAGES_PER_STEP:]
    c = pl.program_id(1)

    @pl.when(c == 0)
    def _():
        qh, ql = _split_bf16(q_ref[...])
        q4_ref[...] = jnp.concatenate([qh, qh, ql, ql], axis=1).astype(BF16)

    def scores(ki_t):
        n = ki_t.shape[1]
        kh, kl = _split_bf16(ki_t)
        ki4_t = jnp.concatenate([kh, kl, kh, kl], axis=0).astype(BF16)
        s = jnp.dot(q4_ref[...], ki4_t, preferred_element_type=F32)
        s = jnp.maximum(s, 0.0) * (w_ref[...] * (IDX_HEADS ** -0.5))
        return jnp.sum(s.reshape(IDX_HEADS, SUBLANES, n), axis=0)

    @pl.when(c < nch)
    def _():
        s_ref[...] = scores(jnp.concatenate([r[...] for r in k_refs], axis=1))

    @pl.when(c == nch)
    def _():
        n = kn_ref.shape[1]
        sn = scores(kn_ref[...])
        t_key = lax.broadcasted_iota(I32, (SUBLANES, n), 1)
        t_q = lax.broadcasted_iota(I32, (SUBLANES, n), 0)
        s_ref[...] = jnp.full(s_ref.shape, -jnp.inf, F32)
        s_ref[:, 0:n] = jnp.where(t_key <= t_q, sn, -jnp.inf)


def _sattn_kernel(pt_ref, rb_ref, sf_ref, sc_ref, q_ref, ga_ref, *rest, nch, past, topk, page):
    k_refs = rest[:PAGES_PER_STEP]
    v_refs = rest[PAGES_PER_STEP:2 * PAGES_PER_STEP]
    kn_ref, vn_ref, o_ref, thr_ref, m_ref, l_ref, acc_ref = rest[2 * PAGES_PER_STEP:]
    c = pl.program_id(1)
    chunk = PAGES_PER_STEP * page

    @pl.when(c == 0)
    def _():
        thr_ref[...] = _kth_largest(sf_ref[...], topk)
        m_ref[...] = jnp.full(m_ref.shape, -jnp.inf, F32)
        l_ref[...] = jnp.zeros(l_ref.shape, F32)
        acc_ref[...] = jnp.zeros(acc_ref.shape, F32)

    def attend(k_of, v_of, sc, key_pos0):
        n = sc.shape[1]
        mask_add = jnp.where(sc >= thr_ref[...], 0.0, -jnp.inf)
        q_pos = past + lax.broadcasted_iota(I32, (SUBLANES, n), 0)
        key_pos = key_pos0 + lax.broadcasted_iota(I32, (SUBLANES, n), 1)
        bucket = _t5_bucket(q_pos - key_pos)
        q = q_ref[...]
        rows = []
        for h in range(ATT_HEADS):
            cols = slice(h * HEAD_DIM, (h + 1) * HEAD_DIM)
            bias = jnp.zeros((SUBLANES, n), F32)
            for b in range(REL_BUCKETS):
                bias = jnp.where(bucket == b, rb_ref[b, h], bias)
            lg = lax.dot_general(q[:, cols].astype(BF16), k_of(h).astype(BF16), NT_DIMS,
                                 preferred_element_type=F32)
            rows.append(lg * (HEAD_DIM ** -0.5) + bias + mask_add)
        logits = jnp.concatenate(rows, axis=0)
        m_old = m_ref[...]
        m_new = jnp.maximum(m_old, jnp.max(logits, axis=1, keepdims=True))
        m_safe = jnp.where(m_new == -jnp.inf, 0.0, m_new)
        alpha = jnp.exp(m_old - m_safe)
        p = jnp.exp(logits - m_safe[:, 0:1])
        l_ref[...] = alpha * l_ref[...] + jnp.sum(p, axis=1, keepdims=True)
        m_ref[...] = m_new
        for h in range(ATT_HEADS):
            cols = slice(h * HEAD_DIM, (h + 1) * HEAD_DIM)
            rs = slice(h * SUBLANES, (h + 1) * SUBLANES)
            acc_ref[:, cols] = alpha[rs] * acc_ref[:, cols] + jnp.dot(
                p[rs].astype(BF16), v_of(h).astype(BF16), preferred_element_type=F32)

    @pl.when(c < nch)
    def _():
        k_of = lambda h: jnp.concatenate([r[h] for r in k_refs], axis=0)
        v_of = lambda h: jnp.concatenate([r[h] for r in v_refs], axis=0)
        attend(k_of, v_of, sc_ref[...], c * chunk)

    @pl.when(c == nch)
    def _():
        n = kn_ref.shape[0]
        head = lambda ref: (lambda h: ref[:, h * HEAD_DIM:(h + 1) * HEAD_DIM])
        attend(head(kn_ref), head(vn_ref), sc_ref[:, 0:n], past)
        ga = ga_ref[...]
        l = l_ref[...]
        for h in range(ATT_HEADS):
            cols = slice(h * HEAD_DIM, (h + 1) * HEAD_DIM)
            o_ref[:, cols] = acc_ref[:, cols] / l[h * SUBLANES:(h + 1) * SUBLANES] * _silu(ga[:, cols])


def _dsa_sample(layer, ps3, cache_k_t, cache_v_t, cache_kidx_t, page_table, rel_bias):
    db, t, _ = ps3.shape
    n_pages = page_table.shape[1]
    page = cache_kidx_t.shape[3]
    past = n_pages * page
    topk = min(TOPK_MAX, (past + t) // 4)
    nch = n_pages // PAGES_PER_STEP
    chunk = PAGES_PER_STEP * page
    nch_idx = n_pages // IDX_PAGES_PER_STEP
    chunk_idx = IDX_PAGES_PER_STEP * page
    ntot = (nch_idx + 1) * chunk_idx
    assert chunk_idx % chunk == 0 and t == SUBLANES

    qi = ps3[:, :, COL_QI:COL_QI + IDX_W].reshape(db, t, IDX_HEADS, IDX_DIM)
    qi_hq = qi.transpose(0, 2, 1, 3).reshape(db, IDX_HEADS * t, IDX_DIM)
    wi_hq = ps3[:, :, COL_WI:COL_WI + IDX_HEADS].transpose(0, 2, 1).reshape(db, IDX_HEADS * t, 1)
    pad_rows = lambda a: jnp.pad(a, ((0, 0), (0, page - t), (0, 0)))
    ki_new_t = pad_rows(ps3[:, :, COL_KI:COL_KI + IDX_DIM]).transpose(0, 2, 1)
    k_new = pad_rows(ps3[:, :, COL_K:COL_K + ATT_W])
    v_new = pad_rows(ps3[:, :, COL_V:COL_V + ATT_W])
    q = ps3[:, :, COL_Q:COL_Q + ATT_W]
    ga = ps3[:, :, COL_GA:COL_GA + ATT_W]

    def page_spec(r, per_step, dims, **kw):
        def imap(b, c, pt):
            return (layer, pt[b, jnp.minimum(c * per_step + r, n_pages - 1)]) + (0,) * len(dims)
        return pl.BlockSpec((None, None) + dims, imap, **kw)

    per_b = lambda rows, w: pl.BlockSpec((None, rows, w), lambda b, c, pt: (b, 0, 0))

    scores = pl.pallas_call(
        functools.partial(_sidx_kernel, nch=nch_idx),
        grid_spec=pltpu.PrefetchScalarGridSpec(
            num_scalar_prefetch=1,
            grid=(db, nch_idx + 1),
            in_specs=[per_b(IDX_HEADS * t, IDX_DIM), per_b(IDX_HEADS * t, 1)]
                     + [page_spec(r, IDX_PAGES_PER_STEP, (IDX_DIM, page)) for r in range(IDX_PAGES_PER_STEP)]
                     + [per_b(IDX_DIM, page)],
            out_specs=pl.BlockSpec((None, t, chunk_idx), lambda b, c, pt: (b, 0, c)),
            scratch_shapes=[pltpu.VMEM((IDX_HEADS * t, 4 * IDX_DIM), BF16)]),
        out_shape=jax.ShapeDtypeStruct((db, t, ntot), F32),
        compiler_params=_params("parallel", "arbitrary"),
        name="dsa_sample_scores",
    )(page_table, qi_hq, wi_hq, *([cache_kidx_t] * IDX_PAGES_PER_STEP), ki_new_t)

    return pl.pallas_call(
        functools.partial(_sattn_kernel, nch=nch, past=past, topk=topk, page=page),
        grid_spec=pltpu.PrefetchScalarGridSpec(
            num_scalar_prefetch=1,
            grid=(db, nch + 1),
            in_specs=[pl.BlockSpec(memory_space=pltpu.SMEM),
                      per_b(t, ntot),
                      pl.BlockSpec((None, t, chunk), lambda b, c, pt: (b, 0, c)),
                      per_b(t, ATT_W), per_b(t, ATT_W)]
                     + [page_spec(r, PAGES_PER_STEP, (ATT_HEADS, page, HEAD_DIM),
                                  pipeline_mode=pl.Buffered(PAGE_BUFFERS)) for r in range(PAGES_PER_STEP)] * 2
                     + [per_b(page, ATT_W), per_b(page, ATT_W)],
            out_specs=per_b(t, ATT_W),
            scratch_shapes=[pltpu.VMEM((t, 1), F32),
                            pltpu.VMEM((ATT_HEADS * t, LANES), F32),
                            pltpu.VMEM((ATT_HEADS * t, LANES), F32),
                            pltpu.VMEM((t, ATT_W), F32)]),
        out_shape=jax.ShapeDtypeStruct((db, t, ATT_W), F32),
        compiler_params=_params("parallel", "arbitrary"),
        name="dsa_sample_attn",
    )(page_table, rel_bias, scores, scores, q, ga,
      *([cache_k_t] * PAGES_PER_STEP), *([cache_v_t] * PAGES_PER_STEP), k_new, v_new)


def _ssm_param_kernel(are_ref, aim_ref, ldt_ref, bre_ref, bim_ref, lbr_ref, lbi_ref, bbr_ref, bbi_ref):
    lam_re = jnp.minimum(are_ref[...], -1e-4)
    lam_im = aim_ref[...]
    step = jnp.exp(ldt_ref[...])
    mag = jnp.exp(lam_re * step)
    ang = lam_im * step
    lb_re = mag * jnp.cos(ang)
    lb_im = mag * jnp.sin(ang)
    nr = lb_re - 1.0
    ni = lb_im
    den = lam_re * lam_re + lam_im * lam_im
    coef_re = (nr * lam_re + ni * lam_im) / den
    coef_im = (ni * lam_re - nr * lam_im) / den
    lbr_ref[...] = lb_re
    lbi_ref[...] = lb_im
    for c in range(SSM_GROUP):
        br = bre_ref[c]
        bi = bim_ref[c]
        bbr_ref[c] = coef_re * br - coef_im * bi
        bbi_ref[c] = coef_re * bi + coef_im * br


def _ssm_params(a_re, a_im, log_dt, b_re, b_im):
    g, p, c = b_re.shape
    vm = pl.BlockSpec(memory_space=pltpu.VMEM)
    return pl.pallas_call(
        _ssm_param_kernel,
        in_specs=[vm] * 5,
        out_specs=[vm] * 4,
        out_shape=[jax.ShapeDtypeStruct((g, p), F32)] * 2 + [jax.ShapeDtypeStruct((c, g, p), F32)] * 2,
        name="ssm_params",
    )(a_re, a_im, log_dt.reshape(g, 1), b_re.transpose(2, 0, 1), b_im.transpose(2, 0, 1))


def _block_diag_slabs(m):
    g, a, b = m.shape
    eye = jnp.eye(SLAB_GROUPS, dtype=m.dtype)
    m4 = m.reshape(N_SLABS, SLAB_GROUPS, a, b)
    return jnp.einsum('sgab,gh->sgahb', m4, eye).reshape(N_SLABS, SLAB_GROUPS * a, SLAB_GROUPS * b)


def _scan_pitch(seg_len):
    return seg_len + 4 if seg_len % 8 == 0 else seg_len


def _s5_kernel(u_ref, x0r_ref, x0i_ref, lbr_ref, lbi_ref, wbu_ref, wyr_ref, wyi_ref, d_ref,
               z_ref, xr_ref, xi_ref, up_ref, zs_ref, *, seg_len, chain, hi):
    nc = SLAB_STATES // LANES
    pitch = _scan_pitch(seg_len)
    up_ref[...] = jnp.zeros(up_ref.shape, F32)
    for s in range(SCAN_ROWS):
        up_ref[pl.ds(s * pitch, seg_len), :] = u_ref[pl.ds(s * seg_len, seg_len), :]
    u = up_ref[...]
    bu = _dot(u, wbu_ref[...], hi)
    for j in range(2 * nc):
        zs_ref[j] = bu[:, j * LANES:(j + 1) * LANES]
    shape = (SCAN_ROWS, SLAB_STATES)
    lbr = jnp.broadcast_to(lbr_ref[...], shape)
    lbi = jnp.broadcast_to(lbi_ref[...], shape)

    def load(t, base):
        rows = pl.ds(t, SCAN_ROWS, stride=pitch)
        return jnp.concatenate([zs_ref[base + j, rows, :] for j in range(nc)], axis=1)

    def store(t, base, val):
        rows = pl.ds(t, SCAN_ROWS, stride=pitch)
        for j in range(nc):
            zs_ref[base + j, rows, :] = val[:, j * LANES:(j + 1) * LANES]

    def advance(t, carry):
        xr, xi = carry
        nr = lbr * xr - lbi * xi + load(t, 0)
        ni = lbr * xi + lbi * xr + load(t, nc)
        return nr, ni

    def advance_store(t, carry):
        nr, ni = advance(t, carry)
        store(t, 0, nr)
        store(t, nc, ni)
        return nr, ni

    unroll = min(seg_len, 8)
    x0r = x0r_ref[...]
    x0i = x0i_ref[...]
    if chain:
        zero = jnp.zeros(shape, F32)
        er, ei = lax.fori_loop(0, seg_len, advance, (zero, zero), unroll=unroll)
        pr, pi = lbr, lbi
        for _ in range(int(math.log2(seg_len))):
            pr, pi = pr * pr - pi * pi, 2.0 * pr * pi
        row = lax.broadcasted_iota(I32, shape, 0)
        sr = jnp.where(row == 0, x0r, 0.0)
        si = jnp.where(row == 0, x0i, 0.0)
        for s in range(SCAN_ROWS - 1):
            nr = er + pr * sr - pi * si
            ni = ei + pr * si + pi * sr
            sr = jnp.where(row == s + 1, pltpu.roll(nr, 1, axis=0), sr)
            si = jnp.where(row == s + 1, pltpu.roll(ni, 1, axis=0), si)
        x0r, x0i = sr, si
    xr, xi = lax.fori_loop(0, seg_len, advance_store, (x0r, x0i), unroll=unroll)
    xr_ref[...] = xr
    xi_ref[...] = xi
    y = d_ref[...] * u
    for j in range(nc):
        blk = slice(j * LANES, (j + 1) * LANES)
        y = y + _dot(zs_ref[j], wyr_ref[blk, :], hi) - _dot(zs_ref[nc + j], wyi_ref[blk, :], hi)
    up_ref[...] = _gelu_tanh(y)
    for s in range(SCAN_ROWS):
        z_ref[pl.ds(s * seg_len, seg_len), :] = up_ref[pl.ds(s * pitch, seg_len), :]


def _s5(p3, x0r, x0i, lbr, lbi, wbu, wyr, wyi, d, *, seg_len, chain, hi):
    nb, r, _ = p3.shape
    assert r == SCAN_ROWS * seg_len and (not chain or seg_len & (seg_len - 1) == 0)
    rp = SCAN_ROWS * _scan_pitch(seg_len)
    slab = lambda rows, w: pl.BlockSpec((None, rows, w), lambda b, s: (s, 0, 0))
    st = pl.BlockSpec((None, SCAN_ROWS, SLAB_STATES), lambda b, s: (b, 0, s))
    return pl.pallas_call(
        functools.partial(_s5_kernel, seg_len=seg_len, chain=chain, hi=hi),
        grid=(nb, N_SLABS),
        in_specs=[pl.BlockSpec((None, r, LANES), lambda b, s: (b, 0, COL_U // LANES + s)),
                  st, st, slab(1, SLAB_STATES), slab(1, SLAB_STATES),
                  slab(LANES, 2 * SLAB_STATES), slab(SLAB_STATES, LANES), slab(SLAB_STATES, LANES),
                  slab(1, LANES)],
        out_specs=[pl.BlockSpec((None, r, LANES), lambda b, s: (b, 0, s)), st, st],
        out_shape=[jax.ShapeDtypeStruct((nb, r, SSM_W), F32),
                   jax.ShapeDtypeStruct((nb, SCAN_ROWS, SSM_GROUPS * SSM_STATE), F32),
                   jax.ShapeDtypeStruct((nb, SCAN_ROWS, SSM_GROUPS * SSM_STATE), F32)],
        scratch_shapes=[pltpu.VMEM((rp, LANES), F32),
                        pltpu.VMEM((2 * SLAB_STATES // LANES, rp, LANES), F32)],
        compiler_params=_params("parallel", "parallel"),
        name="s5_scan",
    )(p3, x0r, x0i, lbr, lbi, wbu, wyr, wyi, d)


def _cross_kernel(q_ref, g_ref, mk_ref, mv_ref, o_ref):
    q = q_ref[...]
    g = g_ref[...]
    for h in range(CROSS_HEADS):
        cols = slice(h * CROSS_HD, (h + 1) * CROSS_HD)
        logits = lax.dot_general(q[:, cols].astype(BF16), mk_ref[:, cols].astype(BF16), NT_DIMS,
                                 preferred_element_type=F32) * (CROSS_HD ** -0.5)
        m = jnp.max(logits, axis=1, keepdims=True)
        p = jnp.exp(logits - m)
        l = jnp.sum(p, axis=1, keepdims=True)
        o = jnp.dot(p.astype(BF16), mv_ref[:, cols].astype(BF16), preferred_element_type=F32)
        o_ref[:, cols] = o / l * _silu(g[:, cols])


def _cross(p3, mk, mv, tq):
    nb, s, _ = p3.shape
    mem = mk.shape[1]
    qspec = lambda col: pl.BlockSpec((None, tq, CROSS_W), lambda b, i: (b, i, col // CROSS_W))
    mspec = pl.BlockSpec((None, mem, CROSS_W), lambda b, i: (b, 0, 0))
    return pl.pallas_call(
        _cross_kernel,
        grid=(nb, s // tq),
        in_specs=[qspec(COL_QC), qspec(COL_GC), mspec, mspec],
        out_specs=pl.BlockSpec((None, tq, CROSS_W), lambda b, i: (b, i, 0)),
        out_shape=jax.ShapeDtypeStruct((nb, s, CROSS_W), F32),
        compiler_params=_params("parallel", "parallel"),
        name="cross_attn",
    )(p3, p3, mk, mv)


def _merge_out_kernel(x_ref, a_ref, z_ref, c_ref, gs_ref, gm0_ref, gm1_ref, gm2_ref,
                      wglu_ref, bglu_ref, wba_ref, wbs_ref, wbc_ref, wout_ref, gfin_ref, o_ref, *, hi, final):
    gl = _dot(z_ref[...], wglu_ref[...], hi) + bglu_ref[...]
    s_out = gl[:, 0:SSM_W] * _sigmoid(gl[:, SSM_W:2 * SSM_W]) * _silu(gs_ref[...])
    merged = _sigmoid(gm0_ref[...]) * _dot(a_ref[...], wba_ref[...], hi)
    merged = merged + _sigmoid(gm1_ref[...]) * _dot(s_out, wbs_ref[...], hi)
    merged = merged + _sigmoid(gm2_ref[...]) * _dot(c_ref[...], wbc_ref[...], hi)
    y = x_ref[...] + _dot(merged, wout_ref[...], hi)
    if final:
        y = y * lax.rsqrt(jnp.mean(y * y, axis=-1, keepdims=True) + EPS) * gfin_ref[...]
    o_ref[...] = y


def _merge_out(x2d, a2d, z2d, c2d, p2d, w_glu, b_glu, w_ba, w_bs, w_bc, w_out, g_final, tm, hi, final):
    m, d = x2d.shape
    rows = lambda w, col=0: pl.BlockSpec((tm, w), lambda i: (i, col // w))
    const = lambda arr: pl.BlockSpec(arr.shape, lambda i: (0, 0), pipeline_mode=pl.Buffered(1))
    b2 = b_glu.reshape(1, -1)
    g2 = g_final.reshape(1, d)
    return pl.pallas_call(
        functools.partial(_merge_out_kernel, hi=hi, final=final),
        grid=(m // tm,),
        in_specs=[rows(d), rows(ATT_W), rows(SSM_W), rows(CROSS_W), rows(SSM_W, COL_GS),
                  rows(D_MODEL, COL_GM), rows(D_MODEL, COL_GM + D_MODEL), rows(D_MODEL, COL_GM + 2 * D_MODEL),
                  const(w_glu), const(b2), const(w_ba), const(w_bs), const(w_bc), const(w_out), const(g2)],
        out_specs=rows(d),
        out_shape=jax.ShapeDtypeStruct((m, d), F32),
        compiler_params=_params("parallel"),
        name="merge_out",
    )(x2d, a2d, z2d, c2d, p2d, p2d, p2d, p2d, w_glu, b2, w_ba, w_bs, w_bc, w_out, g2)


def _mixer_layer(x3, attend, x0r, x0i, mk, mv, norm_g, w_in_t, layer, w_glu, b_glu, w_ba, w_bs, w_bc, w_out,
                 ssm, g_final, *, tm, n_scan, seg_len, chain, hi, tq, final):
    nb, s, d = x3.shape
    m = nb * s
    x2d = x3.reshape(m, d)
    tr = min(tm, 256)
    hh, hl = _rms_split(x2d, norm_g, tr)
    p2d = _inproj(hh, hl, w_in_t, layer, tm, 0 if hi else PROJ_LO_TILES)
    p3 = p2d.reshape(nb, s, PROJ_W)
    a_out = attend(p3)
    z, xr, xi = _s5(p2d.reshape(n_scan, m // n_scan, PROJ_W), x0r, x0i, *ssm,
                    seg_len=seg_len, chain=chain, hi=hi)
    c_out = _cross(p3, mk, mv, tq)
    y = _merge_out(x2d, a_out.reshape(m, ATT_W), z.reshape(m, SSM_W), c_out.reshape(m, CROSS_W), p2d,
                   w_glu, b_glu, w_ba, w_bs, w_bc, w_out, g_final, tr, hi, final).reshape(nb, s, d)
    ki = p3[:, :, COL_KI:COL_KI + IDX_DIM]
    return y, (p3, ki, xr, xi)


def _stack_heads_kernel(*refs):
    *in_refs, o_ref = refs
    for l, r in enumerate(in_refs):
        for h in range(ATT_HEADS):
            o_ref[l, h] = r[:, h * HEAD_DIM:(h + 1) * HEAD_DIM]


def _stack_heads(p3_layers, col, ts):
    depth = len(p3_layers)
    b, s, _ = p3_layers[0].shape
    out = pl.pallas_call(
        _stack_heads_kernel,
        grid=(b, s // ts),
        in_specs=[pl.BlockSpec((None, ts, ATT_W), lambda bi, i: (bi, i, col // ATT_W))] * depth,
        out_specs=pl.BlockSpec((depth, None, ATT_HEADS, ts, HEAD_DIM), lambda bi, i: (0, bi, 0, i, 0)),
        out_shape=jax.ShapeDtypeStruct((depth, b, ATT_HEADS, s, HEAD_DIM), F32),
        compiler_params=_params("parallel", "parallel"),
        name="stack_heads",
    )(*p3_layers)
    return jnp.transpose(out, (0, 1, 3, 2, 4))


def kernel(x_prompt, x_sample, cache_k, cache_v, cache_kidx, cache_mem_k, cache_mem_v, state_ssm_re, state_ssm_im, page_table, mem_prompt, norm_g, w_in, w_branch_attn, w_branch_ssm, w_branch_cross, w_out, w_mem_kv, ssm_a_re, ssm_a_im, ssm_b_re, ssm_b_im, ssm_c_re, ssm_c_im, ssm_d, ssm_log_dt, w_glu, b_glu, rel_bias, final_norm_g):
    depth = w_in.shape[0]
    b, s, d = x_prompt.shape
    db, t, _ = x_sample.shape
    mem_len = mem_prompt.shape[1]
    n_state = SSM_GROUPS * SSM_STATE
    assert s == SCAN_ROWS * (s // SCAN_ROWS) and db == SCAN_ROWS

    bias_tiles = _bias_tiles(rel_bias)
    w_in_t = jnp.transpose(w_in, (0, 2, 1))
    cache_k_t = jnp.transpose(cache_k, (0, 1, 3, 2, 4))
    cache_v_t = jnp.transpose(cache_v, (0, 1, 3, 2, 4))
    cache_kidx_t = jnp.transpose(cache_kidx, (0, 1, 3, 2))
    zeros_state = jnp.zeros((b, SCAN_ROWS, n_state), F32)
    yp, ys = x_prompt, x_sample
    outs = [[] for _ in range(12)]
    p3_prompt = []
    for l in range(depth):
        lb_re, lb_im, bb_re, bb_im = _ssm_params(ssm_a_re[l], ssm_a_im[l], ssm_log_dt[l],
                                                 ssm_b_re[l], ssm_b_im[l])
        wbu = jnp.concatenate([_block_diag_slabs(bb_re.transpose(1, 0, 2)),
                               _block_diag_slabs(bb_im.transpose(1, 0, 2))], axis=2)
        wyr = _block_diag_slabs(ssm_c_re[l].transpose(0, 2, 1))
        wyi = _block_diag_slabs(ssm_c_im[l].transpose(0, 2, 1))
        ssm = (lb_re.reshape(N_SLABS, 1, SLAB_STATES), lb_im.reshape(N_SLABS, 1, SLAB_STATES),
               wbu, wyr, wyi, ssm_d[l].reshape(N_SLABS, 1, LANES))

        mkv = _memkv(mem_prompt.reshape(b * mem_len, d), w_mem_kv[l])
        mk_p = mkv[:, :CROSS_W].reshape(b, mem_len, CROSS_W)
        mv_p = mkv[:, CROSS_W:].reshape(b, mem_len, CROSS_W)
        attend_p = functools.partial(_dsa_prompt, bias_tiles=bias_tiles, rel_bias=rel_bias)
        yp, (p3_p, kip, xr, xi) = _mixer_layer(
            yp, attend_p, zeros_state, zeros_state, mk_p, mv_p, norm_g[l], w_in_t, l,
            w_glu[l].astype(BF16), b_glu[l], w_branch_attn[l].astype(BF16), w_branch_ssm[l].astype(BF16),
            w_branch_cross[l].astype(BF16), w_out[l].astype(BF16), ssm, final_norm_g,
            tm=2048, n_scan=b, seg_len=s // SCAN_ROWS, chain=True, hi=False, tq=512, final=l == depth - 1)
        p3_prompt.append(p3_p); outs[2].append(kip)
        outs[3].append(mk_p.reshape(b, mem_len, CROSS_HEADS, CROSS_HD))
        outs[4].append(mv_p.reshape(b, mem_len, CROSS_HEADS, CROSS_HD))
        outs[5].append(xr[:, SCAN_ROWS - 1].reshape(b, SSM_GROUPS, SSM_STATE))
        outs[6].append(xi[:, SCAN_ROWS - 1].reshape(b, SSM_GROUPS, SSM_STATE))

        attend_s = functools.partial(_dsa_sample, l, cache_k_t=cache_k_t, cache_v_t=cache_v_t,
                                     cache_kidx_t=cache_kidx_t, page_table=page_table, rel_bias=rel_bias)
        ys, (p3_s, kin, sr, si) = _mixer_layer(
            ys, attend_s,
            state_ssm_re[l].reshape(1, db, n_state), state_ssm_im[l].reshape(1, db, n_state),
            cache_mem_k[l].reshape(db, mem_len, CROSS_W), cache_mem_v[l].reshape(db, mem_len, CROSS_W),
            norm_g[l], w_in_t, l, w_glu[l], b_glu[l], w_branch_attn[l], w_branch_ssm[l],
            w_branch_cross[l], w_out[l], ssm, final_norm_g,
            tm=db * t, n_scan=1, seg_len=t, chain=False, hi=True, tq=t, final=l == depth - 1)
        outs[7].append(p3_s[:, :, COL_K:COL_K + ATT_W].reshape(db, t, ATT_HEADS, HEAD_DIM))
        outs[8].append(p3_s[:, :, COL_V:COL_V + ATT_W].reshape(db, t, ATT_HEADS, HEAD_DIM))
        outs[9].append(kin)
        outs[10].append(sr.reshape(db, SSM_GROUPS, SSM_STATE)); outs[11].append(si.reshape(db, SSM_GROUPS, SSM_STATE))

    stacked = [jnp.stack(o, axis=0) if o else None for o in outs]
    stacked[0] = _stack_heads(p3_prompt, COL_K, 512)
    stacked[1] = _stack_heads(p3_prompt, COL_V, 512)
    return (yp, ys) + tuple(stacked)
```
